```python
import math
import jax, jax.numpy as jnp
from jax import lax
import numpy as np

D_MODEL = 2048
BATCH = 4
SEQ = 2048
DEPTH = 4
DEC_BATCH = 32
DEC_SEQ = 4
PAST_LEN = 16384
PAGE_SIZE = 128

HEAD_DIM = 64
N_HEADS = 16
N_KV = 4
GROUP = N_HEADS // N_KV
D_Q = N_HEADS * HEAD_DIM
D_KV = N_KV * HEAD_DIM
WINDOW = 128
BLOCK = 128
ROPE_THETA = 10000.0
D_CONV = D_MODEL // 2
CONV_W = 3
D_FF = 5632
N_IN = D_Q + 2 * D_KV + 3 * D_CONV + 2 * D_MODEL
ALPHA = (2.0 * DEPTH) ** 0.25
BETA = (8.0 * DEPTH) ** -0.25
LN_EPS = 1e-5

kernel_name = "hybrid_swa_sink_shortconv_macaron_deepnorm_step"


def layer_norm(x, g, b):
    xf = x.astype(jnp.float32)
    mu = jnp.mean(xf, axis=-1, keepdims=True)
    var = jnp.mean(jnp.square(xf - mu), axis=-1, keepdims=True)
    y = (xf - mu) * lax.rsqrt(var + LN_EPS) * g.astype(jnp.float32) + b.astype(jnp.float32)
    return y.astype(x.dtype)


def swiglu(x, w_gu, w_down):
    g, u = jnp.split(x @ w_gu, 2, axis=-1)
    return (jax.nn.silu(g) * u) @ w_down


def rope(x, pos):
    inv_freq = ROPE_THETA ** (-jnp.arange(0, HEAD_DIM, 2, dtype=jnp.float32) / HEAD_DIM)
    ang = pos.astype(jnp.float32)[:, None] * inv_freq[None, :]
    cos = jnp.cos(ang)[None, :, None, :].astype(x.dtype)
    sin = jnp.sin(ang)[None, :, None, :].astype(x.dtype)
    x1, x2 = jnp.split(x, 2, axis=-1)
    return jnp.concatenate([x1 * cos - x2 * sin, x2 * cos + x1 * sin], axis=-1)


def sink_window_attention(q, k, v, q_pos, k_pos, sinks):
    s = jnp.einsum('bnqkgd,bnskd->bnkgqs', q, k).astype(jnp.float32) * (HEAD_DIM ** -0.5)
    diff = q_pos[:, :, None] - k_pos[:, None, :]
    mask = (diff >= 0) & (diff <= WINDOW) & (k_pos[:, None, :] >= 0)
    s = jnp.where(mask[None, :, None, None, :, :], s, -jnp.inf)
    sink = sinks.astype(jnp.float32).reshape(N_KV, GROUP)[None, None, :, :, None, None]
    sink = jnp.broadcast_to(sink, s.shape[:-1] + (1,))
    p = jax.nn.softmax(jnp.concatenate([s, sink], axis=-1), axis=-1)[..., :-1]
    return jnp.einsum('bnkgqs,bnskd->bnqkgd', p.astype(v.dtype), v)


def token_mix(x, past, w_in, sinks, conv_w, w_branch_attn, w_branch_conv, w_out):
    B, T, _ = x.shape
    cuts = np.cumsum([D_Q, D_KV, D_KV, D_CONV, D_CONV, D_CONV, D_MODEL]).tolist()
    q, k, v, cb, cc, ch, ga, gc = jnp.split(x @ w_in, cuts, axis=-1)
    q = q.reshape(B, T, N_HEADS, HEAD_DIM)
    k = k.reshape(B, T, N_KV, HEAD_DIM)
    v = v.reshape(B, T, N_KV, HEAD_DIM)
    offset = 0 if past is None else PAST_LEN
    pos = offset + jnp.arange(T, dtype=jnp.int32)
    q = rope(q, pos)
    k = rope(k, pos)
    u = cc * ch
    if past is None:
        nb = T // BLOCK
        qb = q.reshape(B, nb, BLOCK, N_KV, GROUP, HEAD_DIM)
        kb = k.reshape(B, nb, BLOCK, N_KV, HEAD_DIM)
        vb = v.reshape(B, nb, BLOCK, N_KV, HEAD_DIM)
        k_band = jnp.concatenate([jnp.concatenate([jnp.zeros_like(kb[:, :1]), kb[:, :-1]], 1), kb], 2)
        v_band = jnp.concatenate([jnp.concatenate([jnp.zeros_like(vb[:, :1]), vb[:, :-1]], 1), vb], 2)
        q_pos = pos.reshape(nb, BLOCK)
        k_pos = jnp.concatenate([q_pos - BLOCK, q_pos], axis=-1)
        attn = sink_window_attention(qb, k_band, v_band, q_pos, k_pos, sinks)
        new_k, new_v = k[:, -WINDOW:], v[:, -WINDOW:]
        u_pad = jnp.concatenate([jnp.zeros((B, CONV_W - 1, D_CONV), u.dtype), u], axis=1)
    else:
        k_past, v_past, conv_past = past
        k_all = jnp.concatenate([k_past, k], axis=1)
        v_all = jnp.concatenate([v_past, v], axis=1)
        q_pos = pos[None, :]
        k_pos = (PAST_LEN - WINDOW + jnp.arange(WINDOW + T, dtype=jnp.int32))[None, :]
        attn = sink_window_attention(q.reshape(B, 1, T, N_KV, GROUP, HEAD_DIM),
                                     k_all[:, None], v_all[:, None], q_pos, k_pos, sinks)
        new_k, new_v = k_all[:, -WINDOW:], v_all[:, -WINDOW:]
        u_pad = jnp.concatenate([conv_past, u], axis=1)
    attn = attn.reshape(B, T, D_Q)
    conv = sum(conv_w[j] * u_pad[:, j:j + T] for j in range(CONV_W))
    y_conv = cb * conv
    merged = jax.nn.sigmoid(ga) * (attn @ w_branch_attn) + jax.nn.sigmoid(gc) * (y_conv @ w_branch_conv)
    return merged @ w_out, new_k, new_v, u_pad[:, -(CONV_W - 1):]


def run_trunk(x, caches, ln_g, ln_b, w_in, sinks, conv_w, w_branch_attn, w_branch_conv, w_out,
              ffn1_gu, ffn1_down, ffn2_gu, ffn2_down):
    ks, vs, cs = [], [], []
    for l in range(DEPTH):
        past = None if caches is None else (caches[0][l], caches[1][l], caches[2][l])
        x = layer_norm(ALPHA * x + 0.5 * swiglu(x, ffn1_gu[l], ffn1_down[l]), ln_g[l, 0], ln_b[l, 0])
        m, nk, nv, nc = token_mix(x, past, w_in[l], sinks[l], conv_w[l],
                                  w_branch_attn[l], w_branch_conv[l], w_out[l])
        x = layer_norm(ALPHA * x + m, ln_g[l, 1], ln_b[l, 1])
        x = layer_norm(ALPHA * x + 0.5 * swiglu(x, ffn2_gu[l], ffn2_down[l]), ln_g[l, 2], ln_b[l, 2])
        ks.append(nk)
        vs.append(nv)
        cs.append(nc)
    return x, jnp.stack(ks), jnp.stack(vs), jnp.stack(cs)


def setup_inputs(seed: int = 0) -> dict:
    key = jax.random.key(seed)
    ks = jax.random.split(key, 20)
    f32 = jnp.float32
    nrm = lambda k, shape, scale: jax.random.normal(k, shape, f32) * scale
    return {
        "x_prompt": nrm(ks[0], (BATCH, SEQ, D_MODEL), 1.0),
        "x_sample": nrm(ks[1], (DEC_BATCH, DEC_SEQ, D_MODEL), 1.0),
        "cache_k_win": nrm(ks[2], (DEPTH, DEC_BATCH, WINDOW, N_KV, HEAD_DIM), 1.0),
        "cache_v_win": nrm(ks[3], (DEPTH, DEC_BATCH, WINDOW, N_KV, HEAD_DIM), 1.0),
        "state_conv": nrm(ks[4], (DEPTH, DEC_BATCH, CONV_W - 1, D_CONV), 1.0),
        "ln_g": 1.0 + nrm(ks[5], (DEPTH, 3, D_MODEL), 0.02),
        "ln_b": nrm(ks[6], (DEPTH, 3, D_MODEL), 0.02),
        "w_in": nrm(ks[7], (DEPTH, D_MODEL, N_IN), D_MODEL ** -0.5),
        "sinks": nrm(ks[8], (DEPTH, N_HEADS), 0.5),
        "conv_w": nrm(ks[9], (DEPTH, CONV_W, D_CONV), CONV_W ** -0.5),
        "w_branch_attn": nrm(ks[10], (DEPTH, D_Q, D_MODEL), D_Q ** -0.5),
        "w_branch_conv": nrm(ks[11], (DEPTH, D_CONV, D_MODEL), D_CONV ** -0.5),
        "w_out": nrm(ks[12], (DEPTH, D_MODEL, D_MODEL), BETA * D_MODEL ** -0.5),
        "ffn1_gu": nrm(ks[13], (DEPTH, D_MODEL, 2 * D_FF), D_MODEL ** -0.5),
        "ffn1_down": nrm(ks[14], (DEPTH, D_FF, D_MODEL), BETA * D_FF ** -0.5),
        "ffn2_gu": nrm(ks[15], (DEPTH, D_MODEL, 2 * D_FF), D_MODEL ** -0.5),
        "ffn2_down": nrm(ks[16], (DEPTH, D_FF, D_MODEL), BETA * D_FF ** -0.5),
    }


def reference(x_prompt, x_sample, cache_k_win, cache_v_win, state_conv, ln_g, ln_b, w_in, sinks,
              conv_w, w_branch_attn, w_branch_conv, w_out, ffn1_gu, ffn1_down, ffn2_gu, ffn2_down):
    y_prompt, k_win_prompt, v_win_prompt, conv_prompt = run_trunk(
        x_prompt, None, ln_g, ln_b, w_in, sinks, conv_w, w_branch_attn, w_branch_conv, w_out,
        ffn1_gu, ffn1_down, ffn2_gu, ffn2_down)
    y_sample, k_win_sample, v_win_sample, conv_sample = run_trunk(
        x_sample, (cache_k_win, cache_v_win, state_conv), ln_g, ln_b, w_in, sinks, conv_w,
        w_branch_attn, w_branch_conv, w_out, ffn1_gu, ffn1_down, ffn2_gu, ffn2_down)
    return (y_prompt, y_sample, k_win_prompt, v_win_prompt, conv_prompt,
            k_win_sample, v_win_sample, conv_sample)
```

```python
import functools

import jax
import jax.numpy as jnp
from jax import lax
from jax.experimental import pallas as pl
from jax.experimental.pallas import tpu as pltpu

F32 = jnp.float32
BF16 = jnp.bfloat16

PAST_LEN = 16384
WINDOW = 128
ROPE_THETA = 10000.0
LN_EPS = 1e-5
CONV_W = 3

LANES = 128
SUBLANES = 8
VMEM_LIMIT_BYTES = 60 * 1024 * 1024

KEY_SLOTS = 2 * WINDOW


def _params(n_axes):
    return pltpu.CompilerParams(
        dimension_semantics=("arbitrary",) * n_axes,
        vmem_limit_bytes=VMEM_LIMIT_BYTES,
    )


def _layer_norm(y, g, b):
    mu = jnp.mean(y, axis=-1, keepdims=True)
    d = y - mu
    var = jnp.mean(d * d, axis=-1, keepdims=True)
    return d * lax.rsqrt(var + LN_EPS) * g + b


def _ffn_ln_kernel(alpha, emit_bf16, x_ref, wg_ref, wu_ref, wd_ref, g_ref, b_ref, *rest):
    if emit_bf16:
        o_ref, ob_ref, xb_ref, acc_ref = rest
    else:
        o_ref, xb_ref, acc_ref = rest
        ob_ref = None
    j = pl.program_id(1)

    @pl.when(j == 0)
    def _():
        xb_ref[...] = x_ref[...].astype(BF16)
        acc_ref[...] = jnp.zeros_like(acc_ref)

    xb = xb_ref[...]
    g = jnp.dot(xb, wg_ref[...], preferred_element_type=F32)
    u = jnp.dot(xb, wu_ref[...], preferred_element_type=F32)
    h = (jax.nn.silu(g) * u).astype(BF16)
    acc_ref[...] += jnp.dot(h, wd_ref[...], preferred_element_type=F32)

    @pl.when(j == pl.num_programs(1) - 1)
    def _():
        y = alpha * x_ref[...] + 0.5 * acc_ref[...]
        out = _layer_norm(y, g_ref[...], b_ref[...])
        o_ref[...] = out
        if emit_bf16:
            ob_ref[...] = out.astype(BF16)


def _ffn_ln(x, w_gu, w_down, ln_g, ln_b, layer, ln_idx, alpha, *, tm, tf, emit_bf16):
    m, d = x.shape
    f = w_down.shape[1]
    nj = f // tf
    out_shape = [jax.ShapeDtypeStruct((m, d), F32)]
    out_specs = [pl.BlockSpec((tm, d), lambda i, j: (i, 0))]
    if emit_bf16:
        out_shape.append(jax.ShapeDtypeStruct((m, d), BF16))
        out_specs.append(pl.BlockSpec((tm, d), lambda i, j: (i, 0)))
    return pl.pallas_call(
        functools.partial(_ffn_ln_kernel, alpha, emit_bf16),
        grid=(m // tm, nj),
        in_specs=[
            pl.BlockSpec((tm, d), lambda i, j: (i, 0)),
            pl.BlockSpec((None, d, tf), lambda i, j: (layer, 0, j)),
            pl.BlockSpec((None, d, tf), lambda i, j: (layer, 0, nj + j)),
            pl.BlockSpec((None, tf, d), lambda i, j: (layer, j, 0)),
            pl.BlockSpec((None, 1, d), lambda i, j: (ln_idx, 0, 0)),
            pl.BlockSpec((None, 1, d), lambda i, j: (ln_idx, 0, 0)),
        ],
        out_specs=out_specs,
        out_shape=out_shape,
        scratch_shapes=[pltpu.VMEM((tm, d), BF16), pltpu.VMEM((tm, d), F32)],
        compiler_params=_params(2),
        name="ffn_ln",
    )(x, w_gu, w_gu, w_down, ln_g, ln_b)


def _rope_cols(x, cos, sin_signed, first_half):
    half = x.shape[1] // 4
    fwd = pltpu.roll(x, x.shape[1] - half, 1)
    bwd = pltpu.roll(x, half, 1)
    return x * cos + jnp.where(first_half, fwd, bwd) * sin_signed


def _qkv_kernel(d_q, d_kv, q_scale, xb_ref, w_ref, cos_ref, sin_ref, q_ref, k_ref, v_ref):
    qkv = jnp.dot(xb_ref[...], w_ref[...], preferred_element_type=F32)
    cos = cos_ref[...]
    sin = sin_ref[...]
    lane = lax.broadcasted_iota(jnp.int32, cos.shape, 1)
    first_half = (lane % (LANES // 2)) < (LANES // 4)
    for c in range(d_q // LANES):
        sl = slice(c * LANES, (c + 1) * LANES)
        q_ref[:, sl] = (_rope_cols(qkv[:, sl], cos, sin, first_half) * q_scale).astype(BF16)
    for c in range(d_kv // LANES):
        src = slice(d_q + c * LANES, d_q + (c + 1) * LANES)
        k_ref[:, c * LANES:(c + 1) * LANES] = _rope_cols(qkv[:, src], cos, sin, first_half)
    v_ref[...] = qkv[:, d_q + d_kv:]


def _qkv_proj(xb, w_in, cos_tab, sin_tab, layer, *, tm, d_q, d_kv, head_dim):
    m, d = xb.shape
    n_tab = cos_tab.shape[0] // tm
    n_cols = d_q + 2 * d_kv
    return pl.pallas_call(
        functools.partial(_qkv_kernel, d_q, d_kv, head_dim ** -0.5),
        grid=(m // tm,),
        in_specs=[
            pl.BlockSpec((tm, d), lambda i: (i, 0)),
            pl.BlockSpec((None, d, n_cols), lambda i: (layer, 0, 0)),
            pl.BlockSpec((tm, LANES), lambda i: (i % n_tab, 0)),
            pl.BlockSpec((tm, LANES), lambda i: (i % n_tab, 0)),
        ],
        out_specs=[
            pl.BlockSpec((tm, d_q), lambda i: (i, 0)),
            pl.BlockSpec((tm, d_kv), lambda i: (i, 0)),
            pl.BlockSpec((tm, d_kv), lambda i: (i, 0)),
        ],
        out_shape=[
            jax.ShapeDtypeStruct((m, d_q), BF16),
            jax.ShapeDtypeStruct((m, d_kv), F32),
            jax.ShapeDtypeStruct((m, d_kv), F32),
        ],
        compiler_params=_params(1),
        name="qkv_proj",
    )(xb, w_in, cos_tab, sin_tab)


def _lane_tile4(x128, want_high):
    lane = lax.broadcasted_iota(jnp.int32, x128.shape, 1)
    swapped = pltpu.roll(x128, LANES // 2, 1)
    low = lane < LANES // 2
    both = jnp.where(low, swapped, x128) if want_high else jnp.where(low, x128, swapped)
    return jnp.concatenate([both, both], axis=1)


def _band_attention(q, kband, vband, kpos0, sink_of, n_kv, group, head_dim):
    rows = q.shape[0]
    gw = group * head_dim
    r_idx = lax.broadcasted_iota(jnp.int32, (rows, KEY_SLOTS), 0)
    s_idx = lax.broadcasted_iota(jnp.int32, (rows, KEY_SLOTS), 1)
    diff = r_idx + WINDOW - s_idx
    valid = (diff >= 0) & (diff <= WINDOW) & (s_idx + kpos0 >= 0)
    head_of_lane = lax.broadcasted_iota(jnp.int32, (rows, gw), 1) // head_dim
    head_keep = [(head_of_lane == g).astype(F32).astype(BF16) for g in range(group)]
    outs = []
    for kh in range(n_kv):
        col = (kh * head_dim) // LANES
        high = ((kh * head_dim) % LANES) != 0
        kk = _lane_tile4(kband[:, col * LANES:(col + 1) * LANES], high).astype(BF16)
        vv = _lane_tile4(vband[:, col * LANES:(col + 1) * LANES], high).astype(BF16)
        qg = q[:, kh * gw:(kh + 1) * gw]
        out = jnp.zeros((rows, gw), F32)
        for g in range(group):
            qm = qg * head_keep[g]
            s = lax.dot_general(qm, kk, (((1,), (1,)), ((), ())), preferred_element_type=F32)
            s = jnp.where(valid, s, -jnp.inf)
            sink = sink_of(kh * group + g)
            mx = jnp.maximum(jnp.max(s, axis=1, keepdims=True), sink)
            p = jnp.exp(s - mx)
            den = jnp.sum(p, axis=1, keepdims=True) + jnp.exp(sink - mx)
            p = (p * (1.0 / den)).astype(BF16)
            o = jnp.dot(p, vv, preferred_element_type=F32)
            out = jnp.where(head_of_lane == g, o, out)
        outs.append(out)
    return jnp.concatenate(outs, axis=1)


def _attn_prompt_kernel(layer, n_kv, group, head_dim, sink_ref, q_ref, kp_ref, kc_ref, vp_ref, vc_ref,
                        o_ref):
    n = pl.program_id(1)
    kband = jnp.concatenate([kp_ref[...], kc_ref[...]], axis=0)
    vband = jnp.concatenate([vp_ref[...], vc_ref[...]], axis=0)
    kpos0 = (n - 1) * WINDOW
    out = _band_attention(q_ref[...], kband, vband, kpos0, lambda h: sink_ref[layer, h],
                          n_kv, group, head_dim)
    o_ref[...] = out.astype(BF16)


def _attn_prompt(q, k, v, sinks, layer, *, batch, n_kv, group, head_dim):
    m, d_q = q.shape
    d_kv = k.shape[1]
    nb = m // batch // WINDOW
    cur = lambda b, n: (b * nb + n, 0)
    prev = lambda b, n: (b * nb + jnp.maximum(n - 1, 0), 0)
    return pl.pallas_call(
        functools.partial(_attn_prompt_kernel, layer, n_kv, group, head_dim),
        grid=(batch, nb),
        in_specs=[
            pl.BlockSpec(memory_space=pltpu.SMEM),
            pl.BlockSpec((WINDOW, d_q), cur),
            pl.BlockSpec((WINDOW, d_kv), prev),
            pl.BlockSpec((WINDOW, d_kv), cur),
            pl.BlockSpec((WINDOW, d_kv), prev),
            pl.BlockSpec((WINDOW, d_kv), cur),
        ],
        out_specs=pl.BlockSpec((WINDOW, d_q), cur),
        out_shape=jax.ShapeDtypeStruct((m, d_q), BF16),
        compiler_params=_params(2),
        name="attn_prompt",
    )(sinks, q, k, k, v, v)


def _attn_sample_kernel(layer, n_kv, group, head_dim, sink_ref, q_ref, kc_ref, kn_ref, vc_ref, vn_ref,
                        o_ref):
    pad = jnp.zeros((KEY_SLOTS - WINDOW - kn_ref.shape[0], kn_ref.shape[1]), F32)
    kband = jnp.concatenate([kc_ref[...], kn_ref[...], pad], axis=0)
    vband = jnp.concatenate([vc_ref[...], vn_ref[...], pad], axis=0)
    out = _band_attention(q_ref[...], kband, vband, PAST_LEN - WINDOW, lambda h: sink_ref[layer, h],
                          n_kv, group, head_dim)
    o_ref[...] = out.astype(BF16)


def _attn_sample(q8, k_cache, k_new8, v_cache, v_new8, sinks, layer, *, n_kv, group, head_dim):
    nbatch, rows, d_q = q8.shape
    d_kv = k_new8.shape[2]
    per_b = lambda b: (b, 0, 0)
    cache = lambda b: (layer, b, 0, 0)
    return pl.pallas_call(
        functools.partial(_attn_sample_kernel, layer, n_kv, group, head_dim),
        grid=(nbatch,),
        in_specs=[
            pl.BlockSpec(memory_space=pltpu.SMEM),
            pl.BlockSpec((None, rows, d_q), per_b),
            pl.BlockSpec((None, None, WINDOW, d_kv), cache),
            pl.BlockSpec((None, rows, d_kv), per_b),
            pl.BlockSpec((None, None, WINDOW, d_kv), cache),
            pl.BlockSpec((None, rows, d_kv), per_b),
        ],
        out_specs=pl.BlockSpec((None, rows, d_q), per_b),
        out_shape=jax.ShapeDtypeStruct((nbatch, rows, d_q), BF16),
        compiler_params=_params(1),
        name="attn_sample",
    )(sinks, q8, k_cache, k_new8, v_cache, v_new8)


def _conv_kernel(tiles_per_seq, state_rows, seq_len, xb_ref, wb_ref, wc_ref, wh_ref, cw_ref, *rest):
    if seq_len is None:
        y_ref, st_ref, ubuf = rest
        f1_ref = f2_ref = None
    else:
        f1_ref, f2_ref, y_ref, st_ref, ubuf = rest
    tm = xb_ref.shape[0]
    xb = xb_ref[...]
    cb = jnp.dot(xb, wb_ref[...], preferred_element_type=F32)
    cc = jnp.dot(xb, wc_ref[...], preferred_element_type=F32)
    ch = jnp.dot(xb, wh_ref[...], preferred_element_type=F32)
    u = cc * ch
    first = (pl.program_id(1) % tiles_per_seq) == 0

    @pl.when(first)
    def _():
        ubuf[0:SUBLANES, :] = jnp.zeros((SUBLANES, ubuf.shape[1]), F32)

    @pl.when(jnp.logical_not(first))
    def _():
        ubuf[0:SUBLANES, :] = ubuf[tm:tm + SUBLANES, :]

    ubuf[SUBLANES:, :] = u
    u_m1 = ubuf[pl.ds(SUBLANES - 1, tm), :]
    u_m2 = ubuf[pl.ds(SUBLANES - 2, tm), :]
    if seq_len is not None:
        t_idx = lax.broadcasted_iota(jnp.int32, u.shape, 0) % seq_len
        u_m1 = jnp.where(t_idx >= 1, u_m1, f1_ref[...])
        u_m2 = jnp.where(t_idx >= 2, u_m2, f2_ref[...])
    cw = cw_ref[...]
    conv = cw[0:1, :] * u_m2 + cw[1:2, :] * u_m1 + cw[2:3, :] * u
    y_ref[...] = (cb * conv).astype(BF16)
    st_ref[...] = u[tm - state_rows:, :]


def _conv_proj(xb, w_in, conv_w, layer, fills, *, tm, tc, col0, d_conv, tiles_per_seq, state_rows,
               seq_len):
    m, d = xb.shape
    n_seq = (m // tm) // tiles_per_seq
    nc = d_conv // tc
    off = col0 // tc
    w_spec = lambda k: pl.BlockSpec((None, d, tc), lambda c, i: (layer, 0, off + k * nc + c))
    in_specs = [
        pl.BlockSpec((tm, d), lambda c, i: (i, 0)),
        w_spec(0), w_spec(1), w_spec(2),
        pl.BlockSpec((None, CONV_W, tc), lambda c, i: (layer, 0, c)),
    ]
    args = [xb, w_in, w_in, w_in, conv_w]
    if seq_len is not None:
        in_specs += [pl.BlockSpec((tm, tc), lambda c, i: (i, c))] * 2
        args += list(fills)
    return pl.pallas_call(
        functools.partial(_conv_kernel, tiles_per_seq, state_rows, seq_len),
        grid=(nc, m // tm),
        in_specs=in_specs,
        out_specs=[
            pl.BlockSpec((tm, tc), lambda c, i: (i, c)),
            pl.BlockSpec((state_rows, tc), lambda c, i: (i // tiles_per_seq, c)),
        ],
        out_shape=[
            jax.ShapeDtypeStruct((m, d_conv), BF16),
            jax.ShapeDtypeStruct((n_seq * state_rows, d_conv), F32),
        ],
        scratch_shapes=[pltpu.VMEM((tm + SUBLANES, tc), F32)],
        compiler_params=_params(2),
        name="conv_proj",
    )(*args)


def _mix_ln_kernel(alpha, x_ref, a_ref, c_ref, wga_ref, wgc_ref, wa_ref, wc_ref, wo_ref, g_ref, b_ref,
                   o_ref, xb_ref, acc_ref):
    j = pl.program_id(1)

    @pl.when(j == 0)
    def _():
        xb_ref[...] = x_ref[...].astype(BF16)
        acc_ref[...] = jnp.zeros_like(acc_ref)

    xb = xb_ref[...]
    ga = jnp.dot(xb, wga_ref[...], preferred_element_type=F32)
    gc = jnp.dot(xb, wgc_ref[...], preferred_element_type=F32)
    pa = jnp.dot(a_ref[...], wa_ref[...], preferred_element_type=F32)
    pc = jnp.dot(c_ref[...], wc_ref[...], preferred_element_type=F32)
    merged = (jax.nn.sigmoid(ga) * pa + jax.nn.sigmoid(gc) * pc).astype(BF16)
    acc_ref[...] += jnp.dot(merged, wo_ref[...], preferred_element_type=F32)

    @pl.when(j == pl.num_programs(1) - 1)
    def _():
        y = alpha * x_ref[...] + acc_ref[...]
        o_ref[...] = _layer_norm(y, g_ref[...], b_ref[...])


def _mix_ln(x, attn, yconv, w_in, w_a, w_c, w_o, ln_g, ln_b, layer, ln_idx, alpha, *, tm, tc, ga_col0,
            gc_col0):
    m, d = x.shape
    d_q = attn.shape[1]
    d_conv = yconv.shape[1]
    return pl.pallas_call(
        functools.partial(_mix_ln_kernel, alpha),
        grid=(m // tm, d // tc),
        in_specs=[
            pl.BlockSpec((tm, d), lambda i, j: (i, 0)),
            pl.BlockSpec((tm, d_q), lambda i, j: (i, 0)),
            pl.BlockSpec((tm, d_conv), lambda i, j: (i, 0)),
            pl.BlockSpec((None, d, tc), lambda i, j: (layer, 0, ga_col0 // tc + j)),
            pl.BlockSpec((None, d, tc), lambda i, j: (layer, 0, gc_col0 // tc + j)),
            pl.BlockSpec((None, d_q, tc), lambda i, j: (layer, 0, j)),
            pl.BlockSpec((None, d_conv, tc), lambda i, j: (layer, 0, j)),
            pl.BlockSpec((None, tc, d), lambda i, j: (layer, j, 0)),
            pl.BlockSpec((None, 1, d), lambda i, j: (ln_idx, 0, 0)),
            pl.BlockSpec((None, 1, d), lambda i, j: (ln_idx, 0, 0)),
        ],
        out_specs=pl.BlockSpec((tm, d), lambda i, j: (i, 0)),
        out_shape=jax.ShapeDtypeStruct((m, d), F32),
        scratch_shapes=[pltpu.VMEM((tm, d), BF16), pltpu.VMEM((tm, d), F32)],
        compiler_params=_params(2),
        name="mix_ln",
    )(x, attn, yconv, w_in, w_in, w_a, w_c, w_o, ln_g, ln_b)


def _rope_tables(pos, head_dim):
    inv_freq = ROPE_THETA ** (-jnp.arange(0, head_dim, 2, dtype=F32) / head_dim)
    ang = pos.astype(F32)[:, None] * inv_freq[None, :]
    cos = jnp.cos(ang)
    sin = jnp.sin(ang)
    reps = LANES // head_dim
    cos_tab = jnp.tile(jnp.concatenate([cos, cos], axis=1), (1, reps))
    sin_tab = jnp.tile(jnp.concatenate([-sin, sin], axis=1), (1, reps))
    return cos_tab, sin_tab


def _tiles(m):
    return min(m, 512)


def _run_trunk(x, caches, w, *, depth, dims):
    batch, seq, d = x.shape
    n_kv, head_dim, group, d_q, d_kv, d_conv = dims
    m = batch * seq
    alpha = (2.0 * depth) ** 0.25
    tm = _tiles(m)
    tf = 512
    tc = 512
    xf = x.reshape(m, d)
    conv_col0 = d_q + 2 * d_kv
    ga_col0 = conv_col0 + 3 * d_conv
    gc_col0 = ga_col0 + d
    if caches is None:
        pos = jnp.arange(seq, dtype=jnp.int32)
    else:
        pos = jnp.tile(PAST_LEN + jnp.arange(seq, dtype=jnp.int32), batch)
    cos_tab, sin_tab = _rope_tables(pos, head_dim)

    ks, vs, cs = [], [], []
    for l in range(depth):
        x1, x1b = _ffn_ln(xf, w["ffn1_gu"], w["ffn1_down"], w["ln_g"], w["ln_b"], l, 3 * l, alpha,
                          tm=tm, tf=tf, emit_bf16=True)
        q, k, v = _qkv_proj(x1b, w["w_in"], cos_tab, sin_tab, l, tm=tm, d_q=d_q, d_kv=d_kv,
                            head_dim=head_dim)
        if caches is None:
            attn = _attn_prompt(q, k, v, w["sinks"], l, batch=batch, n_kv=n_kv, group=group,
                                head_dim=head_dim)
            yconv, st = _conv_proj(x1b, w["w_in"], w["conv_w"], l, None, tm=tm, tc=tc, col0=conv_col0,
                                   d_conv=d_conv, tiles_per_seq=seq // tm, state_rows=SUBLANES,
                                   seq_len=None)
            new_k = k.reshape(batch, seq, n_kv, head_dim)[:, -WINDOW:]
            new_v = v.reshape(batch, seq, n_kv, head_dim)[:, -WINDOW:]
            new_c = st.reshape(batch, SUBLANES, d_conv)[:, -(CONV_W - 1):]
        else:
            cache_k, cache_v, state = caches
            pad_rows = lambda a: jnp.pad(a.reshape(batch, seq, -1), ((0, 0), (0, SUBLANES - seq), (0, 0)))
            attn8 = _attn_sample(pad_rows(q), cache_k.reshape(depth, batch, WINDOW, d_kv), pad_rows(k),
                                 cache_v.reshape(depth, batch, WINDOW, d_kv), pad_rows(v), w["sinks"], l,
                                 n_kv=n_kv, group=group, head_dim=head_dim)
            attn = attn8[:, :seq].reshape(m, d_q)
            st_l = state[l]
            zeros = jnp.zeros((batch, seq, d_conv), F32)
            fill1 = zeros.at[:, 0].set(st_l[:, 1]).reshape(m, d_conv)
            fill2 = zeros.at[:, 0].set(st_l[:, 0]).at[:, 1].set(st_l[:, 1]).reshape(m, d_conv)
            yconv, u = _conv_proj(x1b, w["w_in"], w["conv_w"], l, (fill1, fill2), tm=tm, tc=tc,
                                  col0=conv_col0, d_conv=d_conv, tiles_per_seq=1, state_rows=tm,
                                  seq_len=seq)
            k4 = k.reshape(batch, seq, n_kv, head_dim)
            v4 = v.reshape(batch, seq, n_kv, head_dim)
            new_k = jnp.concatenate([cache_k[l], k4], axis=1)[:, -WINDOW:]
            new_v = jnp.concatenate([cache_v[l], v4], axis=1)[:, -WINDOW:]
            new_c = jnp.concatenate([st_l, u.reshape(batch, seq, d_conv)], axis=1)[:, -(CONV_W - 1):]
        x2 = _mix_ln(x1, attn, yconv, w["w_in"], w["w_branch_attn"], w["w_branch_conv"], w["w_out"],
                     w["ln_g"], w["ln_b"], l, 3 * l + 1, alpha, tm=tm, tc=tc, ga_col0=ga_col0,
                     gc_col0=gc_col0)
        (xf,) = _ffn_ln(x2, w["ffn2_gu"], w["ffn2_down"], w["ln_g"], w["ln_b"], l, 3 * l + 2, alpha,
                        tm=tm, tf=tf, emit_bf16=False)
        ks.append(new_k)
        vs.append(new_v)
        cs.append(new_c)
    return xf.reshape(batch, seq, d), jnp.stack(ks), jnp.stack(vs), jnp.stack(cs)


def kernel(x_prompt, x_sample, cache_k_win, cache_v_win, state_conv, ln_g, ln_b, w_in, sinks, conv_w,
           w_branch_attn, w_branch_conv, w_out, ffn1_gu, ffn1_down, ffn2_gu, ffn2_down):
    depth = w_in.shape[0]
    d = x_prompt.shape[-1]
    n_kv, head_dim = cache_k_win.shape[-2:]
    d_q = w_branch_attn.shape[1]
    d_conv = conv_w.shape[-1]
    d_kv = n_kv * head_dim
    group = d_q // d_kv
    dims = (n_kv, head_dim, group, d_q, d_kv, d_conv)
    w = {
        "ln_g": ln_g.reshape(depth * 3, 1, d),
        "ln_b": ln_b.reshape(depth * 3, 1, d),
        "sinks": sinks,
        "conv_w": conv_w,
        "w_in": w_in.astype(BF16),
        "w_branch_attn": w_branch_attn.astype(BF16),
        "w_branch_conv": w_branch_conv.astype(BF16),
        "w_out": w_out.astype(BF16),
        "ffn1_gu": ffn1_gu.astype(BF16),
        "ffn1_down": ffn1_down.astype(BF16),
        "ffn2_gu": ffn2_gu.astype(BF16),
        "ffn2_down": ffn2_down.astype(BF16),
    }
    y_p, k_p, v_p, c_p = _run_trunk(x_prompt, None, w, depth=depth, dims=dims)
    y_s, k_s, v_s, c_s = _run_trunk(x_sample, (cache_k_win, cache_v_win, state_conv), w, depth=depth,
                                    dims=dims)
    return (y_p, y_s, k_p, v_p, c_p, k_s, v_s, c_s)
```

```python
import functools

import jax
import jax.numpy as jnp
from jax import lax
from jax.experimental import pallas as pl
from jax.experimental.pallas import tpu as pltpu

F32 = jnp.float32
BF16 = jnp.bfloat16

PAST_LEN = 16384
WINDOW = 128
ROPE_THETA = 10000.0
LN_EPS = 1e-5
CONV_W = 3

LANES = 128
SUBLANES = 8
VMEM_LIMIT_BYTES = 60 * 1024 * 1024

KEY_SLOTS = 2 * WINDOW


def _params(n_axes):
    return pltpu.CompilerParams(
        dimension_semantics=("arbitrary",) * n_axes,
        vmem_limit_bytes=VMEM_LIMIT_BYTES,
    )


def _layer_norm(y, g, b):
    mu = jnp.mean(y, axis=-1, keepdims=True)
    d = y - mu
    var = jnp.mean(d * d, axis=-1, keepdims=True)
    return d * lax.rsqrt(var + LN_EPS) * g + b


def _ffn_ln_kernel(alpha, emit_bf16, x_ref, wg_ref, wu_ref, wd_ref, g_ref, b_ref, *rest):
    if emit_bf16:
        o_ref, ob_ref, xb_ref, acc_ref = rest
    else:
        o_ref, xb_ref, acc_ref = rest
        ob_ref = None
    j = pl.program_id(1)

    @pl.when(j == 0)
    def _():
        xb_ref[...] = x_ref[...].astype(BF16)
        acc_ref[...] = jnp.zeros_like(acc_ref)

    xb = xb_ref[...]
    g = jnp.dot(xb, wg_ref[...], preferred_element_type=F32)
    u = jnp.dot(xb, wu_ref[...], preferred_element_type=F32)
    h = (jax.nn.silu(g) * u).astype(BF16)
    acc_ref[...] += jnp.dot(h, wd_ref[...], preferred_element_type=F32)

    @pl.when(j == pl.num_programs(1) - 1)
    def _():
        y = alpha * x_ref[...] + 0.5 * acc_ref[...]
        out = _layer_norm(y, g_ref[...], b_ref[...])
        o_ref[...] = out
        if emit_bf16:
            ob_ref[...] = out.astype(BF16)


def _ffn_ln(x, w_gu, w_down, ln_g, ln_b, layer, ln_idx, alpha, *, tm, tf, emit_bf16):
    m, d = x.shape
    f = w_down.shape[1]
    nj = f // tf
    out_shape = [jax.ShapeDtypeStruct((m, d), F32)]
    out_specs = [pl.BlockSpec((tm, d), lambda i, j: (i, 0))]
    if emit_bf16:
        out_shape.append(jax.ShapeDtypeStruct((m, d), BF16))
        out_specs.append(pl.BlockSpec((tm, d), lambda i, j: (i, 0)))
    return pl.pallas_call(
        functools.partial(_ffn_ln_kernel, alpha, emit_bf16),
        grid=(m // tm, nj),
        in_specs=[
            pl.BlockSpec((tm, d), lambda i, j: (i, 0)),
            pl.BlockSpec((None, d, tf), lambda i, j: (layer, 0, j)),
            pl.BlockSpec((None, d, tf), lambda i, j: (layer, 0, nj + j)),
            pl.BlockSpec((None, tf, d), lambda i, j: (layer, j, 0)),
            pl.BlockSpec((None, 1, d), lambda i, j: (ln_idx, 0, 0)),
            pl.BlockSpec((None, 1, d), lambda i, j: (ln_idx, 0, 0)),
        ],
        out_specs=out_specs,
        out_shape=out_shape,
        scratch_shapes=[pltpu.VMEM((tm, d), BF16), pltpu.VMEM((tm, d), F32)],
        compiler_params=_params(2),
        name="ffn_ln",
    )(x, w_gu, w_gu, w_down, ln_g, ln_b)


def _rope_cols(x, cos, sin_signed, first_half):
    half = x.shape[1] // 4
    fwd = pltpu.roll(x, x.shape[1] - half, 1)
    bwd = pltpu.roll(x, half, 1)
    return x * cos + jnp.where(first_half, fwd, bwd) * sin_signed


def _qkv_kernel(d_q, d_kv, q_scale, xb_ref, w_ref, cos_ref, sin_ref, q_ref, k_ref, v_ref):
    qkv = jnp.dot(xb_ref[...], w_ref[...], preferred_element_type=F32)
    cos = cos_ref[...]
    sin = sin_ref[...]
    lane = lax.broadcasted_iota(jnp.int32, cos.shape, 1)
    first_half = (lane % (LANES // 2)) < (LANES // 4)
    for c in range(d_q // LANES):
        sl = slice(c * LANES, (c + 1) * LANES)
        q_ref[:, sl] = (_rope_cols(qkv[:, sl], cos, sin, first_half) * q_scale).astype(BF16)
    for c in range(d_kv // LANES):
        src = slice(d_q + c * LANES, d_q + (c + 1) * LANES)
        k_ref[:, c * LANES:(c + 1) * LANES] = _rope_cols(qkv[:, src], cos, sin, first_half)
    v_ref[...] = qkv[:, d_q + d_kv:]


def _qkv_proj(xb, w_in, cos_tab, sin_tab, layer, *, tm, d_q, d_kv, head_dim):
    m, d = xb.shape
    n_tab = cos_tab.shape[0] // tm
    n_cols = d_q + 2 * d_kv
    return pl.pallas_call(
        functools.partial(_qkv_kernel, d_q, d_kv, head_dim ** -0.5),
        grid=(m // tm,),
        in_specs=[
            pl.BlockSpec((tm, d), lambda i: (i, 0)),
            pl.BlockSpec((None, d, n_cols), lambda i: (layer, 0, 0)),
            pl.BlockSpec((tm, LANES), lambda i: (i % n_tab, 0)),
            pl.BlockSpec((tm, LANES), lambda i: (i % n_tab, 0)),
        ],
        out_specs=[
            pl.BlockSpec((tm, d_q), lambda i: (i, 0)),
            pl.BlockSpec((tm, d_kv), lambda i: (i, 0)),
            pl.BlockSpec((tm, d_kv), lambda i: (i, 0)),
        ],
        out_shape=[
            jax.ShapeDtypeStruct((m, d_q), BF16),
            jax.ShapeDtypeStruct((m, d_kv), F32),
            jax.ShapeDtypeStruct((m, d_kv), F32),
        ],
        compiler_params=_params(1),
        name="qkv_proj",
    )(xb, w_in, cos_tab, sin_tab)


def _lane_tile4(x128, want_high):
    lane = lax.broadcasted_iota(jnp.int32, x128.shape, 1)
    swapped = pltpu.roll(x128, LANES // 2, 1)
    low = lane < LANES // 2
    both = jnp.where(low, swapped, x128) if want_high else jnp.where(low, x128, swapped)
    return jnp.concatenate([both, both], axis=1)


def _band_attention(problems, kpos0, sink, n_kv, group, head_dim):
    rows = problems[0][0].shape[0]
    n_heads = n_kv * group
    gw = group * head_dim
    r_idx = lax.broadcasted_iota(jnp.int32, (rows, KEY_SLOTS), 0)
    s_idx = lax.broadcasted_iota(jnp.int32, (rows, KEY_SLOTS), 1)
    diff = r_idx + WINDOW - s_idx
    valid = (diff >= 0) & (diff <= WINDOW) & (s_idx + kpos0 >= 0)
    head_of_lane = lax.broadcasted_iota(jnp.int32, (rows, gw), 1) // head_dim
    head_keep = [(head_of_lane == g).astype(F32) for g in range(group)]

    scores, values = [], []
    for q, kband, vband in problems:
        qf = q.astype(F32)
        for kh in range(n_kv):
            col = (kh * head_dim) // LANES
            high = ((kh * head_dim) % LANES) != 0
            kk = _lane_tile4(kband[:, col * LANES:(col + 1) * LANES], high).astype(BF16)
            values.append(_lane_tile4(vband[:, col * LANES:(col + 1) * LANES], high).astype(BF16))
            qg = qf[:, kh * gw:(kh + 1) * gw]
            qs = jnp.concatenate([qg * head_keep[g] for g in range(group)], axis=0).astype(BF16)
            scores.append(lax.dot_general(qs, kk, (((1,), (1,)), ((), ())), preferred_element_type=F32))
    s = jnp.concatenate(scores, axis=0).reshape(len(problems) * n_heads, rows, KEY_SLOTS)
    s = jnp.where(valid[None], s, -jnp.inf)
    sink_all = jnp.concatenate([sink] * len(problems), axis=0)
    mx = jnp.maximum(jnp.max(s, axis=-1, keepdims=True), sink_all)
    p = jnp.exp(s - mx)
    den = jnp.sum(p, axis=-1, keepdims=True) + jnp.exp(sink_all - mx)
    p = p * (1.0 / den)

    outs = []
    for i in range(len(problems)):
        slabs = []
        for kh in range(n_kv):
            c = i * n_kv + kh
            pg = p[c * group:(c + 1) * group].reshape(group * rows, KEY_SLOTS).astype(BF16)
            o = jnp.dot(pg, values[c], preferred_element_type=F32).reshape(group, rows, gw)
            out = o[0]
            for g in range(1, group):
                out = jnp.where(head_of_lane == g, o[g], out)
            slabs.append(out)
        outs.append(jnp.concatenate(slabs, axis=1))
    return outs


def _sink_column(sink_ref):
    return sink_ref[...][:, :, 0:1]


def _attn_prompt_kernel(n_kv, group, head_dim, sink_ref, q_ref, kp_ref, kc_ref, vp_ref, vc_ref, o_ref):
    n = pl.program_id(1)
    kband = jnp.concatenate([kp_ref[...], kc_ref[...]], axis=0)
    vband = jnp.concatenate([vp_ref[...], vc_ref[...]], axis=0)
    kpos0 = (n - 1) * WINDOW
    (out,) = _band_attention([(q_ref[...], kband, vband)], kpos0, _sink_column(sink_ref), n_kv, group,
                             head_dim)
    o_ref[...] = out.astype(BF16)


def _attn_prompt(q, k, v, sink_tab, layer, *, batch, n_kv, group, head_dim):
    m, d_q = q.shape
    d_kv = k.shape[1]
    nb = m // batch // WINDOW
    cur = lambda b, n: (b * nb + n, 0)
    prev = lambda b, n: (b * nb + jnp.maximum(n - 1, 0), 0)
    return pl.pallas_call(
        functools.partial(_attn_prompt_kernel, n_kv, group, head_dim),
        grid=(batch, nb),
        in_specs=[
            pl.BlockSpec((None, n_kv * group, 1, LANES), lambda b, n: (layer, 0, 0, 0)),
            pl.BlockSpec((WINDOW, d_q), cur),
            pl.BlockSpec((WINDOW, d_kv), prev),
            pl.BlockSpec((WINDOW, d_kv), cur),
            pl.BlockSpec((WINDOW, d_kv), prev),
            pl.BlockSpec((WINDOW, d_kv), cur),
        ],
        out_specs=pl.BlockSpec((WINDOW, d_q), cur),
        out_shape=jax.ShapeDtypeStruct((m, d_q), BF16),
        compiler_params=_params(2),
        name="attn_prompt",
    )(sink_tab, q, k, k, v, v)


def _attn_sample_kernel(n_kv, group, head_dim, sink_ref, q_ref, kc_ref, kn_ref, vc_ref, vn_ref, o_ref):
    n_seq, new_rows, d_kv = kn_ref.shape
    pad = jnp.zeros((KEY_SLOTS - WINDOW - new_rows, d_kv), F32)
    problems = []
    for i in range(n_seq):
        kband = jnp.concatenate([kc_ref[i], kn_ref[i], pad], axis=0)
        vband = jnp.concatenate([vc_ref[i], vn_ref[i], pad], axis=0)
        problems.append((q_ref[i], kband, vband))
    outs = _band_attention(problems, PAST_LEN - WINDOW, _sink_column(sink_ref), n_kv, group, head_dim)
    for i in range(n_seq):
        o_ref[i] = outs[i].astype(BF16)


def _attn_sample(q8, k_cache, k_new8, v_cache, v_new8, sink_tab, layer, *, seqs_per_step, n_kv, group,
                 head_dim):
    nbatch, rows, d_q = q8.shape
    d_kv = k_new8.shape[2]
    ns = seqs_per_step
    per_b = lambda b: (b, 0, 0)
    cache = lambda b: (layer, b, 0, 0)
    return pl.pallas_call(
        functools.partial(_attn_sample_kernel, n_kv, group, head_dim),
        grid=(nbatch // ns,),
        in_specs=[
            pl.BlockSpec((None, n_kv * group, 1, LANES), lambda b: (layer, 0, 0, 0)),
            pl.BlockSpec((ns, rows, d_q), per_b),
            pl.BlockSpec((None, ns, WINDOW, d_kv), cache),
            pl.BlockSpec((ns, rows, d_kv), per_b),
            pl.BlockSpec((None, ns, WINDOW, d_kv), cache),
            pl.BlockSpec((ns, rows, d_kv), per_b),
        ],
        out_specs=pl.BlockSpec((ns, rows, d_q), per_b),
        out_shape=jax.ShapeDtypeStruct((nbatch, rows, d_q), BF16),
        compiler_params=_params(1),
        name="attn_sample",
    )(sink_tab, q8, k_cache, k_new8, v_cache, v_new8)


def _conv_kernel(tiles_per_seq, state_rows, seq_len, xb_ref, wb_ref, wc_ref, wh_ref, cw_ref, *rest):
    if seq_len is None:
        y_ref, st_ref, ubuf = rest
        f1_ref = f2_ref = None
    else:
        f1_ref, f2_ref, y_ref, st_ref, ubuf = rest
    tm = xb_ref.shape[0]
    xb = xb_ref[...]
    cb = jnp.dot(xb, wb_ref[...], preferred_element_type=F32)
    cc = jnp.dot(xb, wc_ref[...], preferred_element_type=F32)
    ch = jnp.dot(xb, wh_ref[...], preferred_element_type=F32)
    u = cc * ch
    first = (pl.program_id(1) % tiles_per_seq) == 0

    @pl.when(first)
    def _():
        ubuf[0:SUBLANES, :] = jnp.zeros((SUBLANES, ubuf.shape[1]), F32)

    @pl.when(jnp.logical_not(first))
    def _():
        ubuf[0:SUBLANES, :] = ubuf[tm:tm + SUBLANES, :]

    ubuf[SUBLANES:, :] = u
    u_m1 = ubuf[pl.ds(SUBLANES - 1, tm), :]
    u_m2 = ubuf[pl.ds(SUBLANES - 2, tm), :]
    if seq_len is not None:
        t_idx = lax.broadcasted_iota(jnp.int32, u.shape, 0) % seq_len
        u_m1 = jnp.where(t_idx >= 1, u_m1, f1_ref[...])
        u_m2 = jnp.where(t_idx >= 2, u_m2, f2_ref[...])
    cw = cw_ref[...]
    conv = cw[0:1, :] * u_m2 + cw[1:2, :] * u_m1 + cw[2:3, :] * u
    y_ref[...] = (cb * conv).astype(BF16)
    st_ref[...] = u[tm - state_rows:, :]


def _conv_proj(xb, w_in, conv_w, layer, fills, *, tm, tc, col0, d_conv, tiles_per_seq, state_rows,
               seq_len):
    m, d = xb.shape
    n_seq = (m // tm) // tiles_per_seq
    nc = d_conv // tc
    off = col0 // tc
    w_spec = lambda k: pl.BlockSpec((None, d, tc), lambda c, i: (layer, 0, off + k * nc + c))
    in_specs = [
        pl.BlockSpec((tm, d), lambda c, i: (i, 0)),
        w_spec(0), w_spec(1), w_spec(2),
        pl.BlockSpec((None, CONV_W, tc), lambda c, i: (layer, 0, c)),
    ]
    args = [xb, w_in, w_in, w_in, conv_w]
    if seq_len is not None:
        in_specs += [pl.BlockSpec((tm, tc), lambda c, i: (i, c))] * 2
        args += list(fills)
    return pl.pallas_call(
        functools.partial(_conv_kernel, tiles_per_seq, state_rows, seq_len),
        grid=(nc, m // tm),
        in_specs=in_specs,
        out_specs=[
            pl.BlockSpec((tm, tc), lambda c, i: (i, c)),
            pl.BlockSpec((state_rows, tc), lambda c, i: (i // tiles_per_seq, c)),
        ],
        out_shape=[
            jax.ShapeDtypeStruct((m, d_conv), BF16),
            jax.ShapeDtypeStruct((n_seq * state_rows, d_conv), F32),
        ],
        scratch_shapes=[pltpu.VMEM((tm + SUBLANES, tc), F32)],
        compiler_params=_params(2),
        name="conv_proj",
    )(*args)


def _mix_ln_kernel(alpha, x_ref, a_ref, c_ref, wga_ref, wgc_ref, wa_ref, wc_ref, wo_ref, g_ref, b_ref,
                   o_ref, xb_ref, acc_ref):
    j = pl.program_id(1)

    @pl.when(j == 0)
    def _():
        xb_ref[...] = x_ref[...].astype(BF16)
        acc_ref[...] = jnp.zeros_like(acc_ref)

    xb = xb_ref[...]
    ga = jnp.dot(xb, wga_ref[...], preferred_element_type=F32)
    gc = jnp.dot(xb, wgc_ref[...], preferred_element_type=F32)
    pa = jnp.dot(a_ref[...], wa_ref[...], preferred_element_type=F32)
    pc = jnp.dot(c_ref[...], wc_ref[...], preferred_element_type=F32)
    merged = (jax.nn.sigmoid(ga) * pa + jax.nn.sigmoid(gc) * pc).astype(BF16)
    acc_ref[...] += jnp.dot(merged, wo_ref[...], preferred_element_type=F32)

    @pl.when(j == pl.num_programs(1) - 1)
    def _():
        y = alpha * x_ref[...] + acc_ref[...]
        o_ref[...] = _layer_norm(y, g_ref[...], b_ref[...])


def _mix_ln(x, attn, yconv, w_in, w_a, w_c, w_o, ln_g, ln_b, layer, ln_idx, alpha, *, tm, tc, ga_col0,
            gc_col0):
    m, d = x.shape
    d_q = attn.shape[1]
    d_conv = yconv.shape[1]
    return pl.pallas_call(
        functools.partial(_mix_ln_kernel, alpha),
        grid=(m // tm, d // tc),
        in_specs=[
            pl.BlockSpec((tm, d), lambda i, j: (i, 0)),
            pl.BlockSpec((tm, d_q), lambda i, j: (i, 0)),
            pl.BlockSpec((tm, d_conv), lambda i, j: (i, 0)),
            pl.BlockSpec((None, d, tc), lambda i, j: (layer, 0, ga_col0 // tc + j)),
            pl.BlockSpec((None, d, tc), lambda i, j: (layer, 0, gc_col0 // tc + j)),
            pl.BlockSpec((None, d_q, tc), lambda i, j: (layer, 0, j)),
            pl.BlockSpec((None, d_conv, tc), lambda i, j: (layer, 0, j)),
            pl.BlockSpec((None, tc, d), lambda i, j: (layer, j, 0)),
            pl.BlockSpec((None, 1, d), lambda i, j: (ln_idx, 0, 0)),
            pl.BlockSpec((None, 1, d), lambda i, j: (ln_idx, 0, 0)),
        ],
        out_specs=pl.BlockSpec((tm, d), lambda i, j: (i, 0)),
        out_shape=jax.ShapeDtypeStruct((m, d), F32),
        scratch_shapes=[pltpu.VMEM((tm, d), BF16), pltpu.VMEM((tm, d), F32)],
        compiler_params=_params(2),
        name="mix_ln",
    )(x, attn, yconv, w_in, w_in, w_a, w_c, w_o, ln_g, ln_b)


def _rope_tables(pos, head_dim):
    inv_freq = ROPE_THETA ** (-jnp.arange(0, head_dim, 2, dtype=F32) / head_dim)
    ang = pos.astype(F32)[:, None] * inv_freq[None, :]
    cos = jnp.cos(ang)
    sin = jnp.sin(ang)
    reps = LANES // head_dim
    cos_tab = jnp.tile(jnp.concatenate([cos, cos], axis=1), (1, reps))
    sin_tab = jnp.tile(jnp.concatenate([-sin, sin], axis=1), (1, reps))
    return cos_tab, sin_tab


SAMPLE_SEQS_PER_STEP = 8


def _largest_divisor(n, cap):
    return max(k for k in range(1, cap + 1) if n % k == 0)


def _tiles(m):
    return min(m, 512)


def _run_trunk(x, caches, w, *, depth, dims):
    batch, seq, d = x.shape
    n_kv, head_dim, group, d_q, d_kv, d_conv = dims
    m = batch * seq
    alpha = (2.0 * depth) ** 0.25
    tm = _tiles(m)
    tf = 512
    tc = 512
    xf = x.reshape(m, d)
    conv_col0 = d_q + 2 * d_kv
    ga_col0 = conv_col0 + 3 * d_conv
    gc_col0 = ga_col0 + d
    if caches is None:
        pos = jnp.arange(seq, dtype=jnp.int32)
    else:
        pos = jnp.tile(PAST_LEN + jnp.arange(seq, dtype=jnp.int32), batch)
    cos_tab, sin_tab = _rope_tables(pos, head_dim)

    ks, vs, cs = [], [], []
    for l in range(depth):
        x1, x1b = _ffn_ln(xf, w["ffn1_gu"], w["ffn1_down"], w["ln_g"], w["ln_b"], l, 3 * l, alpha,
                          tm=tm, tf=tf, emit_bf16=True)
        q, k, v = _qkv_proj(x1b, w["w_in"], cos_tab, sin_tab, l, tm=tm, d_q=d_q, d_kv=d_kv,
                            head_dim=head_dim)
        if caches is None:
            attn = _attn_prompt(q, k, v, w["sinks"], l, batch=batch, n_kv=n_kv, group=group,
                                head_dim=head_dim)
            yconv, st = _conv_proj(x1b, w["w_in"], w["conv_w"], l, None, tm=tm, tc=tc, col0=conv_col0,
                                   d_conv=d_conv, tiles_per_seq=seq // tm, state_rows=SUBLANES,
                                   seq_len=None)
            new_k = k.reshape(batch, seq, n_kv, head_dim)[:, -WINDOW:]
            new_v = v.reshape(batch, seq, n_kv, head_dim)[:, -WINDOW:]
            new_c = st.reshape(batch, SUBLANES, d_conv)[:, -(CONV_W - 1):]
        else:
            cache_k, cache_v, state = caches
            pad_rows = lambda a: jnp.pad(a.reshape(batch, seq, -1), ((0, 0), (0, SUBLANES - seq), (0, 0)))
            attn8 = _attn_sample(pad_rows(q), cache_k.reshape(depth, batch, WINDOW, d_kv), pad_rows(k),
                                 cache_v.reshape(depth, batch, WINDOW, d_kv), pad_rows(v), w["sinks"], l,
                                 seqs_per_step=_largest_divisor(batch, SAMPLE_SEQS_PER_STEP),
                                 n_kv=n_kv, group=group, head_dim=head_dim)
            attn = attn8[:, :seq].reshape(m, d_q)
            st_l = state[l]
            zeros = jnp.zeros((batch, seq, d_conv), F32)
            fill1 = zeros.at[:, 0].set(st_l[:, 1]).reshape(m, d_conv)
            fill2 = zeros.at[:, 0].set(st_l[:, 0]).at[:, 1].set(st_l[:, 1]).reshape(m, d_conv)
            yconv, u = _conv_proj(x1b, w["w_in"], w["conv_w"], l, (fill1, fill2), tm=tm, tc=tc,
                                  col0=conv_col0, d_conv=d_conv, tiles_per_seq=1, state_rows=tm,
                                  seq_len=seq)
            k4 = k.reshape(batch, seq, n_kv, head_dim)
            v4 = v.reshape(batch, seq, n_kv, head_dim)
            new_k = jnp.concatenate([cache_k[l], k4], axis=1)[:, -WINDOW:]
            new_v = jnp.concatenate([cache_v[l], v4], axis=1)[:, -WINDOW:]
            new_c = jnp.concatenate([st_l, u.reshape(batch, seq, d_conv)], axis=1)[:, -(CONV_W - 1):]
        x2 = _mix_ln(x1, attn, yconv, w["w_in"], w["w_branch_attn"], w["w_branch_conv"], w["w_out"],
                     w["ln_g"], w["ln_b"], l, 3 * l + 1, alpha, tm=tm, tc=tc, ga_col0=ga_col0,
                     gc_col0=gc_col0)
        (xf,) = _ffn_ln(x2, w["ffn2_gu"], w["ffn2_down"], w["ln_g"], w["ln_b"], l, 3 * l + 2, alpha,
                        tm=tm, tf=tf, emit_bf16=False)
        ks.append(new_k)
        vs.append(new_v)
        cs.append(new_c)
    return xf.reshape(batch, seq, d), jnp.stack(ks), jnp.stack(vs), jnp.stack(cs)


def kernel(x_prompt, x_sample, cache_k_win, cache_v_win, state_conv, ln_g, ln_b, w_in, sinks, conv_w,
           w_branch_attn, w_branch_conv, w_out, ffn1_gu, ffn1_down, ffn2_gu, ffn2_down):
    depth = w_in.shape[0]
    d = x_prompt.shape[-1]
    n_kv, head_dim = cache_k_win.shape[-2:]
    d_q = w_branch_attn.shape[1]
    d_conv = conv_w.shape[-1]
    d_kv = n_kv * head_dim
    group = d_q // d_kv
    dims = (n_kv, head_dim, group, d_q, d_kv, d_conv)
    w = {
        "ln_g": ln_g.reshape(depth * 3, 1, d),
        "ln_b": ln_b.reshape(depth * 3, 1, d),
        "sinks": jnp.broadcast_to(sinks[:, :, None, None], sinks.shape + (1, LANES)),
        "conv_w": conv_w,
        "w_in": w_in.astype(BF16),
        "w_branch_attn": w_branch_attn.astype(BF16),
        "w_branch_conv": w_branch_conv.astype(BF16),
        "w_out": w_out.astype(BF16),
        "ffn1_gu": ffn1_gu.astype(BF16),
        "ffn1_down": ffn1_down.astype(BF16),
        "ffn2_gu": ffn2_gu.astype(BF16),
        "ffn2_down": ffn2_down.astype(BF16),
    }
    y_p, k_p, v_p, c_p = _run_trunk(x_prompt, None, w, depth=depth, dims=dims)
    y_s, k_s, v_s, c_s = _run_trunk(x_sample, (cache_k_win, cache_v_win, state_conv), w, depth=depth,
                                    dims=dims)
    return (y_p, y_s, k_p, v_p, c_p, k_s, v_s, c_s)
```

```python
import functools

import jax
import jax.numpy as jnp
from jax import lax
from jax.experimental import pallas as pl
from jax.experimental.pallas import tpu as pltpu

F32 = jnp.float32
BF16 = jnp.bfloat16

PAST_LEN = 16384
WINDOW = 128
ROPE_THETA = 10000.0
LN_EPS = 1e-5
CONV_W = 3

LANES = 128
SUBLANES = 8
BF16_ROWS = 16
VMEM_LIMIT_BYTES = 60 * 1024 * 1024

KEY_SLOTS = 2 * WINDOW
MAX_ROW_TILE = 640
SAMPLE_SEQS_PER_STEP = 8


def _params(n_axes):
    return pltpu.CompilerParams(
        dimension_semantics=("arbitrary",) * n_axes,
        vmem_limit_bytes=VMEM_LIMIT_BYTES,
    )


def _layer_norm(y, g, b):
    mu = jnp.mean(y, axis=-1, keepdims=True)
    d = y - mu
    var = jnp.mean(d * d, axis=-1, keepdims=True)
    return d * lax.rsqrt(var + LN_EPS) * g + b


def _ffn_ln_kernel(alpha, emit_bf16, x_ref, wg_ref, wu_ref, wd_ref, g_ref, b_ref, *rest):
    if emit_bf16:
        o_ref, ob_ref, xb_ref, acc_ref = rest
    else:
        o_ref, xb_ref, acc_ref = rest
        ob_ref = None
    j = pl.program_id(1)

    @pl.when(j == 0)
    def _():
        xb_ref[...] = x_ref[...].astype(BF16)
        acc_ref[...] = jnp.zeros_like(acc_ref)

    xb = xb_ref[...]
    g = jnp.dot(xb, wg_ref[...], preferred_element_type=F32)
    u = jnp.dot(xb, wu_ref[...], preferred_element_type=F32)
    h = (jax.nn.silu(g) * u).astype(BF16)
    acc_ref[...] += jnp.dot(h, wd_ref[...], preferred_element_type=F32)

    @pl.when(j == pl.num_programs(1) - 1)
    def _():
        y = alpha * x_ref[...] + 0.5 * acc_ref[...]
        out = _layer_norm(y, g_ref[...], b_ref[...])
        o_ref[...] = out
        if emit_bf16:
            ob_ref[...] = out.astype(BF16)


def _ffn_ln(x, w_gu, w_down, ln_g, ln_b, layer, ln_idx, alpha, *, tm, tf, emit_bf16):
    m, d = x.shape
    f = w_down.shape[1]
    nj = f // tf
    out_shape = [jax.ShapeDtypeStruct((m, d), F32)]
    out_specs = [pl.BlockSpec((tm, d), lambda i, j: (i, 0))]
    if emit_bf16:
        out_shape.append(jax.ShapeDtypeStruct((m, d), BF16))
        out_specs.append(pl.BlockSpec((tm, d), lambda i, j: (i, 0)))
    return pl.pallas_call(
        functools.partial(_ffn_ln_kernel, alpha, emit_bf16),
        grid=(m // tm, nj),
        in_specs=[
            pl.BlockSpec((tm, d), lambda i, j: (i, 0)),
            pl.BlockSpec((None, d, tf), lambda i, j: (layer, 0, j)),
            pl.BlockSpec((None, d, tf), lambda i, j: (layer, 0, nj + j)),
            pl.BlockSpec((None, tf, d), lambda i, j: (layer, j, 0)),
            pl.BlockSpec((None, 1, d), lambda i, j: (ln_idx, 0, 0)),
            pl.BlockSpec((None, 1, d), lambda i, j: (ln_idx, 0, 0)),
        ],
        out_specs=out_specs,
        out_shape=out_shape,
        scratch_shapes=[pltpu.VMEM((tm, d), BF16), pltpu.VMEM((tm, d), F32)],
        compiler_params=_params(2),
        name="ffn_ln",
    )(x, w_gu, w_gu, w_down, ln_g, ln_b)


def _rope_cols(x, cos, sin_signed, first_half, half):
    fwd = pltpu.roll(x, x.shape[1] - half, 1)
    bwd = pltpu.roll(x, half, 1)
    return x * cos + jnp.where(first_half, fwd, bwd) * sin_signed


def _qkv_kernel(d_q, d_kv, head_dim, xb_ref, w_ref, cos_ref, sin_ref, q_ref, k_ref, v_ref):
    qkv = jnp.dot(xb_ref[...], w_ref[...], preferred_element_type=F32)
    cos = cos_ref[...]
    sin = sin_ref[...]
    half = head_dim // 2
    lane = lax.broadcasted_iota(jnp.int32, cos.shape, 1)
    first_half = (lane % head_dim) < half
    q_scale = head_dim ** -0.5
    for c in range(d_q // LANES):
        sl = slice(c * LANES, (c + 1) * LANES)
        q_ref[:, sl] = (_rope_cols(qkv[:, sl], cos, sin, first_half, half) * q_scale).astype(BF16)
    for c in range(d_kv // LANES):
        src = slice(d_q + c * LANES, d_q + (c + 1) * LANES)
        k_ref[:, c * LANES:(c + 1) * LANES] = _rope_cols(qkv[:, src], cos, sin, first_half, half)
    v_ref[...] = qkv[:, d_q + d_kv:]


def _qkv_proj(xb, w_in, cos_tab, sin_tab, layer, *, tm, d_q, d_kv, head_dim):
    m, d = xb.shape
    n_cols = d_q + 2 * d_kv
    return pl.pallas_call(
        functools.partial(_qkv_kernel, d_q, d_kv, head_dim),
        grid=(m // tm,),
        in_specs=[
            pl.BlockSpec((tm, d), lambda i: (i, 0)),
            pl.BlockSpec((None, d, n_cols), lambda i: (layer, 0, 0)),
            pl.BlockSpec((tm, LANES), lambda i: (i, 0)),
            pl.BlockSpec((tm, LANES), lambda i: (i, 0)),
        ],
        out_specs=[
            pl.BlockSpec((tm, d_q), lambda i: (i, 0)),
            pl.BlockSpec((tm, d_kv), lambda i: (i, 0)),
            pl.BlockSpec((tm, d_kv), lambda i: (i, 0)),
        ],
        out_shape=[
            jax.ShapeDtypeStruct((m, d_q), BF16),
            jax.ShapeDtypeStruct((m, d_kv), F32),
            jax.ShapeDtypeStruct((m, d_kv), F32),
        ],
        compiler_params=_params(1),
        name="qkv_proj",
    )(xb, w_in, cos_tab, sin_tab)


def _lane_tile4(x128, want_high):
    lane = lax.broadcasted_iota(jnp.int32, x128.shape, 1)
    swapped = pltpu.roll(x128, LANES // 2, 1)
    low = lane < LANES // 2
    both = jnp.where(low, swapped, x128) if want_high else jnp.where(low, x128, swapped)
    return jnp.concatenate([both, both], axis=1)


def _band_attention(problems, kpos0, sink, n_kv, group, head_dim):
    rows = problems[0][0].shape[0]
    n_heads = n_kv * group
    gw = group * head_dim
    r_idx = lax.broadcasted_iota(jnp.int32, (rows, KEY_SLOTS), 0)
    s_idx = lax.broadcasted_iota(jnp.int32, (rows, KEY_SLOTS), 1)
    in_past = s_idx + kpos0 >= 0
    valid_of_row0 = {}
    head_of_lane = lax.broadcasted_iota(jnp.int32, (rows, gw), 1) // head_dim
    head_keep = [(head_of_lane == g).astype(F32) for g in range(group)]

    scores, values = [], []
    for q, kband, vband, row0 in problems:
        if row0 not in valid_of_row0:
            diff = r_idx - row0 + WINDOW - s_idx
            valid_of_row0[row0] = (diff >= 0) & (diff <= WINDOW) & in_past
        valid = valid_of_row0[row0]
        for kh in range(n_kv):
            col = (kh * head_dim) // LANES
            high = ((kh * head_dim) % LANES) != 0
            kk = _lane_tile4(kband[:, col * LANES:(col + 1) * LANES], high).astype(BF16)
            values.append(_lane_tile4(vband[:, col * LANES:(col + 1) * LANES], high).astype(BF16))
            qg = q[:, kh * gw:(kh + 1) * gw]
            qs = jnp.concatenate([qg * head_keep[g] for g in range(group)], axis=0).astype(BF16)
            s = lax.dot_general(qs, kk, (((1,), (1,)), ((), ())), preferred_element_type=F32)
            scores.append(jnp.where(valid[None], s.reshape(group, rows, KEY_SLOTS), -jnp.inf))
    s = jnp.concatenate(scores, axis=0)
    sink_all = jnp.concatenate([sink] * len(problems), axis=0)
    mx = jnp.maximum(jnp.max(s, axis=-1, keepdims=True), sink_all)
    p = jnp.exp(s - mx)
    den = jnp.sum(p, axis=-1, keepdims=True) + jnp.exp(sink_all - mx)
    p = p * (1.0 / den)

    outs = []
    for i in range(len(problems)):
        slabs = []
        for kh in range(n_kv):
            c = i * n_kv + kh
            pg = p[c * group:(c + 1) * group].reshape(group * rows, KEY_SLOTS).astype(BF16)
            o = jnp.dot(pg, values[c], preferred_element_type=F32).reshape(group, rows, gw)
            out = o[0]
            for g in range(1, group):
                out = jnp.where(head_of_lane == g, o[g], out)
            slabs.append(out)
        outs.append(jnp.concatenate(slabs, axis=1))
    return outs


def _sink_column(sink_ref):
    return sink_ref[...][:, :, 0:1]


def _attn_prompt_kernel(n_kv, group, head_dim, sink_ref, q_ref, kp_ref, kc_ref, vp_ref, vc_ref, o_ref):
    n = pl.program_id(1)
    kband = jnp.concatenate([kp_ref[...], kc_ref[...]], axis=0)
    vband = jnp.concatenate([vp_ref[...], vc_ref[...]], axis=0)
    kpos0 = (n - 1) * WINDOW
    (out,) = _band_attention([(q_ref[...].astype(F32), kband, vband, 0)], kpos0, _sink_column(sink_ref),
                             n_kv, group, head_dim)
    o_ref[...] = out.astype(BF16)


def _attn_prompt(q, k, v, sink_tab, layer, *, batch, seq, n_kv, group, head_dim):
    m, d_q = q.shape
    d_kv = k.shape[1]
    nb = seq // WINDOW
    cur = lambda b, n: (b * nb + n, 0)
    prev = lambda b, n: (b * nb + jnp.maximum(n - 1, 0), 0)
    return pl.pallas_call(
        functools.partial(_attn_prompt_kernel, n_kv, group, head_dim),
        grid=(batch, nb),
        in_specs=[
            pl.BlockSpec((None, n_kv * group, 1, LANES), lambda b, n: (layer, 0, 0, 0)),
            pl.BlockSpec((WINDOW, d_q), cur),
            pl.BlockSpec((WINDOW, d_kv), prev),
            pl.BlockSpec((WINDOW, d_kv), cur),
            pl.BlockSpec((WINDOW, d_kv), prev),
            pl.BlockSpec((WINDOW, d_kv), cur),
        ],
        out_specs=pl.BlockSpec((WINDOW, d_q), cur),
        out_shape=jax.ShapeDtypeStruct((m, d_q), BF16),
        compiler_params=_params(2),
        name="attn_prompt",
    )(sink_tab, q, k, k, v, v)


def _attn_sample_kernel(seq, n_kv, group, head_dim, sink_ref, q_ref, kc_ref, kn_ref, vc_ref, vn_ref,
                        _, o_ref):
    n_seq = kc_ref.shape[0]
    d_kv = kn_ref.shape[1]
    per_block = SUBLANES // seq
    pad = jnp.zeros((KEY_SLOTS - WINDOW - SUBLANES, d_kv), F32)
    qf = q_ref[...].astype(F32)
    problems = []
    for i in range(n_seq):
        blk, row0 = i // per_block, (i % per_block) * seq
        rows = slice(blk * SUBLANES, (blk + 1) * SUBLANES)
        k8, v8 = kn_ref[rows, :], vn_ref[rows, :]
        if row0:
            k8 = pltpu.roll(k8, SUBLANES - row0, 0)
            v8 = pltpu.roll(v8, SUBLANES - row0, 0)
        kband = jnp.concatenate([kc_ref[i], k8, pad], axis=0)
        vband = jnp.concatenate([vc_ref[i], v8, pad], axis=0)
        problems.append((qf[rows, :], kband, vband, row0))
    outs = _band_attention(problems, PAST_LEN - WINDOW, _sink_column(sink_ref), n_kv, group, head_dim)
    r_idx = lax.broadcasted_iota(jnp.int32, outs[0].shape, 0)
    blocks = []
    for blk in range(n_seq // per_block):
        out = outs[blk * per_block]
        for s in range(1, per_block):
            out = jnp.where(r_idx >= s * seq, outs[blk * per_block + s], out)
        blocks.append(out)
    o_ref[...] = jnp.concatenate(blocks, axis=0).astype(BF16)


def _attn_sample(attn, q, k, v, k_cache, v_cache, sink_tab, layer, *, row0, seq, seqs_per_step, n_kv,
                 group, head_dim):
    d_q = q.shape[1]
    d_kv = k.shape[1]
    nbatch = k_cache.shape[1]
    ns = seqs_per_step
    rows = ns * seq
    blk0 = row0 // rows
    new = lambda b: (blk0 + b, 0)
    cache = lambda b: (layer, b, 0, 0)
    return pl.pallas_call(
        functools.partial(_attn_sample_kernel, seq, n_kv, group, head_dim),
        grid=(nbatch // ns,),
        in_specs=[
            pl.BlockSpec((None, n_kv * group, 1, LANES), lambda b: (layer, 0, 0, 0)),
            pl.BlockSpec((rows, d_q), new),
            pl.BlockSpec((None, ns, WINDOW, d_kv), cache),
            pl.BlockSpec((rows, d_kv), new),
            pl.BlockSpec((None, ns, WINDOW, d_kv), cache),
            pl.BlockSpec((rows, d_kv), new),
            pl.BlockSpec(memory_space=pl.ANY),
        ],
        out_specs=pl.BlockSpec((rows, d_q), new),
        out_shape=jax.ShapeDtypeStruct(attn.shape, attn.dtype),
        input_output_aliases={6: 0},
        compiler_params=_params(1),
        name="attn_sample",
    )(sink_tab, q, k_cache, k, v_cache, v, attn)


def _conv_kernel(xb_ref, wb_ref, wc_ref, wh_ref, cw_ref, t_ref, f1_ref, f2_ref, y_ref, u_ref, ubuf):
    tm = xb_ref.shape[0]
    i = pl.program_id(1)
    xb = xb_ref[...]
    cb = jnp.dot(xb, wb_ref[...], preferred_element_type=F32)
    cc = jnp.dot(xb, wc_ref[...], preferred_element_type=F32)
    ch = jnp.dot(xb, wh_ref[...], preferred_element_type=F32)
    u = cc * ch

    @pl.when(i == 0)
    def _():
        ubuf[0:SUBLANES, :] = jnp.zeros((SUBLANES, ubuf.shape[1]), F32)

    @pl.when(i > 0)
    def _():
        ubuf[0:SUBLANES, :] = ubuf[tm:tm + SUBLANES, :]

    ubuf[SUBLANES:, :] = u
    t = t_ref[...]
    last = i == pl.num_programs(1) - 1
    u_m1 = jnp.where(t >= 1, ubuf[pl.ds(SUBLANES - 1, tm), :], jnp.where(last, f1_ref[...], 0.0))
    u_m2 = jnp.where(t >= 2, ubuf[pl.ds(SUBLANES - 2, tm), :], jnp.where(last, f2_ref[...], 0.0))
    cw = cw_ref[...]
    conv = cw[0:1, :] * u_m2 + cw[1:2, :] * u_m1 + cw[2:3, :] * u
    y_ref[...] = (cb * conv).astype(BF16)
    u_ref[...] = u


def _conv_proj(xb, w_in, conv_w, t_idx, fill1, fill2, layer, *, tm, tc, col0, d_conv):
    m, d = xb.shape
    nc = d_conv // tc
    off = col0 // tc
    w_spec = lambda k: pl.BlockSpec((None, d, tc), lambda c, i: (layer, 0, off + k * nc + c))
    return pl.pallas_call(
        _conv_kernel,
        grid=(nc, m // tm),
        in_specs=[
            pl.BlockSpec((tm, d), lambda c, i: (i, 0)),
            w_spec(0), w_spec(1), w_spec(2),
            pl.BlockSpec((None, CONV_W, tc), lambda c, i: (layer, 0, c)),
            pl.BlockSpec((tm, 1), lambda c, i: (i, 0)),
            pl.BlockSpec((tm, tc), lambda c, i: (0, c)),
            pl.BlockSpec((tm, tc), lambda c, i: (0, c)),
        ],
        out_specs=[
            pl.BlockSpec((tm, tc), lambda c, i: (i, c)),
            pl.BlockSpec((tm, tc), lambda c, i: (i, c)),
        ],
        out_shape=[
            jax.ShapeDtypeStruct((m, d_conv), BF16),
            jax.ShapeDtypeStruct((m, d_conv), F32),
        ],
        scratch_shapes=[pltpu.VMEM((tm + SUBLANES, tc), F32)],
        compiler_params=_params(2),
        name="conv_proj",
    )(xb, w_in, w_in, w_in, conv_w, t_idx, fill1, fill2)


def _mix_ln_kernel(alpha, x_ref, a_ref, c_ref, wga_ref, wgc_ref, wa_ref, wc_ref, wo_ref, g_ref, b_ref,
                   o_ref, xb_ref, acc_ref):
    j = pl.program_id(1)

    @pl.when(j == 0)
    def _():
        xb_ref[...] = x_ref[...].astype(BF16)
        acc_ref[...] = jnp.zeros_like(acc_ref)

    xb = xb_ref[...]
    ga = jnp.dot(xb, wga_ref[...], preferred_element_type=F32)
    gc = jnp.dot(xb, wgc_ref[...], preferred_element_type=F32)
    pa = jnp.dot(a_ref[...], wa_ref[...], preferred_element_type=F32)
    pc = jnp.dot(c_ref[...], wc_ref[...], preferred_element_type=F32)
    merged = (jax.nn.sigmoid(ga) * pa + jax.nn.sigmoid(gc) * pc).astype(BF16)
    acc_ref[...] += jnp.dot(merged, wo_ref[...], preferred_element_type=F32)

    @pl.when(j == pl.num_programs(1) - 1)
    def _():
        y = alpha * x_ref[...] + acc_ref[...]
        o_ref[...] = _layer_norm(y, g_ref[...], b_ref[...])


def _mix_ln(x, attn, yconv, w_in, w_a, w_c, w_o, ln_g, ln_b, layer, ln_idx, alpha, *, tm, tc, ga_col0,
            gc_col0):
    m, d = x.shape
    d_q = attn.shape[1]
    d_conv = yconv.shape[1]
    return pl.pallas_call(
        functools.partial(_mix_ln_kernel, alpha),
        grid=(m // tm, d // tc),
        in_specs=[
            pl.BlockSpec((tm, d), lambda i, j: (i, 0)),
            pl.BlockSpec((tm, d_q), lambda i, j: (i, 0)),
            pl.BlockSpec((tm, d_conv), lambda i, j: (i, 0)),
            pl.BlockSpec((None, d, tc), lambda i, j: (layer, 0, ga_col0 // tc + j)),
            pl.BlockSpec((None, d, tc), lambda i, j: (layer, 0, gc_col0 // tc + j)),
            pl.BlockSpec((None, d_q, tc), lambda i, j: (layer, 0, j)),
            pl.BlockSpec((None, d_conv, tc), lambda i, j: (layer, 0, j)),
            pl.BlockSpec((None, tc, d), lambda i, j: (layer, j, 0)),
            pl.BlockSpec((None, 1, d), lambda i, j: (ln_idx, 0, 0)),
            pl.BlockSpec((None, 1, d), lambda i, j: (ln_idx, 0, 0)),
        ],
        out_specs=pl.BlockSpec((tm, d), lambda i, j: (i, 0)),
        out_shape=jax.ShapeDtypeStruct((m, d), F32),
        scratch_shapes=[pltpu.VMEM((tm, d), BF16), pltpu.VMEM((tm, d), F32)],
        compiler_params=_params(2),
        name="mix_ln",
    )(x, attn, yconv, w_in, w_in, w_a, w_c, w_o, ln_g, ln_b)


def _rope_tables(pos, head_dim):
    inv_freq = ROPE_THETA ** (-jnp.arange(0, head_dim, 2, dtype=F32) / head_dim)
    ang = pos.astype(F32)[:, None] * inv_freq[None, :]
    cos = jnp.cos(ang)
    sin = jnp.sin(ang)
    reps = LANES // head_dim
    cos_tab = jnp.tile(jnp.concatenate([cos, cos], axis=1), (1, reps))
    sin_tab = jnp.tile(jnp.concatenate([-sin, sin], axis=1), (1, reps))
    return cos_tab, sin_tab


def _largest_divisor(n, cap, multiple_of=1):
    return max(k for k in range(multiple_of, cap + 1, multiple_of) if n % k == 0)


def kernel(x_prompt, x_sample, cache_k_win, cache_v_win, state_conv, ln_g, ln_b, w_in, sinks, conv_w,
           w_branch_attn, w_branch_conv, w_out, ffn1_gu, ffn1_down, ffn2_gu, ffn2_down):
    depth = w_in.shape[0]
    bp, tp, d = x_prompt.shape
    bs, ts, _ = x_sample.shape
    n_kv, head_dim = cache_k_win.shape[-2:]
    d_q = w_branch_attn.shape[1]
    d_conv = conv_w.shape[-1]
    d_kv = n_kv * head_dim
    group = d_q // d_kv
    mp, ms = bp * tp, bs * ts
    m = mp + ms
    alpha = (2.0 * depth) ** 0.25
    tm = _largest_divisor(m, MAX_ROW_TILE, BF16_ROWS)
    ns = _largest_divisor(bs, SAMPLE_SEQS_PER_STEP, SUBLANES // ts)
    assert tm >= ms and SUBLANES % ts == 0 and mp % (ns * ts) == 0 and tp % WINDOW == 0
    tf = 512
    tc = 512
    conv_col0 = d_q + 2 * d_kv
    ga_col0 = conv_col0 + 3 * d_conv
    gc_col0 = ga_col0 + d

    ln_g = ln_g.reshape(depth * 3, 1, d)
    ln_b = ln_b.reshape(depth * 3, 1, d)
    sink_tab = jnp.broadcast_to(sinks[:, :, None, None], sinks.shape + (1, LANES))
    w_in_b = w_in.astype(BF16)
    w_a_b = w_branch_attn.astype(BF16)
    w_c_b = w_branch_conv.astype(BF16)
    w_o_b = w_out.astype(BF16)
    ffn1_gu_b, ffn1_down_b = ffn1_gu.astype(BF16), ffn1_down.astype(BF16)
    ffn2_gu_b, ffn2_down_b = ffn2_gu.astype(BF16), ffn2_down.astype(BF16)

    t_prompt = jnp.tile(jnp.arange(tp, dtype=jnp.int32), bp)
    t_sample = jnp.tile(jnp.arange(ts, dtype=jnp.int32), bs)
    t_idx = jnp.concatenate([t_prompt, t_sample])
    cos_tab, sin_tab = _rope_tables(jnp.concatenate([t_prompt, PAST_LEN + t_sample]), head_dim)
    t_idx = t_idx.reshape(m, 1)
    k_cache = cache_k_win.reshape(depth, bs, WINDOW, d_kv)
    v_cache = cache_v_win.reshape(depth, bs, WINDOW, d_kv)

    x = jnp.concatenate([x_prompt.reshape(mp, d), x_sample.reshape(ms, d)], axis=0)
    ks_p, vs_p, cs_p, ks_s, vs_s, cs_s = [], [], [], [], [], []
    for l in range(depth):
        x1, x1b = _ffn_ln(x, ffn1_gu_b, ffn1_down_b, ln_g, ln_b, l, 3 * l, alpha, tm=tm, tf=tf,
                          emit_bf16=True)
        q, k, v = _qkv_proj(x1b, w_in_b, cos_tab, sin_tab, l, tm=tm, d_q=d_q, d_kv=d_kv, head_dim=head_dim)
        attn = _attn_prompt(q, k, v, sink_tab, l, batch=bp, seq=tp, n_kv=n_kv, group=group,
                            head_dim=head_dim)
        attn = _attn_sample(attn, q, k, v, k_cache, v_cache, sink_tab, l, row0=mp, seq=ts,
                            seqs_per_step=ns, n_kv=n_kv, group=group, head_dim=head_dim)
        st = state_conv[l]
        zeros = jnp.zeros((bs, ts, d_conv), F32)
        fill1 = zeros.at[:, 0].set(st[:, 1]).reshape(ms, d_conv)
        fill2 = zeros.at[:, 0].set(st[:, 0]).at[:, 1].set(st[:, 1]).reshape(ms, d_conv)
        fill1 = jnp.pad(fill1, ((tm - ms, 0), (0, 0)))
        fill2 = jnp.pad(fill2, ((tm - ms, 0), (0, 0)))
        yconv, u = _conv_proj(x1b, w_in_b, conv_w, t_idx, fill1, fill2, l, tm=tm, tc=tc, col0=conv_col0,
                              d_conv=d_conv)
        x2 = _mix_ln(x1, attn, yconv, w_in_b, w_a_b, w_c_b, w_o_b, ln_g, ln_b, l, 3 * l + 1, alpha,
                     tm=tm, tc=tc, ga_col0=ga_col0, gc_col0=gc_col0)
        (x,) = _ffn_ln(x2, ffn2_gu_b, ffn2_down_b, ln_g, ln_b, l, 3 * l + 2, alpha, tm=tm, tf=tf,
                       emit_bf16=False)

        ks_p.append(k[:mp].reshape(bp, tp, n_kv, head_dim)[:, -WINDOW:])
        vs_p.append(v[:mp].reshape(bp, tp, n_kv, head_dim)[:, -WINDOW:])
        cs_p.append(u[:mp].reshape(bp, tp, d_conv)[:, -(CONV_W - 1):])
        k_new = k[mp:].reshape(bs, ts, n_kv, head_dim)
        v_new = v[mp:].reshape(bs, ts, n_kv, head_dim)
        ks_s.append(jnp.concatenate([cache_k_win[l], k_new], axis=1)[:, -WINDOW:])
        vs_s.append(jnp.concatenate([cache_v_win[l], v_new], axis=1)[:, -WINDOW:])
        cs_s.append(jnp.concatenate([st, u[mp:].reshape(bs, ts, d_conv)], axis=1)[:, -(CONV_W - 1):])

    y_p = x[:mp].reshape(bp, tp, d)
    y_s = x[mp:].reshape(bs, ts, d)
    return (y_p, y_s, jnp.stack(ks_p), jnp.stack(vs_p), jnp.stack(cs_p),
            jnp.stack(ks_s), jnp.stack(vs_s), jnp.stack(cs_s))
```

```python
import functools

import jax
import jax.numpy as jnp
from jax import lax
from jax.experimental import pallas as pl
from jax.experimental.pallas import tpu as pltpu

F32 = jnp.float32
BF16 = jnp.bfloat16

PAST_LEN = 16384
WINDOW = 128
ROPE_THETA = 10000.0
LN_EPS = 1e-5
CONV_W = 3

LANES = 128
SUBLANES = 8
BF16_ROWS = 16
VMEM_LIMIT_BYTES = 62 * 1024 * 1024

KEY_SLOTS = 2 * WINDOW
MAX_ROW_TILE = 640
SAMPLE_SEQS_PER_STEP = 8


def _params(n_axes):
    return pltpu.CompilerParams(
        dimension_semantics=("arbitrary",) * n_axes,
        vmem_limit_bytes=VMEM_LIMIT_BYTES,
    )


def _layer_norm(y, g, b):
    mu = jnp.mean(y, axis=-1, keepdims=True)
    d = y - mu
    var = jnp.mean(d * d, axis=-1, keepdims=True)
    return d * lax.rsqrt(var + LN_EPS) * g + b


def _round_up(n, k):
    return -(-n // k) * k


def _ffn_ln_kernel(alpha, has_tail, emit_bf16, split, n_jobs, *refs):
    refs = list(refs)
    x_ref = refs.pop(0)
    xt_ref = refs.pop(0) if has_tail else None
    wg_ref, wu_ref, wd_ref, g_ref, b_ref = refs[:5]
    del refs[:5]
    job_src = refs[:n_jobs]
    del refs[:n_jobs]
    o_ref = refs.pop(0)
    ob_ref = refs.pop(0) if emit_bf16 else None
    ot_ref = refs.pop(0) if split else None
    job_dst = refs[:n_jobs]
    del refs[:n_jobs]
    xb_ref, acc_ref = refs
    i = pl.program_id(0)
    j = pl.program_id(1)
    last_tile = i == pl.num_programs(0) - 1
    tm = x_ref.shape[0]

    def load_x():
        x = x_ref[...]
        if xt_ref is None:
            return x
        merged = jnp.concatenate([x[:tm - xt_ref.shape[0]], xt_ref[...]], axis=0)
        return jnp.where(last_tile, merged, x)

    for src, dst in zip(job_src, job_dst):
        dst[...] = src[...].astype(BF16)

    @pl.when(j == 0)
    def _():
        xb_ref[...] = load_x().astype(BF16)
        acc_ref[...] = jnp.zeros_like(acc_ref)

    xb = xb_ref[...]
    g = jnp.dot(xb, wg_ref[...], preferred_element_type=F32)
    u = jnp.dot(xb, wu_ref[...], preferred_element_type=F32)
    h = (jax.nn.silu(g) * u).astype(BF16)
    acc_ref[...] += jnp.dot(h, wd_ref[...], preferred_element_type=F32)

    @pl.when(j == pl.num_programs(1) - 1)
    def _():
        y = alpha * load_x() + 0.5 * acc_ref[...]
        out = _layer_norm(y, g_ref[...], b_ref[...])
        o_ref[...] = out
        if emit_bf16:
            ob_ref[...] = out.astype(BF16)
        if split:
            @pl.when(last_tile)
            def _():
                ot_ref[...] = out[tm - ot_ref.shape[0]:, :]


def _ffn_ln(x, w_gu, w_down, ln_g, ln_b, ln_idx, alpha, *, tm, tf, emit_bf16=False, x_tail=None,
            split_rows=None, cast_jobs=()):
    d = x.shape[1]
    m = x.shape[0] + (0 if x_tail is None else x_tail.shape[0])
    f = w_down.shape[0]
    nj = f // tf
    n_tiles = m // tm
    row = lambda i, j: (i, 0)
    fixed = lambda i, j: (0, 0)
    in_specs = [pl.BlockSpec((tm, d), row)]
    args = [x]
    if x_tail is not None:
        in_specs.append(pl.BlockSpec(x_tail.shape, fixed))
        args.append(x_tail)
    in_specs += [
        pl.BlockSpec((d, tf), lambda i, j: (0, j)),
        pl.BlockSpec((d, tf), lambda i, j: (0, nj + j)),
        pl.BlockSpec((tf, d), lambda i, j: (j, 0)),
        pl.BlockSpec((None, 1, d), lambda i, j: (ln_idx, 0, 0)),
        pl.BlockSpec((None, 1, d), lambda i, j: (ln_idx, 0, 0)),
    ]
    args += [w_gu, w_gu, w_down, ln_g, ln_b]
    m_head = m if split_rows is None else split_rows
    out_shape = [jax.ShapeDtypeStruct((m_head, d), F32)]
    out_specs = [pl.BlockSpec((tm, d), row)]
    if emit_bf16:
        out_shape.append(jax.ShapeDtypeStruct((m, d), BF16))
        out_specs.append(pl.BlockSpec((tm, d), row))
    if split_rows is not None:
        out_shape.append(jax.ShapeDtypeStruct((m - split_rows, d), F32))
        out_specs.append(pl.BlockSpec((m - split_rows, d), fixed))
    n_steps = n_tiles * nj
    for w, layer in cast_jobs:
        _, r, c = w.shape
        rb = _round_up(-(-r // n_steps), BF16_ROWS)
        nb = -(-r // rb)
        block = lambda i, j, nb=nb: (jnp.minimum(i * nj + j, nb - 1), 0)
        in_specs.append(pl.BlockSpec((None, rb, c), lambda i, j, nb=nb, layer=layer:
                                     (layer, jnp.minimum(i * nj + j, nb - 1), 0)))
        args.append(w)
        out_shape.append(jax.ShapeDtypeStruct((r, c), BF16))
        out_specs.append(pl.BlockSpec((rb, c), block))
    return pl.pallas_call(
        functools.partial(_ffn_ln_kernel, alpha, x_tail is not None, emit_bf16, split_rows is not None,
                          len(cast_jobs)),
        grid=(n_tiles, nj),
        in_specs=in_specs,
        out_specs=out_specs,
        out_shape=out_shape,
        scratch_shapes=[pltpu.VMEM((tm, d), BF16), pltpu.VMEM((tm, d), F32)],
        compiler_params=_params(2),
        name="ffn_ln",
    )(*args)


def _rope_cols(x, cos, sin_signed, first_half, half):
    fwd = pltpu.roll(x, x.shape[1] - half, 1)
    bwd = pltpu.roll(x, half, 1)
    return x * cos + jnp.where(first_half, fwd, bwd) * sin_signed


def _qkv_kernel(d_q, d_kv, head_dim, xb_ref, w_ref, cos_ref, sin_ref, q_ref, k_ref, v_ref):
    qkv = jnp.dot(xb_ref[...], w_ref[...], preferred_element_type=F32)
    cos = cos_ref[...]
    sin = sin_ref[...]
    half = head_dim // 2
    lane = lax.broadcasted_iota(jnp.int32, cos.shape, 1)
    first_half = (lane % head_dim) < half
    q_scale = head_dim ** -0.5
    for c in range(d_q // LANES):
        sl = slice(c * LANES, (c + 1) * LANES)
        q_ref[:, sl] = (_rope_cols(qkv[:, sl], cos, sin, first_half, half) * q_scale).astype(BF16)
    for c in range(d_kv // LANES):
        src = slice(d_q + c * LANES, d_q + (c + 1) * LANES)
        k_ref[:, c * LANES:(c + 1) * LANES] = _rope_cols(qkv[:, src], cos, sin, first_half, half)
    v_ref[...] = qkv[:, d_q + d_kv:]


def _qkv_proj(xb, w_in, cos_tab, sin_tab, *, tm, d_q, d_kv, head_dim):
    m, d = xb.shape
    n_cols = d_q + 2 * d_kv
    return pl.pallas_call(
        functools.partial(_qkv_kernel, d_q, d_kv, head_dim),
        grid=(m // tm,),
        in_specs=[
            pl.BlockSpec((tm, d), lambda i: (i, 0)),
            pl.BlockSpec((d, n_cols), lambda i: (0, 0)),
            pl.BlockSpec((tm, LANES), lambda i: (i, 0)),
            pl.BlockSpec((tm, LANES), lambda i: (i, 0)),
        ],
        out_specs=[
            pl.BlockSpec((tm, d_q), lambda i: (i, 0)),
            pl.BlockSpec((tm, d_kv), lambda i: (i, 0)),
            pl.BlockSpec((tm, d_kv), lambda i: (i, 0)),
        ],
        out_shape=[
            jax.ShapeDtypeStruct((m, d_q), BF16),
            jax.ShapeDtypeStruct((m, d_kv), F32),
            jax.ShapeDtypeStruct((m, d_kv), F32),
        ],
        compiler_params=_params(1),
        name="qkv_proj",
    )(xb, w_in, cos_tab, sin_tab)


def _lane_tile4(x128, want_high):
    lane = lax.broadcasted_iota(jnp.int32, x128.shape, 1)
    swapped = pltpu.roll(x128, LANES // 2, 1)
    low = lane < LANES // 2
    both = jnp.where(low, swapped, x128) if want_high else jnp.where(low, x128, swapped)
    return jnp.concatenate([both, both], axis=1)


def _band_attention(problems, kpos0, sink, n_kv, group, head_dim):
    rows = problems[0][0].shape[0]
    gw = group * head_dim
    r_idx = lax.broadcasted_iota(jnp.int32, (rows, KEY_SLOTS), 0)
    s_idx = lax.broadcasted_iota(jnp.int32, (rows, KEY_SLOTS), 1)
    in_past = s_idx + kpos0 >= 0
    valid_of_row0 = {}
    head_of_lane = lax.broadcasted_iota(jnp.int32, (rows, gw), 1) // head_dim
    head_keep = [(head_of_lane == g).astype(F32) for g in range(group)]

    scores, values = [], []
    for q, kband, vband, row0 in problems:
        if row0 not in valid_of_row0:
            diff = r_idx - row0 + WINDOW - s_idx
            valid_of_row0[row0] = (diff >= 0) & (diff <= WINDOW) & in_past
        valid = valid_of_row0[row0]
        for kh in range(n_kv):
            col = (kh * head_dim) // LANES
            high = ((kh * head_dim) % LANES) != 0
            kk = _lane_tile4(kband[:, col * LANES:(col + 1) * LANES], high).astype(BF16)
            values.append(_lane_tile4(vband[:, col * LANES:(col + 1) * LANES], high).astype(BF16))
            qg = q[:, kh * gw:(kh + 1) * gw]
            qs = jnp.concatenate([qg * head_keep[g] for g in range(group)], axis=0).astype(BF16)
            s = lax.dot_general(qs, kk, (((1,), (1,)), ((), ())), preferred_element_type=F32)
            scores.append(jnp.where(valid[None], s.reshape(group, rows, KEY_SLOTS), -jnp.inf))
    s = jnp.concatenate(scores, axis=0)
    sink_all = jnp.concatenate([sink] * len(problems), axis=0)
    mx = jnp.maximum(jnp.max(s, axis=-1, keepdims=True), sink_all)
    p = jnp.exp(s - mx)
    den = jnp.sum(p, axis=-1, keepdims=True) + jnp.exp(sink_all - mx)
    p = p * (1.0 / den)

    outs = []
    for i in range(len(problems)):
        slabs = []
        for kh in range(n_kv):
            c = i * n_kv + kh
            pg = p[c * group:(c + 1) * group].reshape(group * rows, KEY_SLOTS).astype(BF16)
            o = jnp.dot(pg, values[c], preferred_element_type=F32).reshape(group, rows, gw)
            out = o[0]
            for g in range(1, group):
                out = jnp.where(head_of_lane == g, o[g], out)
            slabs.append(out)
        outs.append(jnp.concatenate(slabs, axis=1))
    return outs


def _sink_column(sink_ref):
    return sink_ref[...][:, :, 0:1]


def _attn_prompt_kernel(n_kv, group, head_dim, sink_ref, q_ref, kp_ref, kc_ref, vp_ref, vc_ref, o_ref):
    n = pl.program_id(1)
    kband = jnp.concatenate([kp_ref[...], kc_ref[...]], axis=0)
    vband = jnp.concatenate([vp_ref[...], vc_ref[...]], axis=0)
    kpos0 = (n - 1) * WINDOW
    (out,) = _band_attention([(q_ref[...].astype(F32), kband, vband, 0)], kpos0, _sink_column(sink_ref),
                             n_kv, group, head_dim)
    o_ref[...] = out.astype(BF16)


def _attn_prompt(q, k, v, sink_tab, layer, *, batch, seq, n_kv, group, head_dim):
    m, d_q = q.shape
    d_kv = k.shape[1]
    nb = seq // WINDOW
    cur = lambda b, n: (b * nb + n, 0)
    prev = lambda b, n: (b * nb + jnp.maximum(n - 1, 0), 0)
    return pl.pallas_call(
        functools.partial(_attn_prompt_kernel, n_kv, group, head_dim),
        grid=(batch, nb),
        in_specs=[
            pl.BlockSpec((None, n_kv * group, 1, LANES), lambda b, n: (layer, 0, 0, 0)),
            pl.BlockSpec((WINDOW, d_q), cur),
            pl.BlockSpec((WINDOW, d_kv), prev),
            pl.BlockSpec((WINDOW, d_kv), cur),
            pl.BlockSpec((WINDOW, d_kv), prev),
            pl.BlockSpec((WINDOW, d_kv), cur),
        ],
        out_specs=pl.BlockSpec((WINDOW, d_q), cur),
        out_shape=jax.ShapeDtypeStruct((m, d_q), BF16),
        compiler_params=_params(2),
        name="attn_prompt",
    )(sink_tab, q, k, k, v, v)


def _attn_sample_kernel(seq, n_kv, group, head_dim, sink_ref, q_ref, kc_ref, kn_ref, vc_ref, vn_ref,
                        _, o_ref):
    n_seq = kc_ref.shape[0]
    d_kv = kn_ref.shape[1]
    per_block = SUBLANES // seq
    pad = jnp.zeros((KEY_SLOTS - WINDOW - SUBLANES, d_kv), F32)
    qf = q_ref[...].astype(F32)
    problems = []
    for i in range(n_seq):
        blk, row0 = i // per_block, (i % per_block) * seq
        rows = slice(blk * SUBLANES, (blk + 1) * SUBLANES)
        k8, v8 = kn_ref[rows, :], vn_ref[rows, :]
        if row0:
            k8 = pltpu.roll(k8, SUBLANES - row0, 0)
            v8 = pltpu.roll(v8, SUBLANES - row0, 0)
        kband = jnp.concatenate([kc_ref[i], k8, pad], axis=0)
        vband = jnp.concatenate([vc_ref[i], v8, pad], axis=0)
        problems.append((qf[rows, :], kband, vband, row0))
    outs = _band_attention(problems, PAST_LEN - WINDOW, _sink_column(sink_ref), n_kv, group, head_dim)
    r_idx = lax.broadcasted_iota(jnp.int32, outs[0].shape, 0)
    blocks = []
    for blk in range(n_seq // per_block):
        out = outs[blk * per_block]
        for s in range(1, per_block):
            out = jnp.where(r_idx >= s * seq, outs[blk * per_block + s], out)
        blocks.append(out)
    o_ref[...] = jnp.concatenate(blocks, axis=0).astype(BF16)


def _attn_sample(attn, q, k, v, k_cache, v_cache, sink_tab, layer, *, row0, seq, seqs_per_step, n_kv,
                 group, head_dim):
    d_q = q.shape[1]
    d_kv = k.shape[1]
    nbatch = k_cache.shape[1]
    ns = seqs_per_step
    rows = ns * seq
    blk0 = row0 // rows
    new = lambda b: (blk0 + b, 0)
    cache = lambda b: (layer, b, 0, 0)
    return pl.pallas_call(
        functools.partial(_attn_sample_kernel, seq, n_kv, group, head_dim),
        grid=(nbatch // ns,),
        in_specs=[
            pl.BlockSpec((None, n_kv * group, 1, LANES), lambda b: (layer, 0, 0, 0)),
            pl.BlockSpec((rows, d_q), new),
            pl.BlockSpec((None, ns, WINDOW, d_kv), cache),
            pl.BlockSpec((rows, d_kv), new),
            pl.BlockSpec((None, ns, WINDOW, d_kv), cache),
            pl.BlockSpec((rows, d_kv), new),
            pl.BlockSpec(memory_space=pl.ANY),
        ],
        out_specs=pl.BlockSpec((rows, d_q), new),
        out_shape=jax.ShapeDtypeStruct(attn.shape, attn.dtype),
        input_output_aliases={6: 0},
        compiler_params=_params(1),
        name="attn_sample",
    )(sink_tab, q, k_cache, k, v_cache, v, attn)


def _conv_kernel(xb_ref, wb_ref, wc_ref, wh_ref, cw_ref, t_ref, f1_ref, f2_ref, y_ref, u_ref, ubuf):
    tm = xb_ref.shape[0]
    i = pl.program_id(1)
    xb = xb_ref[...]
    cb = jnp.dot(xb, wb_ref[...], preferred_element_type=F32)
    cc = jnp.dot(xb, wc_ref[...], preferred_element_type=F32)
    ch = jnp.dot(xb, wh_ref[...], preferred_element_type=F32)
    u = cc * ch

    @pl.when(i == 0)
    def _():
        ubuf[0:SUBLANES, :] = jnp.zeros((SUBLANES, ubuf.shape[1]), F32)

    @pl.when(i > 0)
    def _():
        ubuf[0:SUBLANES, :] = ubuf[tm:tm + SUBLANES, :]

    ubuf[SUBLANES:, :] = u
    t = t_ref[...]
    last = i == pl.num_programs(1) - 1
    u_m1 = jnp.where(t >= 1, ubuf[pl.ds(SUBLANES - 1, tm), :], jnp.where(last, f1_ref[...], 0.0))
    u_m2 = jnp.where(t >= 2, ubuf[pl.ds(SUBLANES - 2, tm), :], jnp.where(last, f2_ref[...], 0.0))
    cw = cw_ref[...]
    conv = cw[0:1, :] * u_m2 + cw[1:2, :] * u_m1 + cw[2:3, :] * u
    y_ref[...] = (cb * conv).astype(BF16)
    u_ref[...] = u


def _conv_proj(xb, w_in, conv_w, t_idx, fill1, fill2, layer, *, tm, tc, col0, d_conv):
    m, d = xb.shape
    nc = d_conv // tc
    off = col0 // tc
    w_spec = lambda k: pl.BlockSpec((d, tc), lambda c, i: (0, off + k * nc + c))
    return pl.pallas_call(
        _conv_kernel,
        grid=(nc, m // tm),
        in_specs=[
            pl.BlockSpec((tm, d), lambda c, i: (i, 0)),
            w_spec(0), w_spec(1), w_spec(2),
            pl.BlockSpec((None, CONV_W, tc), lambda c, i: (layer, 0, c)),
            pl.BlockSpec((tm, 1), lambda c, i: (i, 0)),
            pl.BlockSpec((tm, tc), lambda c, i: (0, c)),
            pl.BlockSpec((tm, tc), lambda c, i: (0, c)),
        ],
        out_specs=[
            pl.BlockSpec((tm, tc), lambda c, i: (i, c)),
            pl.BlockSpec((tm, tc), lambda c, i: (i, c)),
        ],
        out_shape=[
            jax.ShapeDtypeStruct((m, d_conv), BF16),
            jax.ShapeDtypeStruct((m, d_conv), F32),
        ],
        scratch_shapes=[pltpu.VMEM((tm + SUBLANES, tc), F32)],
        compiler_params=_params(2),
        name="conv_proj",
    )(xb, w_in, w_in, w_in, conv_w, t_idx, fill1, fill2)


def _mix_ln_kernel(alpha, x_ref, a_ref, c_ref, wga_ref, wgc_ref, wa_ref, wc_ref, wo_ref, g_ref, b_ref,
                   o_ref, xb_ref, acc_ref):
    j = pl.program_id(1)

    @pl.when(j == 0)
    def _():
        xb_ref[...] = x_ref[...].astype(BF16)
        acc_ref[...] = jnp.zeros_like(acc_ref)

    xb = xb_ref[...]
    ga = jnp.dot(xb, wga_ref[...], preferred_element_type=F32)
    gc = jnp.dot(xb, wgc_ref[...], preferred_element_type=F32)
    pa = jnp.dot(a_ref[...], wa_ref[...], preferred_element_type=F32)
    pc = jnp.dot(c_ref[...], wc_ref[...], preferred_element_type=F32)
    merged = (jax.nn.sigmoid(ga) * pa + jax.nn.sigmoid(gc) * pc).astype(BF16)
    acc_ref[...] += jnp.dot(merged, wo_ref[...], preferred_element_type=F32)

    @pl.when(j == pl.num_programs(1) - 1)
    def _():
        y = alpha * x_ref[...] + acc_ref[...]
        o_ref[...] = _layer_norm(y, g_ref[...], b_ref[...])


def _mix_ln(x, attn, yconv, w_in, w_a, w_c, w_o, ln_g, ln_b, ln_idx, alpha, *, tm, tc, ga_col0, gc_col0):
    m, d = x.shape
    d_q = attn.shape[1]
    d_conv = yconv.shape[1]
    return pl.pallas_call(
        functools.partial(_mix_ln_kernel, alpha),
        grid=(m // tm, d // tc),
        in_specs=[
            pl.BlockSpec((tm, d), lambda i, j: (i, 0)),
            pl.BlockSpec((tm, d_q), lambda i, j: (i, 0)),
            pl.BlockSpec((tm, d_conv), lambda i, j: (i, 0)),
            pl.BlockSpec((d, tc), lambda i, j: (0, ga_col0 // tc + j)),
            pl.BlockSpec((d, tc), lambda i, j: (0, gc_col0 // tc + j)),
            pl.BlockSpec((d_q, tc), lambda i, j: (0, j)),
            pl.BlockSpec((d_conv, tc), lambda i, j: (0, j)),
            pl.BlockSpec((tc, d), lambda i, j: (j, 0)),
            pl.BlockSpec((None, 1, d), lambda i, j: (ln_idx, 0, 0)),
            pl.BlockSpec((None, 1, d), lambda i, j: (ln_idx, 0, 0)),
        ],
        out_specs=pl.BlockSpec((tm, d), lambda i, j: (i, 0)),
        out_shape=jax.ShapeDtypeStruct((m, d), F32),
        scratch_shapes=[pltpu.VMEM((tm, d), BF16), pltpu.VMEM((tm, d), F32)],
        compiler_params=_params(2),
        name="mix_ln",
    )(x, attn, yconv, w_in, w_in, w_a, w_c, w_o, ln_g, ln_b)


def _rope_tables(pos, head_dim):
    inv_freq = ROPE_THETA ** (-jnp.arange(0, head_dim, 2, dtype=F32) / head_dim)
    ang = pos.astype(F32)[:, None] * inv_freq[None, :]
    cos = jnp.cos(ang)
    sin = jnp.sin(ang)
    reps = LANES // head_dim
    cos_tab = jnp.tile(jnp.concatenate([cos, cos], axis=1), (1, reps))
    sin_tab = jnp.tile(jnp.concatenate([-sin, sin], axis=1), (1, reps))
    return cos_tab, sin_tab


def _largest_divisor(n, cap, multiple_of=1):
    return max(k for k in range(multiple_of, cap + 1, multiple_of) if n % k == 0)


def _seq_tails(a, n_seq, seq, rows):
    return jnp.stack([a[(b + 1) * seq - rows:(b + 1) * seq] for b in range(n_seq)])


def kernel(x_prompt, x_sample, cache_k_win, cache_v_win, state_conv, ln_g, ln_b, w_in, sinks, conv_w,
           w_branch_attn, w_branch_conv, w_out, ffn1_gu, ffn1_down, ffn2_gu, ffn2_down):
    depth = w_in.shape[0]
    bp, tp, d = x_prompt.shape
    bs, ts, _ = x_sample.shape
    n_kv, head_dim = cache_k_win.shape[-2:]
    d_q = w_branch_attn.shape[1]
    d_conv = conv_w.shape[-1]
    d_kv = n_kv * head_dim
    group = d_q // d_kv
    mp, ms = bp * tp, bs * ts
    m = mp + ms
    alpha = (2.0 * depth) ** 0.25
    tm = _largest_divisor(m, MAX_ROW_TILE, BF16_ROWS)
    ns = _largest_divisor(bs, SAMPLE_SEQS_PER_STEP, SUBLANES // ts)
    assert tm >= ms and SUBLANES % ts == 0 and mp % (ns * ts) == 0 and tp % WINDOW == 0
    assert ts >= CONV_W - 1 and tp >= WINDOW
    tf = 512
    tc = 512
    conv_col0 = d_q + 2 * d_kv
    ga_col0 = conv_col0 + 3 * d_conv
    gc_col0 = ga_col0 + d

    ln_g = ln_g.reshape(depth * 3, 1, d)
    ln_b = ln_b.reshape(depth * 3, 1, d)
    sink_tab = jnp.broadcast_to(sinks[:, :, None, None], sinks.shape + (1, LANES))

    t_prompt = jnp.tile(jnp.arange(tp, dtype=jnp.int32), bp)
    t_sample = jnp.tile(jnp.arange(ts, dtype=jnp.int32), bs)
    cos_tab, sin_tab = _rope_tables(jnp.concatenate([t_prompt, PAST_LEN + t_sample]), head_dim)
    t_idx = jnp.concatenate([t_prompt, t_sample]).reshape(m, 1)
    k_cache = cache_k_win.reshape(depth, bs, WINDOW, d_kv)
    v_cache = cache_v_win.reshape(depth, bs, WINDOW, d_kv)

    f1_gu, f1_down = ffn1_gu[0].astype(BF16), ffn1_down[0].astype(BF16)
    x = x_prompt.reshape(mp, d)
    x_tail = x_sample.reshape(ms, d)
    ks_p, vs_p, cs_p, ks_s, vs_s, cs_s = [], [], [], [], [], []
    for l in range(depth):
        jobs = [(w, l) for w in (w_in, w_branch_attn, w_branch_conv, w_out, ffn2_gu, ffn2_down)]
        x1, x1b, w_in_b, w_a_b, w_c_b, w_o_b, f2_gu, f2_down = _ffn_ln(
            x, f1_gu, f1_down, ln_g, ln_b, 3 * l, alpha, tm=tm, tf=tf, emit_bf16=True, x_tail=x_tail,
            cast_jobs=jobs)
        q, k, v = _qkv_proj(x1b, w_in_b, cos_tab, sin_tab, tm=tm, d_q=d_q, d_kv=d_kv, head_dim=head_dim)
        attn = _attn_prompt(q, k, v, sink_tab, l, batch=bp, seq=tp, n_kv=n_kv, group=group,
                            head_dim=head_dim)
        attn = _attn_sample(attn, q, k, v, k_cache, v_cache, sink_tab, l, row0=mp, seq=ts,
                            seqs_per_step=ns, n_kv=n_kv, group=group, head_dim=head_dim)
        st = state_conv[l]
        zeros = jnp.zeros((bs, ts, d_conv), F32)
        fill1 = zeros.at[:, 0].set(st[:, 1]).reshape(ms, d_conv)
        fill2 = zeros.at[:, 0].set(st[:, 0]).at[:, 1].set(st[:, 1]).reshape(ms, d_conv)
        fill1 = jnp.pad(fill1, ((tm - ms, 0), (0, 0)))
        fill2 = jnp.pad(fill2, ((tm - ms, 0), (0, 0)))
        yconv, u = _conv_proj(x1b, w_in_b, conv_w, t_idx, fill1, fill2, l, tm=tm, tc=tc, col0=conv_col0,
                              d_conv=d_conv)
        x2 = _mix_ln(x1, attn, yconv, w_in_b, w_a_b, w_c_b, w_o_b, ln_g, ln_b, 3 * l + 1, alpha,
                     tm=tm, tc=tc, ga_col0=ga_col0, gc_col0=gc_col0)
        if l + 1 < depth:
            x, f1_gu, f1_down = _ffn_ln(x2, f2_gu, f2_down, ln_g, ln_b, 3 * l + 2, alpha, tm=tm, tf=tf,
                                        cast_jobs=[(ffn1_gu, l + 1), (ffn1_down, l + 1)])
            x_tail = None
        else:
            y_p, y_s = _ffn_ln(x2, f2_gu, f2_down, ln_g, ln_b, 3 * l + 2, alpha, tm=tm, tf=tf,
                               split_rows=mp)

        ks_p.append(_seq_tails(k, bp, tp, WINDOW).reshape(bp, WINDOW, n_kv, head_dim))
        vs_p.append(_seq_tails(v, bp, tp, WINDOW).reshape(bp, WINDOW, n_kv, head_dim))
        cs_p.append(_seq_tails(u, bp, tp, CONV_W - 1))
        k_new = k[mp:].reshape(bs, ts, n_kv, head_dim)
        v_new = v[mp:].reshape(bs, ts, n_kv, head_dim)
        ks_s.append(jnp.concatenate([cache_k_win[l][:, ts:], k_new], axis=1))
        vs_s.append(jnp.concatenate([cache_v_win[l][:, ts:], v_new], axis=1))
        cs_s.append(u[mp:].reshape(bs, ts, d_conv)[:, -(CONV_W - 1):])

    return (y_p.reshape(bp, tp, d), y_s.reshape(bs, ts, d), jnp.stack(ks_p), jnp.stack(vs_p),
            jnp.stack(cs_p), jnp.stack(ks_s), jnp.stack(vs_s), jnp.stack(cs_s))
```

```python
import functools

import jax
import jax.numpy as jnp
from jax import lax
from jax.experimental import pallas as pl
from jax.experimental.pallas import tpu as pltpu

F32 = jnp.float32
BF16 = jnp.bfloat16

PAST_LEN = 16384
WINDOW = 128
ROPE_THETA = 10000.0
LN_EPS = 1e-5
CONV_W = 3

LANES = 128
SUBLANES = 8
BF16_ROWS = 16
VMEM_LIMIT_BYTES = 62 * 1024 * 1024

KEY_SLOTS = 2 * WINDOW
MAX_ROW_TILE = 640
MIX_ROW_TILE = 512
SAMPLE_SEQS_PER_STEP = 8


def _params(n_axes):
    return pltpu.CompilerParams(
        dimension_semantics=("arbitrary",) * n_axes,
        vmem_limit_bytes=VMEM_LIMIT_BYTES,
    )


def _layer_norm(y, g, b):
    mu = jnp.mean(y, axis=-1, keepdims=True)
    d = y - mu
    var = jnp.mean(d * d, axis=-1, keepdims=True)
    return d * lax.rsqrt(var + LN_EPS) * g + b


def _round_up(n, k):
    return -(-n // k) * k


def _ffn_ln_kernel(alpha, has_tail, emit_bf16, split, n_jobs, *refs):
    refs = list(refs)
    x_ref = refs.pop(0)
    xt_ref = refs.pop(0) if has_tail else None
    wg_ref, wu_ref, wd_ref, g_ref, b_ref = refs[:5]
    del refs[:5]
    job_src = refs[:n_jobs]
    del refs[:n_jobs]
    o_ref = refs.pop(0)
    ob_ref = refs.pop(0) if emit_bf16 else None
    ot_ref = refs.pop(0) if split else None
    job_dst = refs[:n_jobs]
    del refs[:n_jobs]
    xb_ref, acc_ref = refs
    i = pl.program_id(0)
    j = pl.program_id(1)
    last_tile = i == pl.num_programs(0) - 1
    tm = x_ref.shape[0]

    def load_x():
        x = x_ref[...]
        if xt_ref is None:
            return x
        merged = jnp.concatenate([x[:tm - xt_ref.shape[0]], xt_ref[...]], axis=0)
        return jnp.where(last_tile, merged, x)

    @pl.when(j == 0)
    def _():
        xb_ref[...] = load_x().astype(BF16)
        acc_ref[...] = jnp.zeros_like(acc_ref)

    xb = xb_ref[...]
    g = jnp.dot(xb, wg_ref[...], preferred_element_type=F32)
    u = jnp.dot(xb, wu_ref[...], preferred_element_type=F32)
    h = (jax.nn.silu(g) * u).astype(BF16)
    acc_ref[...] += jnp.dot(h, wd_ref[...], preferred_element_type=F32)
    for src, dst in zip(job_src, job_dst):
        dst[...] = src[...].astype(BF16)

    @pl.when(j == pl.num_programs(1) - 1)
    def _():
        y = alpha * load_x() + 0.5 * acc_ref[...]
        out = _layer_norm(y, g_ref[...], b_ref[...])
        o_ref[...] = out
        if emit_bf16:
            ob_ref[...] = out.astype(BF16)
        if split:
            @pl.when(last_tile)
            def _():
                ot_ref[...] = out[tm - ot_ref.shape[0]:, :]


def _ffn_ln(x, w_gu, w_down, ln_g, ln_b, ln_idx, alpha, *, tm, tf, emit_bf16=False, x_tail=None,
            split_rows=None, cast_jobs=()):
    d = x.shape[1]
    m = x.shape[0] + (0 if x_tail is None else x_tail.shape[0])
    f = w_down.shape[0]
    nj = f // tf
    n_tiles = m // tm
    row = lambda i, j: (i, 0)
    fixed = lambda i, j: (0, 0)
    in_specs = [pl.BlockSpec((tm, d), row)]
    args = [x]
    if x_tail is not None:
        in_specs.append(pl.BlockSpec(x_tail.shape, fixed))
        args.append(x_tail)
    in_specs += [
        pl.BlockSpec((d, tf), lambda i, j: (0, j)),
        pl.BlockSpec((d, tf), lambda i, j: (0, nj + j)),
        pl.BlockSpec((tf, d), lambda i, j: (j, 0)),
        pl.BlockSpec((None, 1, d), lambda i, j: (ln_idx, 0, 0)),
        pl.BlockSpec((None, 1, d), lambda i, j: (ln_idx, 0, 0)),
    ]
    args += [w_gu, w_gu, w_down, ln_g, ln_b]
    m_head = m if split_rows is None else split_rows
    out_shape = [jax.ShapeDtypeStruct((m_head, d), F32)]
    out_specs = [pl.BlockSpec((tm, d), row)]
    if emit_bf16:
        out_shape.append(jax.ShapeDtypeStruct((m, d), BF16))
        out_specs.append(pl.BlockSpec((tm, d), row))
    if split_rows is not None:
        out_shape.append(jax.ShapeDtypeStruct((m - split_rows, d), F32))
        out_specs.append(pl.BlockSpec((m - split_rows, d), fixed))
    n_steps = n_tiles * nj
    for w, layer in cast_jobs:
        _, r, c = w.shape
        rb = _round_up(-(-r // n_steps), BF16_ROWS)
        nb = -(-r // rb)
        block = lambda i, j, nb=nb: (jnp.minimum(i * nj + j, nb - 1), 0)
        in_specs.append(pl.BlockSpec((None, rb, c), lambda i, j, nb=nb, layer=layer:
                                     (layer, jnp.minimum(i * nj + j, nb - 1), 0)))
        args.append(w)
        out_shape.append(jax.ShapeDtypeStruct((r, c), BF16))
        out_specs.append(pl.BlockSpec((rb, c), block))
    return pl.pallas_call(
        functools.partial(_ffn_ln_kernel, alpha, x_tail is not None, emit_bf16, split_rows is not None,
                          len(cast_jobs)),
        grid=(n_tiles, nj),
        in_specs=in_specs,
        out_specs=out_specs,
        out_shape=out_shape,
        scratch_shapes=[pltpu.VMEM((tm, d), BF16), pltpu.VMEM((tm, d), F32)],
        compiler_params=_params(2),
        name="ffn_ln",
    )(*args)


def _rope_cols(x, cos, sin_signed, first_half, half):
    fwd = pltpu.roll(x, x.shape[1] - half, 1)
    bwd = pltpu.roll(x, half, 1)
    return x * cos + jnp.where(first_half, fwd, bwd) * sin_signed


def _qkv_kernel(d_q, d_kv, head_dim, xb_ref, w_ref, cos_ref, sin_ref, q_ref, k_ref, v_ref):
    qkv = jnp.dot(xb_ref[...], w_ref[...], preferred_element_type=F32)
    cos = cos_ref[...]
    sin = sin_ref[...]
    half = head_dim // 2
    lane = lax.broadcasted_iota(jnp.int32, cos.shape, 1)
    first_half = (lane % head_dim) < half
    q_scale = head_dim ** -0.5
    for c in range(d_q // LANES):
        sl = slice(c * LANES, (c + 1) * LANES)
        q_ref[:, sl] = (_rope_cols(qkv[:, sl], cos, sin, first_half, half) * q_scale).astype(BF16)
    for c in range(d_kv // LANES):
        src = slice(d_q + c * LANES, d_q + (c + 1) * LANES)
        k_ref[:, c * LANES:(c + 1) * LANES] = _rope_cols(qkv[:, src], cos, sin, first_half, half)
    v_ref[...] = qkv[:, d_q + d_kv:]


def _qkv_proj(xb, w_in, cos_tab, sin_tab, *, tm, d_q, d_kv, head_dim):
    m, d = xb.shape
    n_cols = d_q + 2 * d_kv
    return pl.pallas_call(
        functools.partial(_qkv_kernel, d_q, d_kv, head_dim),
        grid=(m // tm,),
        in_specs=[
            pl.BlockSpec((tm, d), lambda i: (i, 0)),
            pl.BlockSpec((d, n_cols), lambda i: (0, 0)),
            pl.BlockSpec((tm, LANES), lambda i: (i, 0)),
            pl.BlockSpec((tm, LANES), lambda i: (i, 0)),
        ],
        out_specs=[
            pl.BlockSpec((tm, d_q), lambda i: (i, 0)),
            pl.BlockSpec((tm, d_kv), lambda i: (i, 0)),
            pl.BlockSpec((tm, d_kv), lambda i: (i, 0)),
        ],
        out_shape=[
            jax.ShapeDtypeStruct((m, d_q), BF16),
            jax.ShapeDtypeStruct((m, d_kv), F32),
            jax.ShapeDtypeStruct((m, d_kv), F32),
        ],
        compiler_params=_params(1),
        name="qkv_proj",
    )(xb, w_in, cos_tab, sin_tab)


def _lane_tile4(x128, want_high):
    lane = lax.broadcasted_iota(jnp.int32, x128.shape, 1)
    swapped = pltpu.roll(x128, LANES // 2, 1)
    low = lane < LANES // 2
    both = jnp.where(low, swapped, x128) if want_high else jnp.where(low, x128, swapped)
    return jnp.concatenate([both, both], axis=1)


def _band_attention(problems, sink, n_kv, group, head_dim, between=None):
    rows = problems[0][0].shape[0]
    gw = group * head_dim
    r_idx = lax.broadcasted_iota(jnp.int32, (rows, KEY_SLOTS), 0)
    s_idx = lax.broadcasted_iota(jnp.int32, (rows, KEY_SLOTS), 1)
    head_of_lane = lax.broadcasted_iota(jnp.int32, (rows, gw), 1) // head_dim
    head_keep = [(head_of_lane == g).astype(F32) for g in range(group)]

    scores, values = [], []
    for q, kband, vband, row0, kpos0 in problems:
        diff = r_idx - row0 + WINDOW - s_idx
        valid = (diff >= 0) & (diff <= WINDOW) & (s_idx + kpos0 >= 0)
        for kh in range(n_kv):
            col = (kh * head_dim) // LANES
            high = ((kh * head_dim) % LANES) != 0
            kk = _lane_tile4(kband[:, col * LANES:(col + 1) * LANES], high).astype(BF16)
            values.append(_lane_tile4(vband[:, col * LANES:(col + 1) * LANES], high).astype(BF16))
            qg = q[:, kh * gw:(kh + 1) * gw]
            qs = jnp.concatenate([qg * head_keep[g] for g in range(group)], axis=0).astype(BF16)
            s = lax.dot_general(qs, kk, (((1,), (1,)), ((), ())), preferred_element_type=F32)
            scores.append(jnp.where(valid[None], s.reshape(group, rows, KEY_SLOTS), -jnp.inf))
    s = jnp.concatenate(scores, axis=0)
    sink_all = jnp.concatenate([sink] * len(problems), axis=0)
    mx = jnp.maximum(jnp.max(s, axis=-1, keepdims=True), sink_all)
    p = jnp.exp(s - mx)
    den = jnp.sum(p, axis=-1, keepdims=True) + jnp.exp(sink_all - mx)
    p = p * (1.0 / den)

    if between is not None:
        between()

    outs = []
    for i in range(len(problems)):
        slabs = []
        for kh in range(n_kv):
            c = i * n_kv + kh
            pg = p[c * group:(c + 1) * group].reshape(group * rows, KEY_SLOTS).astype(BF16)
            o = jnp.dot(pg, values[c], preferred_element_type=F32).reshape(group, rows, gw)
            out = o[0]
            for g in range(1, group):
                out = jnp.where(head_of_lane == g, o[g], out)
            slabs.append(out)
        outs.append(jnp.concatenate(slabs, axis=1))
    return outs


def _sink_column(sink_ref):
    return sink_ref[...][:, :, 0:1]


def _attn_sample_kernel(seq, n_kv, group, head_dim, sink_ref, q_ref, kc_ref, kn_ref, vc_ref, vn_ref,
                        _, o_ref):
    n_seq = kc_ref.shape[0]
    d_kv = kn_ref.shape[1]
    per_block = SUBLANES // seq
    pad = jnp.zeros((KEY_SLOTS - WINDOW - SUBLANES, d_kv), F32)
    qf = q_ref[...].astype(F32)
    problems = []
    for i in range(n_seq):
        blk, row0 = i // per_block, (i % per_block) * seq
        rows = slice(blk * SUBLANES, (blk + 1) * SUBLANES)
        k8, v8 = kn_ref[rows, :], vn_ref[rows, :]
        if row0:
            k8 = pltpu.roll(k8, SUBLANES - row0, 0)
            v8 = pltpu.roll(v8, SUBLANES - row0, 0)
        kband = jnp.concatenate([kc_ref[i], k8, pad], axis=0)
        vband = jnp.concatenate([vc_ref[i], v8, pad], axis=0)
        problems.append((qf[rows, :], kband, vband, row0, PAST_LEN - WINDOW))
    outs = _band_attention(problems, _sink_column(sink_ref), n_kv, group, head_dim)
    r_idx = lax.broadcasted_iota(jnp.int32, outs[0].shape, 0)
    blocks = []
    for blk in range(n_seq // per_block):
        out = outs[blk * per_block]
        for s in range(1, per_block):
            out = jnp.where(r_idx >= s * seq, outs[blk * per_block + s], out)
        blocks.append(out)
    o_ref[...] = jnp.concatenate(blocks, axis=0).astype(BF16)


def _attn_sample(attn, q, k, v, k_cache, v_cache, sink_tab, layer, *, row0, seq, seqs_per_step, n_kv,
                 group, head_dim):
    d_q = q.shape[1]
    d_kv = k.shape[1]
    nbatch = k_cache.shape[1]
    ns = seqs_per_step
    rows = ns * seq
    blk0 = row0 // rows
    new = lambda b: (blk0 + b, 0)
    cache = lambda b: (layer, b, 0, 0)
    return pl.pallas_call(
        functools.partial(_attn_sample_kernel, seq, n_kv, group, head_dim),
        grid=(nbatch // ns,),
        in_specs=[
            pl.BlockSpec((None, n_kv * group, 1, LANES), lambda b: (layer, 0, 0, 0)),
            pl.BlockSpec((rows, d_q), new),
            pl.BlockSpec((None, ns, WINDOW, d_kv), cache),
            pl.BlockSpec((rows, d_kv), new),
            pl.BlockSpec((None, ns, WINDOW, d_kv), cache),
            pl.BlockSpec((rows, d_kv), new),
            pl.BlockSpec(memory_space=pl.ANY),
        ],
        out_specs=pl.BlockSpec((rows, d_q), new),
        out_shape=jax.ShapeDtypeStruct(attn.shape, attn.dtype),
        input_output_aliases={6: 0},
        compiler_params=_params(1),
        name="attn_sample",
    )(sink_tab, q, k_cache, k, v_cache, v, attn)


def _conv_attn_kernel(nc, blocks_per_seq, n_kv, group, head_dim,
                      xb_ref, wb_ref, wc_ref, wh_ref, cw_ref, t_ref, f1_ref, f2_ref,
                      sink_ref, q_ref, kp_ref, kc_ref, vp_ref, vc_ref,
                      y_ref, u_ref, o_ref, ubuf):
    tm = xb_ref.shape[0]
    i = pl.program_id(0)
    c = pl.program_id(1)

    @pl.when(i == 0)
    def _():
        ubuf[c] = jnp.zeros(ubuf.shape[1:], F32)

    carry = ubuf[c]

    n_blocks = tm // WINDOW
    per_step = n_blocks // nc
    problems, starts = [], []
    for s in range(per_step):
        b = c * per_step + s
        r0 = pl.multiple_of(b * WINDOW, WINDOW)
        rp = pl.multiple_of(jnp.maximum(b - 1, 0) * WINDOW, WINDOW)
        n = lax.rem(i * n_blocks + b, blocks_per_seq)
        k_before = jnp.where(b == 0, kp_ref[...], kc_ref[pl.ds(rp, WINDOW), :])
        v_before = jnp.where(b == 0, vp_ref[...], vc_ref[pl.ds(rp, WINDOW), :])
        kband = jnp.concatenate([k_before, kc_ref[pl.ds(r0, WINDOW), :]], axis=0)
        vband = jnp.concatenate([v_before, vc_ref[pl.ds(r0, WINDOW), :]], axis=0)
        problems.append((q_ref[pl.ds(r0, WINDOW), :].astype(F32), kband, vband, 0, (n - 1) * WINDOW))
        starts.append(r0)
    proj = []

    def conv_matmuls():
        xb = xb_ref[...]
        for w_ref in (wb_ref, wc_ref, wh_ref):
            proj.append(jnp.dot(xb, w_ref[...], preferred_element_type=F32))

    outs = _band_attention(problems, _sink_column(sink_ref), n_kv, group, head_dim, between=conv_matmuls)
    cb, cc, ch = proj
    u = cc * ch
    row8 = lax.broadcasted_iota(jnp.int32, carry.shape, 0)

    def shifted(k):
        r = pltpu.roll(u, k, 0)
        head = jnp.where(row8 < k, pltpu.roll(carry, k, 0), r[0:SUBLANES])
        return jnp.concatenate([head, r[SUBLANES:]], axis=0)

    t = t_ref[...]
    last = i == pl.num_programs(0) - 1
    u_m1 = jnp.where(t >= 1, shifted(1), jnp.where(last, f1_ref[...], 0.0))
    u_m2 = jnp.where(t >= 2, shifted(2), jnp.where(last, f2_ref[...], 0.0))
    cw = cw_ref[...]
    conv = cw[0:1, :] * u_m2 + cw[1:2, :] * u_m1 + cw[2:3, :] * u

    for r0, out in zip(starts, outs):
        o_ref[pl.ds(r0, WINDOW), :] = out.astype(BF16)
    y_ref[...] = (cb * conv).astype(BF16)
    u_ref[...] = u
    ubuf[c] = u[tm - SUBLANES:, :]


def _conv_attn(xb, w_in, conv_w, t_idx, fill1, fill2, q, k, v, sink_tab, layer, *, tm, tc, col0, d_conv,
               seq, n_kv, group, head_dim):
    m, d = xb.shape
    d_q = q.shape[1]
    d_kv = k.shape[1]
    nc = d_conv // tc
    off = col0 // tc
    n_blocks = tm // WINDOW
    w_spec = lambda kk: pl.BlockSpec((d, tc), lambda i, c: (0, off + kk * nc + c))
    tile = lambda i, c: (i, 0)
    before = lambda i, c: (jnp.maximum(i * n_blocks - 1, 0), 0)
    return pl.pallas_call(
        functools.partial(_conv_attn_kernel, nc, seq // WINDOW, n_kv, group, head_dim),
        grid=(pl.cdiv(m, tm), nc),
        in_specs=[
            pl.BlockSpec((tm, d), tile),
            w_spec(0), w_spec(1), w_spec(2),
            pl.BlockSpec((None, CONV_W, tc), lambda i, c: (layer, 0, c)),
            pl.BlockSpec((tm, 1), tile),
            pl.BlockSpec((tm, tc), lambda i, c: (0, c)),
            pl.BlockSpec((tm, tc), lambda i, c: (0, c)),
            pl.BlockSpec((None, n_kv * group, 1, LANES), lambda i, c: (layer, 0, 0, 0)),
            pl.BlockSpec((tm, d_q), tile),
            pl.BlockSpec((WINDOW, d_kv), before),
            pl.BlockSpec((tm, d_kv), tile),
            pl.BlockSpec((WINDOW, d_kv), before),
            pl.BlockSpec((tm, d_kv), tile),
        ],
        out_specs=[
            pl.BlockSpec((tm, tc), lambda i, c: (i, c)),
            pl.BlockSpec((tm, tc), lambda i, c: (i, c)),
            pl.BlockSpec((tm, d_q), tile),
        ],
        out_shape=[
            jax.ShapeDtypeStruct((m, d_conv), BF16),
            jax.ShapeDtypeStruct((m, d_conv), F32),
            jax.ShapeDtypeStruct((m, d_q), BF16),
        ],
        scratch_shapes=[pltpu.VMEM((nc, SUBLANES, tc), F32)],
        compiler_params=_params(2),
        name="conv_attn",
    )(xb, w_in, w_in, w_in, conv_w, t_idx, fill1, fill2, sink_tab, q, k, k, v, v)


def _mix_ln_kernel(alpha, x_ref, a_ref, c_ref, wga_ref, wgc_ref, wa_ref, wc_ref, wo_ref, g_ref, b_ref,
                   o_ref, xb_ref, acc_ref):
    j = pl.program_id(1)

    @pl.when(j == 0)
    def _():
        xb_ref[...] = x_ref[...].astype(BF16)
        acc_ref[...] = jnp.zeros_like(acc_ref)

    xb = xb_ref[...]
    ga = jnp.dot(xb, wga_ref[...], preferred_element_type=F32)
    gc = jnp.dot(xb, wgc_ref[...], preferred_element_type=F32)
    pa = jnp.dot(a_ref[...], wa_ref[...], preferred_element_type=F32)
    pc = jnp.dot(c_ref[...], wc_ref[...], preferred_element_type=F32)
    merged = (jax.nn.sigmoid(ga) * pa + jax.nn.sigmoid(gc) * pc).astype(BF16)
    acc_ref[...] += jnp.dot(merged, wo_ref[...], preferred_element_type=F32)

    @pl.when(j == pl.num_programs(1) - 1)
    def _():
        y = alpha * x_ref[...] + acc_ref[...]
        o_ref[...] = _layer_norm(y, g_ref[...], b_ref[...])


def _mix_ln(x, attn, yconv, w_in, w_a, w_c, w_o, ln_g, ln_b, ln_idx, alpha, *, tm, tc, ga_col0, gc_col0):
    m, d = x.shape
    d_q = attn.shape[1]
    d_conv = yconv.shape[1]
    return pl.pallas_call(
        functools.partial(_mix_ln_kernel, alpha),
        grid=(m // tm, d // tc),
        in_specs=[
            pl.BlockSpec((tm, d), lambda i, j: (i, 0)),
            pl.BlockSpec((tm, d_q), lambda i, j: (i, 0)),
            pl.BlockSpec((tm, d_conv), lambda i, j: (i, 0)),
            pl.BlockSpec((d, tc), lambda i, j: (0, ga_col0 // tc + j)),
            pl.BlockSpec((d, tc), lambda i, j: (0, gc_col0 // tc + j)),
            pl.BlockSpec((d_q, tc), lambda i, j: (0, j)),
            pl.BlockSpec((d_conv, tc), lambda i, j: (0, j)),
            pl.BlockSpec((tc, d), lambda i, j: (j, 0)),
            pl.BlockSpec((None, 1, d), lambda i, j: (ln_idx, 0, 0)),
            pl.BlockSpec((None, 1, d), lambda i, j: (ln_idx, 0, 0)),
        ],
        out_specs=pl.BlockSpec((tm, d), lambda i, j: (i, 0)),
        out_shape=jax.ShapeDtypeStruct((m, d), F32),
        scratch_shapes=[pltpu.VMEM((tm, d), BF16), pltpu.VMEM((tm, d), F32)],
        compiler_params=_params(2),
        name="mix_ln",
    )(x, attn, yconv, w_in, w_in, w_a, w_c, w_o, ln_g, ln_b)


def _rope_tables(pos, head_dim):
    inv_freq = ROPE_THETA ** (-jnp.arange(0, head_dim, 2, dtype=F32) / head_dim)
    ang = pos.astype(F32)[:, None] * inv_freq[None, :]
    cos = jnp.cos(ang)
    sin = jnp.sin(ang)
    reps = LANES // head_dim
    cos_tab = jnp.tile(jnp.concatenate([cos, cos], axis=1), (1, reps))
    sin_tab = jnp.tile(jnp.concatenate([-sin, sin], axis=1), (1, reps))
    return cos_tab, sin_tab


def _largest_divisor(n, cap, multiple_of=1):
    return max(k for k in range(multiple_of, cap + 1, multiple_of) if n % k == 0)


def _seq_tails(a, n_seq, seq, rows):
    return jnp.stack([a[(b + 1) * seq - rows:(b + 1) * seq] for b in range(n_seq)])


def kernel(x_prompt, x_sample, cache_k_win, cache_v_win, state_conv, ln_g, ln_b, w_in, sinks, conv_w,
           w_branch_attn, w_branch_conv, w_out, ffn1_gu, ffn1_down, ffn2_gu, ffn2_down):
    depth = w_in.shape[0]
    bp, tp, d = x_prompt.shape
    bs, ts, _ = x_sample.shape
    n_kv, head_dim = cache_k_win.shape[-2:]
    d_q = w_branch_attn.shape[1]
    d_conv = conv_w.shape[-1]
    d_kv = n_kv * head_dim
    group = d_q // d_kv
    mp, ms = bp * tp, bs * ts
    m = mp + ms
    alpha = (2.0 * depth) ** 0.25
    tm = _largest_divisor(m, MAX_ROW_TILE, BF16_ROWS)
    ns = _largest_divisor(bs, SAMPLE_SEQS_PER_STEP, SUBLANES // ts)
    assert tm >= ms and SUBLANES % ts == 0 and mp % (ns * ts) == 0 and tp % WINDOW == 0
    assert ts >= CONV_W - 1 and tp >= WINDOW
    tf = 512
    tc = 512
    tm_mix = MIX_ROW_TILE
    assert mp % tm_mix == 0 and ms <= tm_mix and tm_mix % (WINDOW * (d_conv // tc)) == 0
    conv_col0 = d_q + 2 * d_kv
    ga_col0 = conv_col0 + 3 * d_conv
    gc_col0 = ga_col0 + d

    ln_g = ln_g.reshape(depth * 3, 1, d)
    ln_b = ln_b.reshape(depth * 3, 1, d)
    sink_tab = jnp.broadcast_to(sinks[:, :, None, None], sinks.shape + (1, LANES))

    t_prompt = jnp.tile(jnp.arange(tp, dtype=jnp.int32), bp)
    t_sample = jnp.tile(jnp.arange(ts, dtype=jnp.int32), bs)
    cos_tab, sin_tab = _rope_tables(jnp.concatenate([t_prompt, PAST_LEN + t_sample]), head_dim)
    t_idx = jnp.concatenate([t_prompt, t_sample]).reshape(m, 1)
    k_cache = cache_k_win.reshape(depth, bs, WINDOW, d_kv)
    v_cache = cache_v_win.reshape(depth, bs, WINDOW, d_kv)

    f1_gu, f1_down = ffn1_gu[0].astype(BF16), ffn1_down[0].astype(BF16)
    x = x_prompt.reshape(mp, d)
    x_tail = x_sample.reshape(ms, d)
    ks_p, vs_p, cs_p, ks_s, vs_s, cs_s = [], [], [], [], [], []
    for l in range(depth):
        jobs = [(w, l) for w in (w_in, w_branch_attn, w_branch_conv, w_out, ffn2_gu, ffn2_down)]
        x1, x1b, w_in_b, w_a_b, w_c_b, w_o_b, f2_gu, f2_down = _ffn_ln(
            x, f1_gu, f1_down, ln_g, ln_b, 3 * l, alpha, tm=tm, tf=tf, emit_bf16=True, x_tail=x_tail,
            cast_jobs=jobs)
        q, k, v = _qkv_proj(x1b, w_in_b, cos_tab, sin_tab, tm=tm, d_q=d_q, d_kv=d_kv, head_dim=head_dim)
        st = state_conv[l]
        zeros = jnp.zeros((bs, ts, d_conv), F32)
        fill1 = zeros.at[:, 0].set(st[:, 1]).reshape(ms, d_conv)
        fill2 = zeros.at[:, 0].set(st[:, 0]).at[:, 1].set(st[:, 1]).reshape(ms, d_conv)
        fill1 = jnp.pad(fill1, ((0, tm_mix - ms), (0, 0)))
        fill2 = jnp.pad(fill2, ((0, tm_mix - ms), (0, 0)))
        yconv, u, attn = _conv_attn(x1b, w_in_b, conv_w, t_idx, fill1, fill2, q, k, v, sink_tab, l,
                                    tm=tm_mix, tc=tc, col0=conv_col0, d_conv=d_conv, seq=tp, n_kv=n_kv,
                                    group=group, head_dim=head_dim)
        attn = _attn_sample(attn, q, k, v, k_cache, v_cache, sink_tab, l, row0=mp, seq=ts,
                            seqs_per_step=ns, n_kv=n_kv, group=group, head_dim=head_dim)
        x2 = _mix_ln(x1, attn, yconv, w_in_b, w_a_b, w_c_b, w_o_b, ln_g, ln_b, 3 * l + 1, alpha,
                     tm=tm, tc=tc, ga_col0=ga_col0, gc_col0=gc_col0)
        if l + 1 < depth:
            x, f1_gu, f1_down = _ffn_ln(x2, f2_gu, f2_down, ln_g, ln_b, 3 * l + 2, alpha, tm=tm, tf=tf,
                                        cast_jobs=[(ffn1_gu, l + 1), (ffn1_down, l + 1)])
            x_tail = None
        else:
            y_p, y_s = _ffn_ln(x2, f2_gu, f2_down, ln_g, ln_b, 3 * l + 2, alpha, tm=tm, tf=tf,
                               split_rows=mp)

        ks_p.append(_seq_tails(k, bp, tp, WINDOW).reshape(bp, WINDOW, n_kv, head_dim))
        vs_p.append(_seq_tails(v, bp, tp, WINDOW).reshape(bp, WINDOW, n_kv, head_dim))
        cs_p.append(_seq_tails(u, bp, tp, CONV_W - 1))
        k_new = k[mp:].reshape(bs, ts, n_kv, head_dim)
        v_new = v[mp:].reshape(bs, ts, n_kv, head_dim)
        ks_s.append(jnp.concatenate([cache_k_win[l][:, ts:], k_new], axis=1))
        vs_s.append(jnp.concatenate([cache_v_win[l][:, ts:], v_new], axis=1))
        cs_s.append(u[mp:].reshape(bs, ts, d_conv)[:, -(CONV_W - 1):])

    return (y_p.reshape(bp, tp, d), y_s.reshape(bs, ts, d), jnp.stack(ks_p), jnp.stack(vs_p),
            jnp.stack(cs_p), jnp.stack(ks_s), jnp.stack(vs_s), jnp.stack(cs_s))
```

```python
import functools

import jax
import jax.numpy as jnp
from jax import lax
from jax.experimental import pallas as pl
from jax.experimental.pallas import tpu as pltpu

F32 = jnp.float32
BF16 = jnp.bfloat16

PAST_LEN = 16384
WINDOW = 128
ROPE_THETA = 10000.0
LN_EPS = 1e-5
CONV_W = 3

LANES = 128
SUBLANES = 8
BF16_ROWS = 16
VMEM_LIMIT_BYTES = 62 * 1024 * 1024

KEY_SLOTS = 2 * WINDOW
MAX_ROW_TILE = 640
MIX_ROW_TILE = 512
SAMPLE_SEQS_PER_STEP = 8


def _params(n_axes):
    return pltpu.CompilerParams(
        dimension_semantics=("arbitrary",) * n_axes,
        vmem_limit_bytes=VMEM_LIMIT_BYTES,
    )


def _layer_norm(y, g, b):
    mu = jnp.mean(y, axis=-1, keepdims=True)
    d = y - mu
    var = jnp.mean(d * d, axis=-1, keepdims=True)
    return d * lax.rsqrt(var + LN_EPS) * g + b


def _round_up(n, k):
    return -(-n // k) * k


def _ffn_ln_kernel(alpha, has_tail, emit_bf16, split, n_jobs, *refs):
    refs = list(refs)
    x_ref = refs.pop(0)
    xt_ref = refs.pop(0) if has_tail else None
    wg_ref, wu_ref, wd_ref, g_ref, b_ref = refs[:5]
    del refs[:5]
    job_src = refs[:n_jobs]
    del refs[:n_jobs]
    o_ref = refs.pop(0)
    ob_ref = refs.pop(0) if emit_bf16 else None
    ot_ref = refs.pop(0) if split else None
    job_dst = refs[:n_jobs]
    del refs[:n_jobs]
    xb_ref, acc_ref = refs
    i = pl.program_id(0)
    j = pl.program_id(1)
    last_tile = i == pl.num_programs(0) - 1
    tm = x_ref.shape[0]

    def load_x():
        x = x_ref[...]
        if xt_ref is None:
            return x
        merged = jnp.concatenate([x[:tm - xt_ref.shape[0]], xt_ref[...]], axis=0)
        return jnp.where(last_tile, merged, x)

    @pl.when(j == 0)
    def _():
        xb_ref[...] = load_x().astype(BF16)
        acc_ref[...] = jnp.zeros_like(acc_ref)

    xb = xb_ref[...]
    g = jnp.dot(xb, wg_ref[...], preferred_element_type=F32)
    u = jnp.dot(xb, wu_ref[...], preferred_element_type=F32)
    h = (jax.nn.silu(g) * u).astype(BF16)
    acc_ref[...] += jnp.dot(h, wd_ref[...], preferred_element_type=F32)
    for src, dst in zip(job_src, job_dst):
        dst[...] = src[...].astype(BF16)

    @pl.when(j == pl.num_programs(1) - 1)
    def _():
        y = alpha * load_x() + 0.5 * acc_ref[...]
        out = _layer_norm(y, g_ref[...], b_ref[...])
        o_ref[...] = out
        if emit_bf16:
            ob_ref[...] = out.astype(BF16)
        if split:
            @pl.when(last_tile)
            def _():
                ot_ref[...] = out[tm - ot_ref.shape[0]:, :]


def _ffn_ln(x, w_gu, w_down, ln_g, ln_b, ln_idx, alpha, *, tm, tf, emit_bf16=False, x_tail=None,
            split_rows=None, cast_jobs=()):
    d = x.shape[1]
    m = x.shape[0] + (0 if x_tail is None else x_tail.shape[0])
    f = w_down.shape[0]
    nj = f // tf
    n_tiles = m // tm
    row = lambda i, j: (i, 0)
    fixed = lambda i, j: (0, 0)
    in_specs = [pl.BlockSpec((tm, d), row)]
    args = [x]
    if x_tail is not None:
        in_specs.append(pl.BlockSpec(x_tail.shape, fixed))
        args.append(x_tail)
    in_specs += [
        pl.BlockSpec((d, tf), lambda i, j: (0, j)),
        pl.BlockSpec((d, tf), lambda i, j: (0, nj + j)),
        pl.BlockSpec((tf, d), lambda i, j: (j, 0)),
        pl.BlockSpec((None, 1, d), lambda i, j: (ln_idx, 0, 0)),
        pl.BlockSpec((None, 1, d), lambda i, j: (ln_idx, 0, 0)),
    ]
    args += [w_gu, w_gu, w_down, ln_g, ln_b]
    m_head = m if split_rows is None else split_rows
    out_shape = [jax.ShapeDtypeStruct((m_head, d), F32)]
    out_specs = [pl.BlockSpec((tm, d), row)]
    if emit_bf16:
        out_shape.append(jax.ShapeDtypeStruct((m, d), BF16))
        out_specs.append(pl.BlockSpec((tm, d), row))
    if split_rows is not None:
        out_shape.append(jax.ShapeDtypeStruct((m - split_rows, d), F32))
        out_specs.append(pl.BlockSpec((m - split_rows, d), fixed))
    n_steps = n_tiles * nj
    for w, layer in cast_jobs:
        _, r, c = w.shape
        rb = _round_up(-(-r // n_steps), BF16_ROWS)
        nb = -(-r // rb)
        block = lambda i, j, nb=nb: (jnp.minimum(i * nj + j, nb - 1), 0)
        in_specs.append(pl.BlockSpec((None, rb, c), lambda i, j, nb=nb, layer=layer:
                                     (layer, jnp.minimum(i * nj + j, nb - 1), 0)))
        args.append(w)
        out_shape.append(jax.ShapeDtypeStruct((r, c), BF16))
        out_specs.append(pl.BlockSpec((rb, c), block))
    return pl.pallas_call(
        functools.partial(_ffn_ln_kernel, alpha, x_tail is not None, emit_bf16, split_rows is not None,
                          len(cast_jobs)),
        grid=(n_tiles, nj),
        in_specs=in_specs,
        out_specs=out_specs,
        out_shape=out_shape,
        scratch_shapes=[pltpu.VMEM((tm, d), BF16), pltpu.VMEM((tm, d), F32)],
        compiler_params=_params(2),
        name="ffn_ln",
    )(*args)


def _rope_cols(x, cos, sin_signed, first_half, half):
    fwd = pltpu.roll(x, x.shape[1] - half, 1)
    bwd = pltpu.roll(x, half, 1)
    return x * cos + jnp.where(first_half, fwd, bwd) * sin_signed


def _qkv_kernel(d_q, d_kv, head_dim, xb_ref, w_ref, cos_ref, sin_ref, q_ref, k_ref, v_ref):
    xb = xb_ref[...]
    bounds = [0, d_q // 2, d_q, d_q + 2 * d_kv]
    parts = [jnp.dot(xb, w_ref[:, lo:hi], preferred_element_type=F32) for lo, hi in zip(bounds, bounds[1:])]
    q = jnp.concatenate(parts[:2], axis=1)
    kv = parts[2]
    cos = cos_ref[...]
    sin = sin_ref[...]
    half = head_dim // 2
    lane = lax.broadcasted_iota(jnp.int32, cos.shape, 1)
    first_half = (lane % head_dim) < half
    q_scale = head_dim ** -0.5
    for c in range(d_q // LANES):
        sl = slice(c * LANES, (c + 1) * LANES)
        q_ref[:, sl] = (_rope_cols(q[:, sl], cos, sin, first_half, half) * q_scale).astype(BF16)
    for c in range(d_kv // LANES):
        sl = slice(c * LANES, (c + 1) * LANES)
        k_ref[:, sl] = _rope_cols(kv[:, sl], cos, sin, first_half, half)
    v_ref[...] = kv[:, d_kv:]


def _qkv_proj(xb, w_in, cos_tab, sin_tab, *, tm, d_q, d_kv, head_dim):
    m, d = xb.shape
    n_cols = d_q + 2 * d_kv
    return pl.pallas_call(
        functools.partial(_qkv_kernel, d_q, d_kv, head_dim),
        grid=(m // tm,),
        in_specs=[
            pl.BlockSpec((tm, d), lambda i: (i, 0)),
            pl.BlockSpec((d, n_cols), lambda i: (0, 0)),
            pl.BlockSpec((tm, LANES), lambda i: (i, 0)),
            pl.BlockSpec((tm, LANES), lambda i: (i, 0)),
        ],
        out_specs=[
            pl.BlockSpec((tm, d_q), lambda i: (i, 0)),
            pl.BlockSpec((tm, d_kv), lambda i: (i, 0)),
            pl.BlockSpec((tm, d_kv), lambda i: (i, 0)),
        ],
        out_shape=[
            jax.ShapeDtypeStruct((m, d_q), BF16),
            jax.ShapeDtypeStruct((m, d_kv), F32),
            jax.ShapeDtypeStruct((m, d_kv), F32),
        ],
        compiler_params=_params(1),
        name="qkv_proj",
    )(xb, w_in, cos_tab, sin_tab)


def _lane_tile4(x128, want_high):
    lane = lax.broadcasted_iota(jnp.int32, x128.shape, 1)
    swapped = pltpu.roll(x128, LANES // 2, 1)
    low = lane < LANES // 2
    both = jnp.where(low, swapped, x128) if want_high else jnp.where(low, x128, swapped)
    return jnp.concatenate([both, both], axis=1)


def _band_attention(problems, sink, n_kv, group, head_dim, between=None):
    rows = problems[0][0].shape[0]
    gw = group * head_dim
    r_idx = lax.broadcasted_iota(jnp.int32, (rows, KEY_SLOTS), 0)
    s_idx = lax.broadcasted_iota(jnp.int32, (rows, KEY_SLOTS), 1)
    head_of_lane = lax.broadcasted_iota(jnp.int32, (rows, gw), 1) // head_dim
    head_keep = [(head_of_lane == g).astype(F32) for g in range(group)]

    scores, values = [], []
    for q, kband, vband, row0, kpos0 in problems:
        diff = r_idx - row0 + WINDOW - s_idx
        valid = (diff >= 0) & (diff <= WINDOW) & (s_idx + kpos0 >= 0)
        for kh in range(n_kv):
            col = (kh * head_dim) // LANES
            high = ((kh * head_dim) % LANES) != 0
            kk = _lane_tile4(kband[:, col * LANES:(col + 1) * LANES], high).astype(BF16)
            values.append(_lane_tile4(vband[:, col * LANES:(col + 1) * LANES], high).astype(BF16))
            qg = q[:, kh * gw:(kh + 1) * gw]
            qs = jnp.concatenate([qg * head_keep[g] for g in range(group)], axis=0).astype(BF16)
            s = lax.dot_general(qs, kk, (((1,), (1,)), ((), ())), preferred_element_type=F32)
            scores.append(jnp.where(valid[None], s.reshape(group, rows, KEY_SLOTS), -jnp.inf))
    s = jnp.concatenate(scores, axis=0)
    sink_all = jnp.concatenate([sink] * len(problems), axis=0)
    mx = jnp.maximum(jnp.max(s, axis=-1, keepdims=True), sink_all)
    p = jnp.exp(s - mx)
    den = jnp.sum(p, axis=-1, keepdims=True) + jnp.exp(sink_all - mx)
    p = p * (1.0 / den)

    if between is not None:
        between()

    outs = []
    for i in range(len(problems)):
        slabs = []
        for kh in range(n_kv):
            c = i * n_kv + kh
            pg = p[c * group:(c + 1) * group].reshape(group * rows, KEY_SLOTS).astype(BF16)
            o = jnp.dot(pg, values[c], preferred_element_type=F32).reshape(group, rows, gw)
            out = o[0]
            for g in range(1, group):
                out = jnp.where(head_of_lane == g, o[g], out)
            slabs.append(out)
        outs.append(jnp.concatenate(slabs, axis=1))
    return outs


def _sink_column(sink_ref):
    return sink_ref[...][:, :, 0:1]


def _attn_sample_kernel(seq, n_kv, group, head_dim, sink_ref, q_ref, kc_ref, kn_ref, vc_ref, vn_ref,
                        _, o_ref):
    n_seq = kc_ref.shape[0]
    d_kv = kn_ref.shape[1]
    per_block = SUBLANES // seq
    pad = jnp.zeros((KEY_SLOTS - WINDOW - SUBLANES, d_kv), F32)
    qf = q_ref[...].astype(F32)
    problems = []
    for i in range(n_seq):
        blk, row0 = i // per_block, (i % per_block) * seq
        rows = slice(blk * SUBLANES, (blk + 1) * SUBLANES)
        k8, v8 = kn_ref[rows, :], vn_ref[rows, :]
        if row0:
            k8 = pltpu.roll(k8, SUBLANES - row0, 0)
            v8 = pltpu.roll(v8, SUBLANES - row0, 0)
        kband = jnp.concatenate([kc_ref[i], k8, pad], axis=0)
        vband = jnp.concatenate([vc_ref[i], v8, pad], axis=0)
        problems.append((qf[rows, :], kband, vband, row0, PAST_LEN - WINDOW))
    outs = _band_attention(problems, _sink_column(sink_ref), n_kv, group, head_dim)
    r_idx = lax.broadcasted_iota(jnp.int32, outs[0].shape, 0)
    blocks = []
    for blk in range(n_seq // per_block):
        out = outs[blk * per_block]
        for s in range(1, per_block):
            out = jnp.where(r_idx >= s * seq, outs[blk * per_block + s], out)
        blocks.append(out)
    o_ref[...] = jnp.concatenate(blocks, axis=0).astype(BF16)


def _attn_sample(attn, q, k, v, k_cache, v_cache, sink_tab, layer, *, row0, seq, seqs_per_step, n_kv,
                 group, head_dim):
    d_q = q.shape[1]
    d_kv = k.shape[1]
    nbatch = k_cache.shape[1]
    ns = seqs_per_step
    rows = ns * seq
    blk0 = row0 // rows
    new = lambda b: (blk0 + b, 0)
    cache = lambda b: (layer, b, 0, 0)
    return pl.pallas_call(
        functools.partial(_attn_sample_kernel, seq, n_kv, group, head_dim),
        grid=(nbatch // ns,),
        in_specs=[
            pl.BlockSpec((None, n_kv * group, 1, LANES), lambda b: (layer, 0, 0, 0)),
            pl.BlockSpec((rows, d_q), new),
            pl.BlockSpec((None, ns, WINDOW, d_kv), cache),
            pl.BlockSpec((rows, d_kv), new),
            pl.BlockSpec((None, ns, WINDOW, d_kv), cache),
            pl.BlockSpec((rows, d_kv), new),
            pl.BlockSpec(memory_space=pl.ANY),
        ],
        out_specs=pl.BlockSpec((rows, d_q), new),
        out_shape=jax.ShapeDtypeStruct(attn.shape, attn.dtype),
        input_output_aliases={6: 0},
        compiler_params=_params(1),
        name="attn_sample",
    )(sink_tab, q, k_cache, k, v_cache, v, attn)


def _conv_attn_kernel(nc, blocks_per_seq, n_kv, group, head_dim,
                      xb_ref, wb_ref, wc_ref, wh_ref, cw_ref, t_ref, f1_ref, f2_ref,
                      sink_ref, q_ref, kp_ref, kc_ref, vp_ref, vc_ref,
                      y_ref, u_ref, o_ref, ubuf):
    tm = xb_ref.shape[0]
    i = pl.program_id(0)
    c = pl.program_id(1)

    @pl.when(i == 0)
    def _():
        ubuf[c] = jnp.zeros(ubuf.shape[1:], F32)

    carry = ubuf[c]

    n_blocks = tm // WINDOW
    per_step = n_blocks // nc
    problems, starts = [], []
    for s in range(per_step):
        b = c * per_step + s
        r0 = pl.multiple_of(b * WINDOW, WINDOW)
        rp = pl.multiple_of(jnp.maximum(b - 1, 0) * WINDOW, WINDOW)
        n = lax.rem(i * n_blocks + b, blocks_per_seq)
        k_before = jnp.where(b == 0, kp_ref[...], kc_ref[pl.ds(rp, WINDOW), :])
        v_before = jnp.where(b == 0, vp_ref[...], vc_ref[pl.ds(rp, WINDOW), :])
        kband = jnp.concatenate([k_before, kc_ref[pl.ds(r0, WINDOW), :]], axis=0)
        vband = jnp.concatenate([v_before, vc_ref[pl.ds(r0, WINDOW), :]], axis=0)
        problems.append((q_ref[pl.ds(r0, WINDOW), :].astype(F32), kband, vband, 0, (n - 1) * WINDOW))
        starts.append(r0)
    proj = []

    def conv_input_matmuls():
        xb = xb_ref[...]
        for w_ref in (wc_ref, wh_ref):
            proj.append(jnp.dot(xb, w_ref[...], preferred_element_type=F32))

    outs = _band_attention(problems, _sink_column(sink_ref), n_kv, group, head_dim,
                           between=conv_input_matmuls)
    cc, ch = proj
    cb = jnp.dot(xb_ref[...], wb_ref[...], preferred_element_type=F32)
    u = cc * ch
    row8 = lax.broadcasted_iota(jnp.int32, carry.shape, 0)

    def shifted(k):
        r = pltpu.roll(u, k, 0)
        head = jnp.where(row8 < k, pltpu.roll(carry, k, 0), r[0:SUBLANES])
        return jnp.concatenate([head, r[SUBLANES:]], axis=0)

    t = t_ref[...]
    last = i == pl.num_programs(0) - 1
    u_m1 = jnp.where(t >= 1, shifted(1), jnp.where(last, f1_ref[...], 0.0))
    u_m2 = jnp.where(t >= 2, shifted(2), jnp.where(last, f2_ref[...], 0.0))
    cw = cw_ref[...]
    conv = cw[0:1, :] * u_m2 + cw[1:2, :] * u_m1 + cw[2:3, :] * u

    for r0, out in zip(starts, outs):
        o_ref[pl.ds(r0, WINDOW), :] = out.astype(BF16)
    y_ref[...] = (cb * conv).astype(BF16)
    u_ref[...] = u
    ubuf[c] = u[tm - SUBLANES:, :]


def _conv_attn(xb, w_in, conv_w, t_idx, fill1, fill2, q, k, v, sink_tab, layer, *, tm, tc, col0, d_conv,
               seq, n_kv, group, head_dim):
    m, d = xb.shape
    d_q = q.shape[1]
    d_kv = k.shape[1]
    nc = d_conv // tc
    off = col0 // tc
    n_blocks = tm // WINDOW
    w_spec = lambda kk: pl.BlockSpec((d, tc), lambda i, c: (0, off + kk * nc + c))
    tile = lambda i, c: (i, 0)
    before = lambda i, c: (jnp.maximum(i * n_blocks - 1, 0), 0)
    return pl.pallas_call(
        functools.partial(_conv_attn_kernel, nc, seq // WINDOW, n_kv, group, head_dim),
        grid=(pl.cdiv(m, tm), nc),
        in_specs=[
            pl.BlockSpec((tm, d), tile),
            w_spec(0), w_spec(1), w_spec(2),
            pl.BlockSpec((None, CONV_W, tc), lambda i, c: (layer, 0, c)),
            pl.BlockSpec((tm, 1), tile),
            pl.BlockSpec((tm, tc), lambda i, c: (0, c)),
            pl.BlockSpec((tm, tc), lambda i, c: (0, c)),
            pl.BlockSpec((None, n_kv * group, 1, LANES), lambda i, c: (layer, 0, 0, 0)),
            pl.BlockSpec((tm, d_q), tile),
            pl.BlockSpec((WINDOW, d_kv), before),
            pl.BlockSpec((tm, d_kv), tile),
            pl.BlockSpec((WINDOW, d_kv), before),
            pl.BlockSpec((tm, d_kv), tile),
        ],
        out_specs=[
            pl.BlockSpec((tm, tc), lambda i, c: (i, c)),
            pl.BlockSpec((tm, tc), lambda i, c: (i, c)),
            pl.BlockSpec((tm, d_q), tile),
        ],
        out_shape=[
            jax.ShapeDtypeStruct((m, d_conv), BF16),
            jax.ShapeDtypeStruct((m, d_conv), F32),
            jax.ShapeDtypeStruct((m, d_q), BF16),
        ],
        scratch_shapes=[pltpu.VMEM((nc, SUBLANES, tc), F32)],
        compiler_params=_params(2),
        name="conv_attn",
    )(xb, w_in, w_in, w_in, conv_w, t_idx, fill1, fill2, sink_tab, q, k, k, v, v)


def _mix_ln_kernel(alpha, x_ref, xb_ref, a_ref, c_ref, wga_ref, wgc_ref, wa_ref, wc_ref, wo_ref, g_ref,
                   b_ref, o_ref, acc_ref):
    j = pl.program_id(1)

    @pl.when(j == 0)
    def _():
        acc_ref[...] = jnp.zeros_like(acc_ref)

    xb = xb_ref[...]
    ga = jnp.dot(xb, wga_ref[...], preferred_element_type=F32)
    gc = jnp.dot(xb, wgc_ref[...], preferred_element_type=F32)
    pa = jnp.dot(a_ref[...], wa_ref[...], preferred_element_type=F32)
    pc = jnp.dot(c_ref[...], wc_ref[...], preferred_element_type=F32)
    merged = (jax.nn.sigmoid(ga) * pa + jax.nn.sigmoid(gc) * pc).astype(BF16)
    acc_ref[...] += jnp.dot(merged, wo_ref[...], preferred_element_type=F32)

    @pl.when(j == pl.num_programs(1) - 1)
    def _():
        y = alpha * x_ref[...] + acc_ref[...]
        o_ref[...] = _layer_norm(y, g_ref[...], b_ref[...])


def _mix_ln(x, xb, attn, yconv, w_in, w_a, w_c, w_o, ln_g, ln_b, ln_idx, alpha, *, tm, tc, ga_col0,
            gc_col0):
    m, d = x.shape
    d_q = attn.shape[1]
    d_conv = yconv.shape[1]
    return pl.pallas_call(
        functools.partial(_mix_ln_kernel, alpha),
        grid=(m // tm, d // tc),
        in_specs=[
            pl.BlockSpec((tm, d), lambda i, j: (i, 0)),
            pl.BlockSpec((tm, d), lambda i, j: (i, 0)),
            pl.BlockSpec((tm, d_q), lambda i, j: (i, 0)),
            pl.BlockSpec((tm, d_conv), lambda i, j: (i, 0)),
            pl.BlockSpec((d, tc), lambda i, j: (0, ga_col0 // tc + j)),
            pl.BlockSpec((d, tc), lambda i, j: (0, gc_col0 // tc + j)),
            pl.BlockSpec((d_q, tc), lambda i, j: (0, j)),
            pl.BlockSpec((d_conv, tc), lambda i, j: (0, j)),
            pl.BlockSpec((tc, d), lambda i, j: (j, 0)),
            pl.BlockSpec((None, 1, d), lambda i, j: (ln_idx, 0, 0)),
            pl.BlockSpec((None, 1, d), lambda i, j: (ln_idx, 0, 0)),
        ],
        out_specs=pl.BlockSpec((tm, d), lambda i, j: (i, 0)),
        out_shape=jax.ShapeDtypeStruct((m, d), F32),
        scratch_shapes=[pltpu.VMEM((tm, d), F32)],
        compiler_params=_params(2),
        name="mix_ln",
    )(x, xb, attn, yconv, w_in, w_in, w_a, w_c, w_o, ln_g, ln_b)


def _rope_tables(pos, head_dim):
    inv_freq = ROPE_THETA ** (-jnp.arange(0, head_dim, 2, dtype=F32) / head_dim)
    ang = pos.astype(F32)[:, None] * inv_freq[None, :]
    cos = jnp.cos(ang)
    sin = jnp.sin(ang)
    reps = LANES // head_dim
    cos_tab = jnp.tile(jnp.concatenate([cos, cos], axis=1), (1, reps))
    sin_tab = jnp.tile(jnp.concatenate([-sin, sin], axis=1), (1, reps))
    return cos_tab, sin_tab


def _largest_divisor(n, cap, multiple_of=1):
    return max(k for k in range(multiple_of, cap + 1, multiple_of) if n % k == 0)


def _seq_tails(a, n_seq, seq, rows):
    return jnp.stack([a[(b + 1) * seq - rows:(b + 1) * seq] for b in range(n_seq)])


def kernel(x_prompt, x_sample, cache_k_win, cache_v_win, state_conv, ln_g, ln_b, w_in, sinks, conv_w,
           w_branch_attn, w_branch_conv, w_out, ffn1_gu, ffn1_down, ffn2_gu, ffn2_down):
    depth = w_in.shape[0]
    bp, tp, d = x_prompt.shape
    bs, ts, _ = x_sample.shape
    n_kv, head_dim = cache_k_win.shape[-2:]
    d_q = w_branch_attn.shape[1]
    d_conv = conv_w.shape[-1]
    d_kv = n_kv * head_dim
    group = d_q // d_kv
    mp, ms = bp * tp, bs * ts
    m = mp + ms
    alpha = (2.0 * depth) ** 0.25
    tm = _largest_divisor(m, MAX_ROW_TILE, BF16_ROWS)
    ns = _largest_divisor(bs, SAMPLE_SEQS_PER_STEP, SUBLANES // ts)
    assert tm >= ms and SUBLANES % ts == 0 and mp % (ns * ts) == 0 and tp % WINDOW == 0
    assert ts >= CONV_W - 1 and tp >= WINDOW
    tf = 512
    tc = 512
    tm_mix = MIX_ROW_TILE
    assert mp % tm_mix == 0 and ms <= tm_mix and tm_mix % (WINDOW * (d_conv // tc)) == 0
    conv_col0 = d_q + 2 * d_kv
    ga_col0 = conv_col0 + 3 * d_conv
    gc_col0 = ga_col0 + d

    ln_g = ln_g.reshape(depth * 3, 1, d)
    ln_b = ln_b.reshape(depth * 3, 1, d)
    sink_tab = jnp.broadcast_to(sinks[:, :, None, None], sinks.shape + (1, LANES))

    t_prompt = jnp.tile(jnp.arange(tp, dtype=jnp.int32), bp)
    t_sample = jnp.tile(jnp.arange(ts, dtype=jnp.int32), bs)
    cos_tab, sin_tab = _rope_tables(jnp.concatenate([t_prompt, PAST_LEN + t_sample]), head_dim)
    t_idx = jnp.concatenate([t_prompt, t_sample]).reshape(m, 1)
    k_cache = cache_k_win.reshape(depth, bs, WINDOW, d_kv)
    v_cache = cache_v_win.reshape(depth, bs, WINDOW, d_kv)

    f1_gu, f1_down = ffn1_gu[0].astype(BF16), ffn1_down[0].astype(BF16)
    x = x_prompt.reshape(mp, d)
    x_tail = x_sample.reshape(ms, d)
    ks_p, vs_p, cs_p, ks_s, vs_s, cs_s = [], [], [], [], [], []
    for l in range(depth):
        jobs = [(w, l) for w in (w_in, w_branch_attn, w_branch_conv, w_out, ffn2_gu, ffn2_down)]
        x1, x1b, w_in_b, w_a_b, w_c_b, w_o_b, f2_gu, f2_down = _ffn_ln(
            x, f1_gu, f1_down, ln_g, ln_b, 3 * l, alpha, tm=tm, tf=tf, emit_bf16=True, x_tail=x_tail,
            cast_jobs=jobs)
        q, k, v = _qkv_proj(x1b, w_in_b, cos_tab, sin_tab, tm=tm, d_q=d_q, d_kv=d_kv, head_dim=head_dim)
        st = state_conv[l]
        zeros = jnp.zeros((bs, ts, d_conv), F32)
        fill1 = zeros.at[:, 0].set(st[:, 1]).reshape(ms, d_conv)
        fill2 = zeros.at[:, 0].set(st[:, 0]).at[:, 1].set(st[:, 1]).reshape(ms, d_conv)
        fill1 = jnp.pad(fill1, ((0, tm_mix - ms), (0, 0)))
        fill2 = jnp.pad(fill2, ((0, tm_mix - ms), (0, 0)))
        yconv, u, attn = _conv_attn(x1b, w_in_b, conv_w, t_idx, fill1, fill2, q, k, v, sink_tab, l,
                                    tm=tm_mix, tc=tc, col0=conv_col0, d_conv=d_conv, seq=tp, n_kv=n_kv,
                                    group=group, head_dim=head_dim)
        attn = _attn_sample(attn, q, k, v, k_cache, v_cache, sink_tab, l, row0=mp, seq=ts,
                            seqs_per_step=ns, n_kv=n_kv, group=group, head_dim=head_dim)
        x2 = _mix_ln(x1, x1b, attn, yconv, w_in_b, w_a_b, w_c_b, w_o_b, ln_g, ln_b, 3 * l + 1, alpha,
                     tm=tm, tc=tc, ga_col0=ga_col0, gc_col0=gc_col0)
        if l + 1 < depth:
            x, f1_gu, f1_down = _ffn_ln(x2, f2_gu, f2_down, ln_g, ln_b, 3 * l + 2, alpha, tm=tm, tf=tf,
                                        cast_jobs=[(ffn1_gu, l + 1), (ffn1_down, l + 1)])
            x_tail = None
        else:
            y_p, y_s = _ffn_ln(x2, f2_gu, f2_down, ln_g, ln_b, 3 * l + 2, alpha, tm=tm, tf=tf,
                               split_rows=mp)

        ks_p.append(_seq_tails(k, bp, tp, WINDOW).reshape(bp, WINDOW, n_kv, head_dim))
        vs_p.append(_seq_tails(v, bp, tp, WINDOW).reshape(bp, WINDOW, n_kv, head_dim))
        cs_p.append(_seq_tails(u, bp, tp, CONV_W - 1))
        k_new = k[mp:].reshape(bs, ts, n_kv, head_dim)
        v_new = v[mp:].reshape(bs, ts, n_kv, head_dim)
        ks_s.append(jnp.concatenate([cache_k_win[l][:, ts:], k_new], axis=1))
        vs_s.append(jnp.concatenate([cache_v_win[l][:, ts:], v_new], axis=1))
        cs_s.append(u[mp:].reshape(bs, ts, d_conv)[:, -(CONV_W - 1):])

    return (y_p.reshape(bp, tp, d), y_s.reshape(bs, ts, d), jnp.stack(ks_p), jnp.stack(vs_p),
            jnp.stack(cs_p), jnp.stack(ks_s), jnp.stack(vs_s), jnp.stack(cs_s))
```

```python
import functools

import jax
import jax.numpy as jnp
from jax import lax
from jax.experimental import pallas as pl
from jax.experimental.pallas import tpu as pltpu

F32 = jnp.float32
BF16 = jnp.bfloat16

PAST_LEN = 16384
WINDOW = 128
ROPE_THETA = 10000.0
LN_EPS = 1e-5
CONV_W = 3

LANES = 128
SUBLANES = 8
BF16_ROWS = 16
VMEM_LIMIT_BYTES = 62 * 1024 * 1024

KEY_SLOTS = 2 * WINDOW
MAX_ROW_TILE = 640
LN_ROW_PARTS = 2
MIX_ROW_TILE = 512
SAMPLE_SEQS_PER_STEP = 8


def _params(n_axes):
    return pltpu.CompilerParams(
        dimension_semantics=("arbitrary",) * n_axes,
        vmem_limit_bytes=VMEM_LIMIT_BYTES,
    )


def _layer_norm(y, g, b):
    mu = jnp.mean(y, axis=-1, keepdims=True)
    d = y - mu
    var = jnp.mean(d * d, axis=-1, keepdims=True)
    return d * lax.rsqrt(var + LN_EPS) * g + b


def _round_up(n, k):
    return -(-n // k) * k


def _ffn_ln_kernel(alpha, has_tail, emit_bf16, split, n_jobs, *refs):
    refs = list(refs)
    x_ref = refs.pop(0)
    xt_ref = refs.pop(0) if has_tail else None
    wg_ref, wu_ref, wd_ref, g_ref, b_ref = refs[:5]
    del refs[:5]
    job_src = refs[:n_jobs]
    del refs[:n_jobs]
    o_ref = refs.pop(0)
    ob_ref = refs.pop(0) if emit_bf16 else None
    ot_ref = refs.pop(0) if split else None
    job_dst = refs[:n_jobs]
    del refs[:n_jobs]
    xb_ref, acc_ref = refs
    i = pl.program_id(0)
    j = pl.program_id(1)
    last_tile = i == pl.num_programs(0) - 1
    tm = x_ref.shape[0]

    def load_x():
        x = x_ref[...]
        if xt_ref is None:
            return x
        merged = jnp.concatenate([x[:tm - xt_ref.shape[0]], xt_ref[...]], axis=0)
        return jnp.where(last_tile, merged, x)

    @pl.when(j == 0)
    def _():
        xb_ref[...] = load_x().astype(BF16)
        acc_ref[...] = jnp.zeros_like(acc_ref)

    last_step = j == pl.num_programs(1) - 1

    def hidden(rows):
        xb = xb_ref[rows, :]
        g = jnp.dot(xb, wg_ref[...], preferred_element_type=F32)
        u = jnp.dot(xb, wu_ref[...], preferred_element_type=F32)
        return (jax.nn.silu(g) * u).astype(BF16)

    def run_jobs():
        for src, dst in zip(job_src, job_dst):
            dst[...] = src[...].astype(BF16)

    @pl.when(jnp.logical_not(last_step))
    def _():
        h = hidden(slice(None))
        run_jobs()
        acc_ref[...] += jnp.dot(h, wd_ref[...], preferred_element_type=F32)

    @pl.when(last_step)
    def _():
        x = load_x()
        rows_per = tm // LN_ROW_PARTS
        n_tail = ot_ref.shape[0] if split else 0
        for r in range(LN_ROW_PARTS):
            rows = slice(r * rows_per, (r + 1) * rows_per)
            h = hidden(rows)
            if r == 0:
                run_jobs()
            acc = acc_ref[rows, :] + jnp.dot(h, wd_ref[...], preferred_element_type=F32)
            y = alpha * x[rows, :] + 0.5 * acc
            out = _layer_norm(y, g_ref[...], b_ref[...])
            o_ref[rows, :] = out
            if emit_bf16:
                ob_ref[rows, :] = out.astype(BF16)
            lo = max(tm - n_tail, rows.start)
            if split and lo < rows.stop:
                @pl.when(last_tile)
                def _():
                    ot_ref[lo - (tm - n_tail):rows.stop - (tm - n_tail), :] = out[lo - rows.start:, :]


def _ffn_ln(x, w_gu, w_down, ln_g, ln_b, ln_idx, alpha, *, tm, tf, emit_bf16=False, x_tail=None,
            split_rows=None, cast_jobs=()):
    d = x.shape[1]
    m = x.shape[0] + (0 if x_tail is None else x_tail.shape[0])
    f = w_down.shape[0]
    nj = f // tf
    n_tiles = m // tm
    row = lambda i, j: (i, 0)
    fixed = lambda i, j: (0, 0)
    in_specs = [pl.BlockSpec((tm, d), row)]
    args = [x]
    if x_tail is not None:
        in_specs.append(pl.BlockSpec(x_tail.shape, fixed))
        args.append(x_tail)
    in_specs += [
        pl.BlockSpec((d, tf), lambda i, j: (0, j)),
        pl.BlockSpec((d, tf), lambda i, j: (0, nj + j)),
        pl.BlockSpec((tf, d), lambda i, j: (j, 0)),
        pl.BlockSpec((None, 1, d), lambda i, j: (ln_idx, 0, 0)),
        pl.BlockSpec((None, 1, d), lambda i, j: (ln_idx, 0, 0)),
    ]
    args += [w_gu, w_gu, w_down, ln_g, ln_b]
    m_head = m if split_rows is None else split_rows
    out_shape = [jax.ShapeDtypeStruct((m_head, d), F32)]
    out_specs = [pl.BlockSpec((tm, d), row)]
    if emit_bf16:
        out_shape.append(jax.ShapeDtypeStruct((m, d), BF16))
        out_specs.append(pl.BlockSpec((tm, d), row))
    if split_rows is not None:
        out_shape.append(jax.ShapeDtypeStruct((m - split_rows, d), F32))
        out_specs.append(pl.BlockSpec((m - split_rows, d), fixed))
    n_steps = n_tiles * nj
    for w, layer in cast_jobs:
        _, r, c = w.shape
        rb = _round_up(-(-r // n_steps), BF16_ROWS)
        nb = -(-r // rb)
        block = lambda i, j, nb=nb: (jnp.minimum(i * nj + j, nb - 1), 0)
        in_specs.append(pl.BlockSpec((None, rb, c), lambda i, j, nb=nb, layer=layer:
                                     (layer, jnp.minimum(i * nj + j, nb - 1), 0)))
        args.append(w)
        out_shape.append(jax.ShapeDtypeStruct((r, c), BF16))
        out_specs.append(pl.BlockSpec((rb, c), block))
    return pl.pallas_call(
        functools.partial(_ffn_ln_kernel, alpha, x_tail is not None, emit_bf16, split_rows is not None,
                          len(cast_jobs)),
        grid=(n_tiles, nj),
        in_specs=in_specs,
        out_specs=out_specs,
        out_shape=out_shape,
        scratch_shapes=[pltpu.VMEM((tm, d), BF16), pltpu.VMEM((tm, d), F32)],
        compiler_params=_params(2),
        name="ffn_ln",
    )(*args)


def _rope_cols(x, cos, sin_signed, first_half, half):
    fwd = pltpu.roll(x, x.shape[1] - half, 1)
    bwd = pltpu.roll(x, half, 1)
    return x * cos + jnp.where(first_half, fwd, bwd) * sin_signed


def _qkv_kernel(d_q, d_kv, head_dim, xb_ref, w_ref, cos_ref, sin_ref, q_ref, k_ref, v_ref):
    xb = xb_ref[...]
    bounds = [0, d_q // 2, d_q, d_q + 2 * d_kv]
    parts = [jnp.dot(xb, w_ref[:, lo:hi], preferred_element_type=F32) for lo, hi in zip(bounds, bounds[1:])]
    q = jnp.concatenate(parts[:2], axis=1)
    kv = parts[2]
    cos = cos_ref[...]
    sin = sin_ref[...]
    half = head_dim // 2
    lane = lax.broadcasted_iota(jnp.int32, cos.shape, 1)
    first_half = (lane % head_dim) < half
    q_scale = head_dim ** -0.5
    for c in range(d_q // LANES):
        sl = slice(c * LANES, (c + 1) * LANES)
        q_ref[:, sl] = (_rope_cols(q[:, sl], cos, sin, first_half, half) * q_scale).astype(BF16)
    for c in range(d_kv // LANES):
        sl = slice(c * LANES, (c + 1) * LANES)
        k_ref[:, sl] = _rope_cols(kv[:, sl], cos, sin, first_half, half)
    v_ref[...] = kv[:, d_kv:]


def _qkv_proj(xb, w_in, cos_tab, sin_tab, *, tm, d_q, d_kv, head_dim):
    m, d = xb.shape
    n_cols = d_q + 2 * d_kv
    return pl.pallas_call(
        functools.partial(_qkv_kernel, d_q, d_kv, head_dim),
        grid=(m // tm,),
        in_specs=[
            pl.BlockSpec((tm, d), lambda i: (i, 0)),
            pl.BlockSpec((d, n_cols), lambda i: (0, 0)),
            pl.BlockSpec((tm, LANES), lambda i: (i, 0)),
            pl.BlockSpec((tm, LANES), lambda i: (i, 0)),
        ],
        out_specs=[
            pl.BlockSpec((tm, d_q), lambda i: (i, 0)),
            pl.BlockSpec((tm, d_kv), lambda i: (i, 0)),
            pl.BlockSpec((tm, d_kv), lambda i: (i, 0)),
        ],
        out_shape=[
            jax.ShapeDtypeStruct((m, d_q), BF16),
            jax.ShapeDtypeStruct((m, d_kv), F32),
            jax.ShapeDtypeStruct((m, d_kv), F32),
        ],
        compiler_params=_params(1),
        name="qkv_proj",
    )(xb, w_in, cos_tab, sin_tab)


def _lane_tile4(x128, want_high):
    lane = lax.broadcasted_iota(jnp.int32, x128.shape, 1)
    swapped = pltpu.roll(x128, LANES // 2, 1)
    low = lane < LANES // 2
    both = jnp.where(low, swapped, x128) if want_high else jnp.where(low, x128, swapped)
    return jnp.concatenate([both, both], axis=1)


def _band_attention(problems, sink, n_kv, group, head_dim, between=None):
    rows = problems[0][0].shape[0]
    gw = group * head_dim
    r_idx = lax.broadcasted_iota(jnp.int32, (rows, KEY_SLOTS), 0)
    s_idx = lax.broadcasted_iota(jnp.int32, (rows, KEY_SLOTS), 1)
    head_of_lane = lax.broadcasted_iota(jnp.int32, (rows, gw), 1) // head_dim
    head_keep = [(head_of_lane == g).astype(F32) for g in range(group)]

    scores, values = [], []
    for q, kband, vband, row0, kpos0 in problems:
        diff = r_idx - row0 + WINDOW - s_idx
        valid = (diff >= 0) & (diff <= WINDOW) & (s_idx + kpos0 >= 0)
        for kh in range(n_kv):
            col = (kh * head_dim) // LANES
            high = ((kh * head_dim) % LANES) != 0
            kk = _lane_tile4(kband[:, col * LANES:(col + 1) * LANES], high).astype(BF16)
            values.append(_lane_tile4(vband[:, col * LANES:(col + 1) * LANES], high).astype(BF16))
            qg = q[:, kh * gw:(kh + 1) * gw]
            qs = jnp.concatenate([qg * head_keep[g] for g in range(group)], axis=0).astype(BF16)
            s = lax.dot_general(qs, kk, (((1,), (1,)), ((), ())), preferred_element_type=F32)
            scores.append(jnp.where(valid[None], s.reshape(group, rows, KEY_SLOTS), -jnp.inf))
    s = jnp.concatenate(scores, axis=0)
    sink_all = jnp.concatenate([sink] * len(problems), axis=0)
    mx = jnp.maximum(jnp.max(s, axis=-1, keepdims=True), sink_all)
    p = jnp.exp(s - mx)
    den = jnp.sum(p, axis=-1, keepdims=True) + jnp.exp(sink_all - mx)
    p = p * (1.0 / den)

    if between is not None:
        between()

    outs = []
    for i in range(len(problems)):
        slabs = []
        for kh in range(n_kv):
            c = i * n_kv + kh
            pg = p[c * group:(c + 1) * group].reshape(group * rows, KEY_SLOTS).astype(BF16)
            o = jnp.dot(pg, values[c], preferred_element_type=F32).reshape(group, rows, gw)
            out = o[0]
            for g in range(1, group):
                out = jnp.where(head_of_lane == g, o[g], out)
            slabs.append(out)
        outs.append(jnp.concatenate(slabs, axis=1))
    return outs


def _sink_column(sink_ref):
    return sink_ref[...][:, :, 0:1]


def _attn_sample_kernel(seq, n_kv, group, head_dim, sink_ref, q_ref, kc_ref, kn_ref, vc_ref, vn_ref,
                        _, o_ref):
    n_seq = kc_ref.shape[0]
    d_kv = kn_ref.shape[1]
    per_block = SUBLANES // seq
    pad = jnp.zeros((KEY_SLOTS - WINDOW - SUBLANES, d_kv), F32)
    qf = q_ref[...].astype(F32)
    problems = []
    for i in range(n_seq):
        blk, row0 = i // per_block, (i % per_block) * seq
        rows = slice(blk * SUBLANES, (blk + 1) * SUBLANES)
        k8, v8 = kn_ref[rows, :], vn_ref[rows, :]
        if row0:
            k8 = pltpu.roll(k8, SUBLANES - row0, 0)
            v8 = pltpu.roll(v8, SUBLANES - row0, 0)
        kband = jnp.concatenate([kc_ref[i], k8, pad], axis=0)
        vband = jnp.concatenate([vc_ref[i], v8, pad], axis=0)
        problems.append((qf[rows, :], kband, vband, row0, PAST_LEN - WINDOW))
    outs = _band_attention(problems, _sink_column(sink_ref), n_kv, group, head_dim)
    r_idx = lax.broadcasted_iota(jnp.int32, outs[0].shape, 0)
    blocks = []
    for blk in range(n_seq // per_block):
        out = outs[blk * per_block]
        for s in range(1, per_block):
            out = jnp.where(r_idx >= s * seq, outs[blk * per_block + s], out)
        blocks.append(out)
    o_ref[...] = jnp.concatenate(blocks, axis=0).astype(BF16)


def _attn_sample(attn, q, k, v, k_cache, v_cache, sink_tab, layer, *, row0, seq, seqs_per_step, n_kv,
                 group, head_dim):
    d_q = q.shape[1]
    d_kv = k.shape[1]
    nbatch = k_cache.shape[1]
    ns = seqs_per_step
    rows = ns * seq
    blk0 = row0 // rows
    new = lambda b: (blk0 + b, 0)
    cache = lambda b: (layer, b, 0, 0)
    return pl.pallas_call(
        functools.partial(_attn_sample_kernel, seq, n_kv, group, head_dim),
        grid=(nbatch // ns,),
        in_specs=[
            pl.BlockSpec((None, n_kv * group, 1, LANES), lambda b: (layer, 0, 0, 0)),
            pl.BlockSpec((rows, d_q), new),
            pl.BlockSpec((None, ns, WINDOW, d_kv), cache),
            pl.BlockSpec((rows, d_kv), new),
            pl.BlockSpec((None, ns, WINDOW, d_kv), cache),
            pl.BlockSpec((rows, d_kv), new),
            pl.BlockSpec(memory_space=pl.ANY),
        ],
        out_specs=pl.BlockSpec((rows, d_q), new),
        out_shape=jax.ShapeDtypeStruct(attn.shape, attn.dtype),
        input_output_aliases={6: 0},
        compiler_params=_params(1),
        name="attn_sample",
    )(sink_tab, q, k_cache, k, v_cache, v, attn)


def _conv_attn_kernel(nc, blocks_per_seq, n_kv, group, head_dim,
                      xb_ref, wb_ref, wc_ref, wh_ref, cw_ref, t_ref, f1_ref, f2_ref,
                      sink_ref, q_ref, kp_ref, kc_ref, vp_ref, vc_ref,
                      y_ref, u_ref, o_ref, ubuf):
    tm = xb_ref.shape[0]
    i = pl.program_id(0)
    c = pl.program_id(1)

    @pl.when(i == 0)
    def _():
        ubuf[c] = jnp.zeros(ubuf.shape[1:], F32)

    carry = ubuf[c]

    n_blocks = tm // WINDOW
    per_step = n_blocks // nc
    problems, starts = [], []
    for s in range(per_step):
        b = c * per_step + s
        r0 = pl.multiple_of(b * WINDOW, WINDOW)
        rp = pl.multiple_of(jnp.maximum(b - 1, 0) * WINDOW, WINDOW)
        n = lax.rem(i * n_blocks + b, blocks_per_seq)
        k_before = jnp.where(b == 0, kp_ref[...], kc_ref[pl.ds(rp, WINDOW), :])
        v_before = jnp.where(b == 0, vp_ref[...], vc_ref[pl.ds(rp, WINDOW), :])
        kband = jnp.concatenate([k_before, kc_ref[pl.ds(r0, WINDOW), :]], axis=0)
        vband = jnp.concatenate([v_before, vc_ref[pl.ds(r0, WINDOW), :]], axis=0)
        problems.append((q_ref[pl.ds(r0, WINDOW), :].astype(F32), kband, vband, 0, (n - 1) * WINDOW))
        starts.append(r0)
    proj = []

    def conv_input_matmuls():
        xb = xb_ref[...]
        for w_ref in (wc_ref, wh_ref):
            proj.append(jnp.dot(xb, w_ref[...], preferred_element_type=F32))

    outs = _band_attention(problems, _sink_column(sink_ref), n_kv, group, head_dim,
                           between=conv_input_matmuls)
    cc, ch = proj
    cb = jnp.dot(xb_ref[...], wb_ref[...], preferred_element_type=F32)
    u = cc * ch
    row8 = lax.broadcasted_iota(jnp.int32, carry.shape, 0)

    def shifted(k):
        r = pltpu.roll(u, k, 0)
        head = jnp.where(row8 < k, pltpu.roll(carry, k, 0), r[0:SUBLANES])
        return jnp.concatenate([head, r[SUBLANES:]], axis=0)

    t = t_ref[...]
    last = i == pl.num_programs(0) - 1
    u_m1 = jnp.where(t >= 1, shifted(1), jnp.where(last, f1_ref[...], 0.0))
    u_m2 = jnp.where(t >= 2, shifted(2), jnp.where(last, f2_ref[...], 0.0))
    cw = cw_ref[...]
    conv = cw[0:1, :] * u_m2 + cw[1:2, :] * u_m1 + cw[2:3, :] * u

    for r0, out in zip(starts, outs):
        o_ref[pl.ds(r0, WINDOW), :] = out.astype(BF16)
    y_ref[...] = (cb * conv).astype(BF16)
    u_ref[...] = u
    ubuf[c] = u[tm - SUBLANES:, :]


def _conv_attn(xb, w_in, conv_w, t_idx, fill1, fill2, q, k, v, sink_tab, layer, *, tm, tc, col0, d_conv,
               seq, n_kv, group, head_dim):
    m, d = xb.shape
    d_q = q.shape[1]
    d_kv = k.shape[1]
    nc = d_conv // tc
    off = col0 // tc
    n_blocks = tm // WINDOW
    w_spec = lambda kk: pl.BlockSpec((d, tc), lambda i, c: (0, off + kk * nc + c))
    tile = lambda i, c: (i, 0)
    before = lambda i, c: (jnp.maximum(i * n_blocks - 1, 0), 0)
    return pl.pallas_call(
        functools.partial(_conv_attn_kernel, nc, seq // WINDOW, n_kv, group, head_dim),
        grid=(pl.cdiv(m, tm), nc),
        in_specs=[
            pl.BlockSpec((tm, d), tile),
            w_spec(0), w_spec(1), w_spec(2),
            pl.BlockSpec((None, CONV_W, tc), lambda i, c: (layer, 0, c)),
            pl.BlockSpec((tm, 1), tile),
            pl.BlockSpec((tm, tc), lambda i, c: (0, c)),
            pl.BlockSpec((tm, tc), lambda i, c: (0, c)),
            pl.BlockSpec((None, n_kv * group, 1, LANES), lambda i, c: (layer, 0, 0, 0)),
            pl.BlockSpec((tm, d_q), tile),
            pl.BlockSpec((WINDOW, d_kv), before),
            pl.BlockSpec((tm, d_kv), tile),
            pl.BlockSpec((WINDOW, d_kv), before),
            pl.BlockSpec((tm, d_kv), tile),
        ],
        out_specs=[
            pl.BlockSpec((tm, tc), lambda i, c: (i, c)),
            pl.BlockSpec((tm, tc), lambda i, c: (i, c)),
            pl.BlockSpec((tm, d_q), tile),
        ],
        out_shape=[
            jax.ShapeDtypeStruct((m, d_conv), BF16),
            jax.ShapeDtypeStruct((m, d_conv), F32),
            jax.ShapeDtypeStruct((m, d_q), BF16),
        ],
        scratch_shapes=[pltpu.VMEM((nc, SUBLANES, tc), F32)],
        compiler_params=_params(2),
        name="conv_attn",
    )(xb, w_in, w_in, w_in, conv_w, t_idx, fill1, fill2, sink_tab, q, k, k, v, v)


def _mix_ln_kernel(alpha, x_ref, xb_ref, a_ref, c_ref, wga_ref, wgc_ref, wa_ref, wc_ref, wo_ref, g_ref,
                   b_ref, o_ref, acc_ref):
    j = pl.program_id(1)

    @pl.when(j == 0)
    def _():
        acc_ref[...] = jnp.zeros_like(acc_ref)

    last_step = j == pl.num_programs(1) - 1
    tm = x_ref.shape[0]

    def merged_branches(rows):
        xb = xb_ref[rows, :]
        ga = jnp.dot(xb, wga_ref[...], preferred_element_type=F32)
        gc = jnp.dot(xb, wgc_ref[...], preferred_element_type=F32)
        pa = jnp.dot(a_ref[rows, :], wa_ref[...], preferred_element_type=F32)
        pc = jnp.dot(c_ref[rows, :], wc_ref[...], preferred_element_type=F32)
        return (jax.nn.sigmoid(ga) * pa + jax.nn.sigmoid(gc) * pc).astype(BF16)

    @pl.when(jnp.logical_not(last_step))
    def _():
        merged = merged_branches(slice(None))
        acc_ref[...] += jnp.dot(merged, wo_ref[...], preferred_element_type=F32)

    @pl.when(last_step)
    def _():
        rows_per = tm // LN_ROW_PARTS
        for r in range(LN_ROW_PARTS):
            rows = slice(r * rows_per, (r + 1) * rows_per)
            merged = merged_branches(rows)
            acc = acc_ref[rows, :] + jnp.dot(merged, wo_ref[...], preferred_element_type=F32)
            y = alpha * x_ref[rows, :] + acc
            o_ref[rows, :] = _layer_norm(y, g_ref[...], b_ref[...])


def _mix_ln(x, xb, attn, yconv, w_in, w_a, w_c, w_o, ln_g, ln_b, ln_idx, alpha, *, tm, tc, ga_col0,
            gc_col0):
    m, d = x.shape
    d_q = attn.shape[1]
    d_conv = yconv.shape[1]
    return pl.pallas_call(
        functools.partial(_mix_ln_kernel, alpha),
        grid=(m // tm, d // tc),
        in_specs=[
            pl.BlockSpec((tm, d), lambda i, j: (i, 0)),
            pl.BlockSpec((tm, d), lambda i, j: (i, 0)),
            pl.BlockSpec((tm, d_q), lambda i, j: (i, 0)),
            pl.BlockSpec((tm, d_conv), lambda i, j: (i, 0)),
            pl.BlockSpec((d, tc), lambda i, j: (0, ga_col0 // tc + j)),
            pl.BlockSpec((d, tc), lambda i, j: (0, gc_col0 // tc + j)),
            pl.BlockSpec((d_q, tc), lambda i, j: (0, j)),
            pl.BlockSpec((d_conv, tc), lambda i, j: (0, j)),
            pl.BlockSpec((tc, d), lambda i, j: (j, 0)),
            pl.BlockSpec((None, 1, d), lambda i, j: (ln_idx, 0, 0)),
            pl.BlockSpec((None, 1, d), lambda i, j: (ln_idx, 0, 0)),
        ],
        out_specs=pl.BlockSpec((tm, d), lambda i, j: (i, 0)),
        out_shape=jax.ShapeDtypeStruct((m, d), F32),
        scratch_shapes=[pltpu.VMEM((tm, d), F32)],
        compiler_params=_params(2),
        name="mix_ln",
    )(x, xb, attn, yconv, w_in, w_in, w_a, w_c, w_o, ln_g, ln_b)


def _rope_tables(pos, head_dim):
    inv_freq = ROPE_THETA ** (-jnp.arange(0, head_dim, 2, dtype=F32) / head_dim)
    ang = pos.astype(F32)[:, None] * inv_freq[None, :]
    cos = jnp.cos(ang)
    sin = jnp.sin(ang)
    reps = LANES // head_dim
    cos_tab = jnp.tile(jnp.concatenate([cos, cos], axis=1), (1, reps))
    sin_tab = jnp.tile(jnp.concatenate([-sin, sin], axis=1), (1, reps))
    return cos_tab, sin_tab


def _largest_divisor(n, cap, multiple_of=1):
    return max(k for k in range(multiple_of, cap + 1, multiple_of) if n % k == 0)


def _seq_tails(a, n_seq, seq, rows):
    return jnp.stack([a[(b + 1) * seq - rows:(b + 1) * seq] for b in range(n_seq)])


def kernel(x_prompt, x_sample, cache_k_win, cache_v_win, state_conv, ln_g, ln_b, w_in, sinks, conv_w,
           w_branch_attn, w_branch_conv, w_out, ffn1_gu, ffn1_down, ffn2_gu, ffn2_down):
    depth = w_in.shape[0]
    bp, tp, d = x_prompt.shape
    bs, ts, _ = x_sample.shape
    n_kv, head_dim = cache_k_win.shape[-2:]
    d_q = w_branch_attn.shape[1]
    d_conv = conv_w.shape[-1]
    d_kv = n_kv * head_dim
    group = d_q // d_kv
    mp, ms = bp * tp, bs * ts
    m = mp + ms
    alpha = (2.0 * depth) ** 0.25
    tm = _largest_divisor(m, MAX_ROW_TILE, BF16_ROWS)
    ns = _largest_divisor(bs, SAMPLE_SEQS_PER_STEP, SUBLANES // ts)
    assert tm >= ms and SUBLANES % ts == 0 and mp % (ns * ts) == 0 and tp % WINDOW == 0
    assert ts >= CONV_W - 1 and tp >= WINDOW
    tf = 512
    tc = 512
    tm_mix = MIX_ROW_TILE
    assert mp % tm_mix == 0 and ms <= tm_mix and tm_mix % (WINDOW * (d_conv // tc)) == 0
    conv_col0 = d_q + 2 * d_kv
    ga_col0 = conv_col0 + 3 * d_conv
    gc_col0 = ga_col0 + d

    ln_g = ln_g.reshape(depth * 3, 1, d)
    ln_b = ln_b.reshape(depth * 3, 1, d)
    sink_tab = jnp.broadcast_to(sinks[:, :, None, None], sinks.shape + (1, LANES))

    t_prompt = jnp.tile(jnp.arange(tp, dtype=jnp.int32), bp)
    t_sample = jnp.tile(jnp.arange(ts, dtype=jnp.int32), bs)
    cos_tab, sin_tab = _rope_tables(jnp.concatenate([t_prompt, PAST_LEN + t_sample]), head_dim)
    t_idx = jnp.concatenate([t_prompt, t_sample]).reshape(m, 1)
    k_cache = cache_k_win.reshape(depth, bs, WINDOW, d_kv)
    v_cache = cache_v_win.reshape(depth, bs, WINDOW, d_kv)

    f1_gu, f1_down = ffn1_gu[0].astype(BF16), ffn1_down[0].astype(BF16)
    x = x_prompt.reshape(mp, d)
    x_tail = x_sample.reshape(ms, d)
    ks_p, vs_p, cs_p, ks_s, vs_s, cs_s = [], [], [], [], [], []
    for l in range(depth):
        jobs = [(w, l) for w in (w_in, w_branch_attn, w_branch_conv, w_out, ffn2_gu, ffn2_down)]
        x1, x1b, w_in_b, w_a_b, w_c_b, w_o_b, f2_gu, f2_down = _ffn_ln(
            x, f1_gu, f1_down, ln_g, ln_b, 3 * l, alpha, tm=tm, tf=tf, emit_bf16=True, x_tail=x_tail,
            cast_jobs=jobs)
        q, k, v = _qkv_proj(x1b, w_in_b, cos_tab, sin_tab, tm=tm, d_q=d_q, d_kv=d_kv, head_dim=head_dim)
        st = state_conv[l]
        zeros = jnp.zeros((bs, ts, d_conv), F32)
        fill1 = zeros.at[:, 0].set(st[:, 1]).reshape(ms, d_conv)
        fill2 = zeros.at[:, 0].set(st[:, 0]).at[:, 1].set(st[:, 1]).reshape(ms, d_conv)
        fill1 = jnp.pad(fill1, ((0, tm_mix - ms), (0, 0)))
        fill2 = jnp.pad(fill2, ((0, tm_mix - ms), (0, 0)))
        yconv, u, attn = _conv_attn(x1b, w_in_b, conv_w, t_idx, fill1, fill2, q, k, v, sink_tab, l,
                                    tm=tm_mix, tc=tc, col0=conv_col0, d_conv=d_conv, seq=tp, n_kv=n_kv,
                                    group=group, head_dim=head_dim)
        attn = _attn_sample(attn, q, k, v, k_cache, v_cache, sink_tab, l, row0=mp, seq=ts,
                            seqs_per_step=ns, n_kv=n_kv, group=group, head_dim=head_dim)
        x2 = _mix_ln(x1, x1b, attn, yconv, w_in_b, w_a_b, w_c_b, w_o_b, ln_g, ln_b, 3 * l + 1, alpha,
                     tm=tm, tc=tc, ga_col0=ga_col0, gc_col0=gc_col0)
        if l + 1 < depth:
            x, f1_gu, f1_down = _ffn_ln(x2, f2_gu, f2_down, ln_g, ln_b, 3 * l + 2, alpha, tm=tm, tf=tf,
                                        cast_jobs=[(ffn1_gu, l + 1), (ffn1_down, l + 1)])
            x_tail = None
        else:
            y_p, y_s = _ffn_ln(x2, f2_gu, f2_down, ln_g, ln_b, 3 * l + 2, alpha, tm=tm, tf=tf,
                               split_rows=mp)

        ks_p.append(_seq_tails(k, bp, tp, WINDOW).reshape(bp, WINDOW, n_kv, head_dim))
        vs_p.append(_seq_tails(v, bp, tp, WINDOW).reshape(bp, WINDOW, n_kv, head_dim))
        cs_p.append(_seq_tails(u, bp, tp, CONV_W - 1))
        k_new = k[mp:].reshape(bs, ts, n_kv, head_dim)
        v_new = v[mp:].reshape(bs, ts, n_kv, head_dim)
        ks_s.append(jnp.concatenate([cache_k_win[l][:, ts:], k_new], axis=1))
        vs_s.append(jnp.concatenate([cache_v_win[l][:, ts:], v_new], axis=1))
        cs_s.append(u[mp:].reshape(bs, ts, d_conv)[:, -(CONV_W - 1):])

    return (y_p.reshape(bp, tp, d), y_s.reshape(bs, ts, d), jnp.stack(ks_p), jnp.stack(vs_p),
            jnp.stack(cs_p), jnp.stack(ks_s), jnp.stack(vs_s), jnp.stack(cs_s))
```

```python
import functools

import jax
import jax.numpy as jnp
from jax import lax
from jax.experimental import pallas as pl
from jax.experimental.pallas import tpu as pltpu

F32 = jnp.float32
BF16 = jnp.bfloat16

PAST_LEN = 16384
WINDOW = 128
ROPE_THETA = 10000.0
LN_EPS = 1e-5
CONV_W = 3

LANES = 128
SUBLANES = 8
BF16_ROWS = 16
VMEM_LIMIT_BYTES = 62 * 1024 * 1024

KEY_SLOTS = 2 * WINDOW
MAX_ROW_TILE = 640
LN_ROW_PARTS = 2
MIX_ROW_TILE = 512
SAMPLE_SEQS_PER_STEP = 8


def _params(n_axes):
    return pltpu.CompilerParams(
        dimension_semantics=("arbitrary",) * n_axes,
        vmem_limit_bytes=VMEM_LIMIT_BYTES,
    )


def _layer_norm(y, g, b):
    mu = jnp.mean(y, axis=-1, keepdims=True)
    d = y - mu
    var = jnp.mean(d * d, axis=-1, keepdims=True)
    return d * lax.rsqrt(var + LN_EPS) * g + b


def _round_up(n, k):
    return -(-n // k) * k


def _ffn_ln_kernel(alpha, has_tail, emit_bf16, split, n_jobs, *refs):
    refs = list(refs)
    x_ref = refs.pop(0)
    xt_ref = refs.pop(0) if has_tail else None
    wg_ref, wu_ref, wd_ref, g_ref, b_ref = refs[:5]
    del refs[:5]
    job_src = refs[:n_jobs]
    del refs[:n_jobs]
    o_ref = refs.pop(0)
    ob_ref = refs.pop(0) if emit_bf16 else None
    ot_ref = refs.pop(0) if split else None
    job_dst = refs[:n_jobs]
    del refs[:n_jobs]
    xb_ref, acc_ref = refs
    i = pl.program_id(0)
    j = pl.program_id(1)
    last_tile = i == pl.num_programs(0) - 1
    tm = x_ref.shape[0]

    def load_x():
        x = x_ref[...]
        if xt_ref is None:
            return x
        merged = jnp.concatenate([x[:tm - xt_ref.shape[0]], xt_ref[...]], axis=0)
        return jnp.where(last_tile, merged, x)

    @pl.when(j == 0)
    def _():
        xb_ref[...] = load_x().astype(BF16)
        acc_ref[...] = jnp.zeros_like(acc_ref)

    last_step = j == pl.num_programs(1) - 1

    def hidden(rows):
        xb = xb_ref[rows, :]
        g = jnp.dot(xb, wg_ref[...], preferred_element_type=F32)
        u = jnp.dot(xb, wu_ref[...], preferred_element_type=F32)
        return (jax.nn.silu(g) * u).astype(BF16)

    def run_jobs():
        for src, dst in zip(job_src, job_dst):
            dst[...] = src[...].astype(BF16)

    @pl.when(jnp.logical_not(last_step))
    def _():
        h = hidden(slice(None))
        run_jobs()
        acc_ref[...] += jnp.dot(h, wd_ref[...], preferred_element_type=F32)

    @pl.when(last_step)
    def _():
        x = load_x()
        rows_per = tm // LN_ROW_PARTS
        n_tail = ot_ref.shape[0] if split else 0
        for r in range(LN_ROW_PARTS):
            rows = slice(r * rows_per, (r + 1) * rows_per)
            h = hidden(rows)
            if r == 0:
                run_jobs()
            acc = acc_ref[rows, :] + jnp.dot(h, wd_ref[...], preferred_element_type=F32)
            y = alpha * x[rows, :] + 0.5 * acc
            out = _layer_norm(y, g_ref[...], b_ref[...])
            o_ref[rows, :] = out
            if emit_bf16:
                ob_ref[rows, :] = out.astype(BF16)
            lo = max(tm - n_tail, rows.start)
            if split and lo < rows.stop:
                @pl.when(last_tile)
                def _():
                    ot_ref[lo - (tm - n_tail):rows.stop - (tm - n_tail), :] = out[lo - rows.start:, :]


def _ffn_ln(x, w_gu, w_down, ln_g, ln_b, ln_idx, alpha, *, tm, tf, emit_bf16=False, x_tail=None,
            split_rows=None, cast_jobs=()):
    d = x.shape[1]
    m = x.shape[0] + (0 if x_tail is None else x_tail.shape[0])
    f = w_down.shape[0]
    nj = f // tf
    n_tiles = m // tm
    row = lambda i, j: (i, 0)
    fixed = lambda i, j: (0, 0)
    in_specs = [pl.BlockSpec((tm, d), row)]
    args = [x]
    if x_tail is not None:
        in_specs.append(pl.BlockSpec(x_tail.shape, fixed))
        args.append(x_tail)
    in_specs += [
        pl.BlockSpec((None, d, tf), lambda i, j: (j, 0, 0)),
        pl.BlockSpec((None, d, tf), lambda i, j: (nj + j, 0, 0)),
        pl.BlockSpec((tf, d), lambda i, j: (j, 0)),
        pl.BlockSpec((None, 1, d), lambda i, j: (ln_idx, 0, 0)),
        pl.BlockSpec((None, 1, d), lambda i, j: (ln_idx, 0, 0)),
    ]
    w_gu = w_gu.reshape(d, 2 * nj, tf).transpose(1, 0, 2)
    args += [w_gu, w_gu, w_down, ln_g, ln_b]
    m_head = m if split_rows is None else split_rows
    out_shape = [jax.ShapeDtypeStruct((m_head, d), F32)]
    out_specs = [pl.BlockSpec((tm, d), row)]
    if emit_bf16:
        out_shape.append(jax.ShapeDtypeStruct((m, d), BF16))
        out_specs.append(pl.BlockSpec((tm, d), row))
    if split_rows is not None:
        out_shape.append(jax.ShapeDtypeStruct((m - split_rows, d), F32))
        out_specs.append(pl.BlockSpec((m - split_rows, d), fixed))
    n_steps = n_tiles * nj
    for w, layer in cast_jobs:
        _, r, c = w.shape
        rb = _round_up(-(-r // n_steps), BF16_ROWS)
        nb = -(-r // rb)
        block = lambda i, j, nb=nb: (jnp.minimum(i * nj + j, nb - 1), 0)
        in_specs.append(pl.BlockSpec((None, rb, c), lambda i, j, nb=nb, layer=layer:
                                     (layer, jnp.minimum(i * nj + j, nb - 1), 0)))
        args.append(w)
        out_shape.append(jax.ShapeDtypeStruct((r, c), BF16))
        out_specs.append(pl.BlockSpec((rb, c), block))
    return pl.pallas_call(
        functools.partial(_ffn_ln_kernel, alpha, x_tail is not None, emit_bf16, split_rows is not None,
                          len(cast_jobs)),
        grid=(n_tiles, nj),
        in_specs=in_specs,
        out_specs=out_specs,
        out_shape=out_shape,
        scratch_shapes=[pltpu.VMEM((tm, d), BF16), pltpu.VMEM((tm, d), F32)],
        compiler_params=_params(2),
        name="ffn_ln",
    )(*args)


def _rope_cols(x, cos, sin_signed, first_half, half):
    fwd = pltpu.roll(x, x.shape[1] - half, 1)
    bwd = pltpu.roll(x, half, 1)
    return x * cos + jnp.where(first_half, fwd, bwd) * sin_signed


def _qkv_kernel(d_q, d_kv, head_dim, xb_ref, w_ref, cos_ref, sin_ref, q_ref, k_ref, v_ref):
    xb = xb_ref[...]
    bounds = [0, d_q // 2, d_q, d_q + 2 * d_kv]
    parts = [jnp.dot(xb, w_ref[:, lo:hi], preferred_element_type=F32) for lo, hi in zip(bounds, bounds[1:])]
    q = jnp.concatenate(parts[:2], axis=1)
    kv = parts[2]
    cos = cos_ref[...]
    sin = sin_ref[...]
    half = head_dim // 2
    lane = lax.broadcasted_iota(jnp.int32, cos.shape, 1)
    first_half = (lane % head_dim) < half
    q_scale = head_dim ** -0.5
    for c in range(d_q // LANES):
        sl = slice(c * LANES, (c + 1) * LANES)
        q_ref[:, sl] = (_rope_cols(q[:, sl], cos, sin, first_half, half) * q_scale).astype(BF16)
    for c in range(d_kv // LANES):
        sl = slice(c * LANES, (c + 1) * LANES)
        k_ref[:, sl] = _rope_cols(kv[:, sl], cos, sin, first_half, half)
    v_ref[...] = kv[:, d_kv:]


def _qkv_proj(xb, w_in, cos_tab, sin_tab, *, tm, d_q, d_kv, head_dim):
    m, d = xb.shape
    n_cols = d_q + 2 * d_kv
    return pl.pallas_call(
        functools.partial(_qkv_kernel, d_q, d_kv, head_dim),
        grid=(m // tm,),
        in_specs=[
            pl.BlockSpec((tm, d), lambda i: (i, 0)),
            pl.BlockSpec((d, n_cols), lambda i: (0, 0)),
            pl.BlockSpec((tm, LANES), lambda i: (i, 0)),
            pl.BlockSpec((tm, LANES), lambda i: (i, 0)),
        ],
        out_specs=[
            pl.BlockSpec((tm, d_q), lambda i: (i, 0)),
            pl.BlockSpec((tm, d_kv), lambda i: (i, 0)),
            pl.BlockSpec((tm, d_kv), lambda i: (i, 0)),
        ],
        out_shape=[
            jax.ShapeDtypeStruct((m, d_q), BF16),
            jax.ShapeDtypeStruct((m, d_kv), F32),
            jax.ShapeDtypeStruct((m, d_kv), F32),
        ],
        compiler_params=_params(1),
        name="qkv_proj",
    )(xb, w_in, cos_tab, sin_tab)


def _lane_tile4(x128, want_high):
    lane = lax.broadcasted_iota(jnp.int32, x128.shape, 1)
    swapped = pltpu.roll(x128, LANES // 2, 1)
    low = lane < LANES // 2
    both = jnp.where(low, swapped, x128) if want_high else jnp.where(low, x128, swapped)
    return jnp.concatenate([both, both], axis=1)


def _band_attention(problems, sink, n_kv, group, head_dim, between=None):
    rows = problems[0][0].shape[0]
    gw = group * head_dim
    r_idx = lax.broadcasted_iota(jnp.int32, (rows, KEY_SLOTS), 0)
    s_idx = lax.broadcasted_iota(jnp.int32, (rows, KEY_SLOTS), 1)
    head_of_lane = lax.broadcasted_iota(jnp.int32, (rows, gw), 1) // head_dim
    head_keep = [(head_of_lane == g).astype(F32) for g in range(group)]

    scores, values = [], []
    for q, kband, vband, row0, kpos0 in problems:
        diff = r_idx - row0 + WINDOW - s_idx
        valid = (diff >= 0) & (diff <= WINDOW) & (s_idx + kpos0 >= 0)
        for kh in range(n_kv):
            col = (kh * head_dim) // LANES
            high = ((kh * head_dim) % LANES) != 0
            kk = _lane_tile4(kband[:, col * LANES:(col + 1) * LANES], high).astype(BF16)
            values.append(_lane_tile4(vband[:, col * LANES:(col + 1) * LANES], high).astype(BF16))
            qg = q[:, kh * gw:(kh + 1) * gw]
            qs = jnp.concatenate([qg * head_keep[g] for g in range(group)], axis=0).astype(BF16)
            s = lax.dot_general(qs, kk, (((1,), (1,)), ((), ())), preferred_element_type=F32)
            scores.append(jnp.where(valid[None], s.reshape(group, rows, KEY_SLOTS), -jnp.inf))
    s = jnp.concatenate(scores, axis=0)
    sink_all = jnp.concatenate([sink] * len(problems), axis=0)
    mx = jnp.maximum(jnp.max(s, axis=-1, keepdims=True), sink_all)
    p = jnp.exp(s - mx)
    den = jnp.sum(p, axis=-1, keepdims=True) + jnp.exp(sink_all - mx)
    p = p * (1.0 / den)

    if between is not None:
        between()

    outs = []
    for i in range(len(problems)):
        slabs = []
        for kh in range(n_kv):
            c = i * n_kv + kh
            pg = p[c * group:(c + 1) * group].reshape(group * rows, KEY_SLOTS).astype(BF16)
            o = jnp.dot(pg, values[c], preferred_element_type=F32).reshape(group, rows, gw)
            out = o[0]
            for g in range(1, group):
                out = jnp.where(head_of_lane == g, o[g], out)
            slabs.append(out)
        outs.append(jnp.concatenate(slabs, axis=1))
    return outs


def _sink_column(sink_ref):
    return sink_ref[...][:, :, 0:1]


def _attn_sample_kernel(seq, n_kv, group, head_dim, sink_ref, q_ref, kc_ref, kn_ref, vc_ref, vn_ref,
                        _, o_ref):
    n_seq = kc_ref.shape[0]
    d_kv = kn_ref.shape[1]
    per_block = SUBLANES // seq
    pad = jnp.zeros((KEY_SLOTS - WINDOW - SUBLANES, d_kv), F32)
    qf = q_ref[...].astype(F32)
    problems = []
    for i in range(n_seq):
        blk, row0 = i // per_block, (i % per_block) * seq
        rows = slice(blk * SUBLANES, (blk + 1) * SUBLANES)
        k8, v8 = kn_ref[rows, :], vn_ref[rows, :]
        if row0:
            k8 = pltpu.roll(k8, SUBLANES - row0, 0)
            v8 = pltpu.roll(v8, SUBLANES - row0, 0)
        kband = jnp.concatenate([kc_ref[i], k8, pad], axis=0)
        vband = jnp.concatenate([vc_ref[i], v8, pad], axis=0)
        problems.append((qf[rows, :], kband, vband, row0, PAST_LEN - WINDOW))
    outs = _band_attention(problems, _sink_column(sink_ref), n_kv, group, head_dim)
    r_idx = lax.broadcasted_iota(jnp.int32, outs[0].shape, 0)
    blocks = []
    for blk in range(n_seq // per_block):
        out = outs[blk * per_block]
        for s in range(1, per_block):
            out = jnp.where(r_idx >= s * seq, outs[blk * per_block + s], out)
        blocks.append(out)
    o_ref[...] = jnp.concatenate(blocks, axis=0).astype(BF16)


def _attn_sample(attn, q, k, v, k_cache, v_cache, sink_tab, layer, *, row0, seq, seqs_per_step, n_kv,
                 group, head_dim):
    d_q = q.shape[1]
    d_kv = k.shape[1]
    nbatch = k_cache.shape[1]
    ns = seqs_per_step
    rows = ns * seq
    blk0 = row0 // rows
    new = lambda b: (blk0 + b, 0)
    cache = lambda b: (layer, b, 0, 0)
    return pl.pallas_call(
        functools.partial(_attn_sample_kernel, seq, n_kv, group, head_dim),
        grid=(nbatch // ns,),
        in_specs=[
            pl.BlockSpec((None, n_kv * group, 1, LANES), lambda b: (layer, 0, 0, 0)),
            pl.BlockSpec((rows, d_q), new),
            pl.BlockSpec((None, ns, WINDOW, d_kv), cache),
            pl.BlockSpec((rows, d_kv), new),
            pl.BlockSpec((None, ns, WINDOW, d_kv), cache),
            pl.BlockSpec((rows, d_kv), new),
            pl.BlockSpec(memory_space=pl.ANY),
        ],
        out_specs=pl.BlockSpec((rows, d_q), new),
        out_shape=jax.ShapeDtypeStruct(attn.shape, attn.dtype),
        input_output_aliases={6: 0},
        compiler_params=_params(1),
        name="attn_sample",
    )(sink_tab, q, k_cache, k, v_cache, v, attn)


def _conv_attn_kernel(nc, blocks_per_seq, n_kv, group, head_dim,
                      xb_ref, wb_ref, wc_ref, wh_ref, cw_ref, t_ref, f1_ref, f2_ref,
                      sink_ref, q_ref, kp_ref, kc_ref, vp_ref, vc_ref,
                      y_ref, u_ref, o_ref, ubuf):
    tm = xb_ref.shape[0]
    i = pl.program_id(0)
    c = pl.program_id(1)

    @pl.when(i == 0)
    def _():
        ubuf[c] = jnp.zeros(ubuf.shape[1:], F32)

    carry = ubuf[c]

    n_blocks = tm // WINDOW
    per_step = n_blocks // nc
    problems, starts = [], []
    for s in range(per_step):
        b = c * per_step + s
        r0 = pl.multiple_of(b * WINDOW, WINDOW)
        rp = pl.multiple_of(jnp.maximum(b - 1, 0) * WINDOW, WINDOW)
        n = lax.rem(i * n_blocks + b, blocks_per_seq)
        k_before = jnp.where(b == 0, kp_ref[...], kc_ref[pl.ds(rp, WINDOW), :])
        v_before = jnp.where(b == 0, vp_ref[...], vc_ref[pl.ds(rp, WINDOW), :])
        kband = jnp.concatenate([k_before, kc_ref[pl.ds(r0, WINDOW), :]], axis=0)
        vband = jnp.concatenate([v_before, vc_ref[pl.ds(r0, WINDOW), :]], axis=0)
        problems.append((q_ref[pl.ds(r0, WINDOW), :].astype(F32), kband, vband, 0, (n - 1) * WINDOW))
        starts.append(r0)
    proj = []

    def conv_input_matmuls():
        xb = xb_ref[...]
        for w_ref in (wc_ref, wh_ref):
            proj.append(jnp.dot(xb, w_ref[...], preferred_element_type=F32))

    outs = _band_attention(problems, _sink_column(sink_ref), n_kv, group, head_dim,
                           between=conv_input_matmuls)
    cc, ch = proj
    cb = jnp.dot(xb_ref[...], wb_ref[...], preferred_element_type=F32)
    u = cc * ch
    row8 = lax.broadcasted_iota(jnp.int32, carry.shape, 0)

    def shifted(k):
        r = pltpu.roll(u, k, 0)
        head = jnp.where(row8 < k, pltpu.roll(carry, k, 0), r[0:SUBLANES])
        return jnp.concatenate([head, r[SUBLANES:]], axis=0)

    t = t_ref[...]
    last = i == pl.num_programs(0) - 1
    u_m1 = jnp.where(t >= 1, shifted(1), jnp.where(last, f1_ref[...], 0.0))
    u_m2 = jnp.where(t >= 2, shifted(2), jnp.where(last, f2_ref[...], 0.0))
    cw = cw_ref[...]
    conv = cw[0:1, :] * u_m2 + cw[1:2, :] * u_m1 + cw[2:3, :] * u

    for r0, out in zip(starts, outs):
        o_ref[pl.ds(r0, WINDOW), :] = out.astype(BF16)
    y_ref[...] = (cb * conv).astype(BF16)
    u_ref[...] = u
    ubuf[c] = u[tm - SUBLANES:, :]


def _conv_attn(xb, w_in, conv_w, t_idx, fill1, fill2, q, k, v, sink_tab, layer, *, tm, tc, col0, d_conv,
               seq, n_kv, group, head_dim):
    m, d = xb.shape
    d_q = q.shape[1]
    d_kv = k.shape[1]
    nc = d_conv // tc
    off = col0 // tc
    n_blocks = tm // WINDOW
    w_spec = lambda kk: pl.BlockSpec((d, tc), lambda i, c: (0, off + kk * nc + c))
    tile = lambda i, c: (i, 0)
    before = lambda i, c: (jnp.maximum(i * n_blocks - 1, 0), 0)
    return pl.pallas_call(
        functools.partial(_conv_attn_kernel, nc, seq // WINDOW, n_kv, group, head_dim),
        grid=(pl.cdiv(m, tm), nc),
        in_specs=[
            pl.BlockSpec((tm, d), tile),
            w_spec(0), w_spec(1), w_spec(2),
            pl.BlockSpec((None, CONV_W, tc), lambda i, c: (layer, 0, c)),
            pl.BlockSpec((tm, 1), tile),
            pl.BlockSpec((tm, tc), lambda i, c: (0, c)),
            pl.BlockSpec((tm, tc), lambda i, c: (0, c)),
            pl.BlockSpec((None, n_kv * group, 1, LANES), lambda i, c: (layer, 0, 0, 0)),
            pl.BlockSpec((tm, d_q), tile),
            pl.BlockSpec((WINDOW, d_kv), before),
            pl.BlockSpec((tm, d_kv), tile),
            pl.BlockSpec((WINDOW, d_kv), before),
            pl.BlockSpec((tm, d_kv), tile),
        ],
        out_specs=[
            pl.BlockSpec((tm, tc), lambda i, c: (i, c)),
            pl.BlockSpec((tm, tc), lambda i, c: (i, c)),
            pl.BlockSpec((tm, d_q), tile),
        ],
        out_shape=[
            jax.ShapeDtypeStruct((m, d_conv), BF16),
            jax.ShapeDtypeStruct((m, d_conv), F32),
            jax.ShapeDtypeStruct((m, d_q), BF16),
        ],
        scratch_shapes=[pltpu.VMEM((nc, SUBLANES, tc), F32)],
        compiler_params=_params(2),
        name="conv_attn",
    )(xb, w_in, w_in, w_in, conv_w, t_idx, fill1, fill2, sink_tab, q, k, k, v, v)


def _mix_ln_kernel(alpha, x_ref, xb_ref, a_ref, c_ref, wga_ref, wgc_ref, wa_ref, wc_ref, wo_ref, g_ref,
                   b_ref, o_ref, acc_ref):
    j = pl.program_id(1)

    @pl.when(j == 0)
    def _():
        acc_ref[...] = jnp.zeros_like(acc_ref)

    last_step = j == pl.num_programs(1) - 1
    tm = x_ref.shape[0]

    def merged_branches(rows):
        xb = xb_ref[rows, :]
        ga = jnp.dot(xb, wga_ref[...], preferred_element_type=F32)
        gc = jnp.dot(xb, wgc_ref[...], preferred_element_type=F32)
        pa = jnp.dot(a_ref[rows, :], wa_ref[...], preferred_element_type=F32)
        pc = jnp.dot(c_ref[rows, :], wc_ref[...], preferred_element_type=F32)
        return (jax.nn.sigmoid(ga) * pa + jax.nn.sigmoid(gc) * pc).astype(BF16)

    @pl.when(jnp.logical_not(last_step))
    def _():
        merged = merged_branches(slice(None))
        acc_ref[...] += jnp.dot(merged, wo_ref[...], preferred_element_type=F32)

    @pl.when(last_step)
    def _():
        rows_per = tm // LN_ROW_PARTS
        for r in range(LN_ROW_PARTS):
            rows = slice(r * rows_per, (r + 1) * rows_per)
            merged = merged_branches(rows)
            acc = acc_ref[rows, :] + jnp.dot(merged, wo_ref[...], preferred_element_type=F32)
            y = alpha * x_ref[rows, :] + acc
            o_ref[rows, :] = _layer_norm(y, g_ref[...], b_ref[...])


def _mix_ln(x, xb, attn, yconv, w_in, w_a, w_c, w_o, ln_g, ln_b, ln_idx, alpha, *, tm, tc, ga_col0,
            gc_col0):
    m, d = x.shape
    d_q = attn.shape[1]
    d_conv = yconv.shape[1]
    return pl.pallas_call(
        functools.partial(_mix_ln_kernel, alpha),
        grid=(m // tm, d // tc),
        in_specs=[
            pl.BlockSpec((tm, d), lambda i, j: (i, 0)),
            pl.BlockSpec((tm, d), lambda i, j: (i, 0)),
            pl.BlockSpec((tm, d_q), lambda i, j: (i, 0)),
            pl.BlockSpec((tm, d_conv), lambda i, j: (i, 0)),
            pl.BlockSpec((d, tc), lambda i, j: (0, ga_col0 // tc + j)),
            pl.BlockSpec((d, tc), lambda i, j: (0, gc_col0 // tc + j)),
            pl.BlockSpec((d_q, tc), lambda i, j: (0, j)),
            pl.BlockSpec((d_conv, tc), lambda i, j: (0, j)),
            pl.BlockSpec((tc, d), lambda i, j: (j, 0)),
            pl.BlockSpec((None, 1, d), lambda i, j: (ln_idx, 0, 0)),
            pl.BlockSpec((None, 1, d), lambda i, j: (ln_idx, 0, 0)),
        ],
        out_specs=pl.BlockSpec((tm, d), lambda i, j: (i, 0)),
        out_shape=jax.ShapeDtypeStruct((m, d), F32),
        scratch_shapes=[pltpu.VMEM((tm, d), F32)],
        compiler_params=_params(2),
        name="mix_ln",
    )(x, xb, attn, yconv, w_in, w_in, w_a, w_c, w_o, ln_g, ln_b)


def _rope_tables(pos, head_dim):
    inv_freq = ROPE_THETA ** (-jnp.arange(0, head_dim, 2, dtype=F32) / head_dim)
    ang = pos.astype(F32)[:, None] * inv_freq[None, :]
    cos = jnp.cos(ang)
    sin = jnp.sin(ang)
    reps = LANES // head_dim
    cos_tab = jnp.tile(jnp.concatenate([cos, cos], axis=1), (1, reps))
    sin_tab = jnp.tile(jnp.concatenate([-sin, sin], axis=1), (1, reps))
    return cos_tab, sin_tab


def _largest_divisor(n, cap, multiple_of=1):
    return max(k for k in range(multiple_of, cap + 1, multiple_of) if n % k == 0)


def _seq_tails(a, n_seq, seq, rows):
    return jnp.stack([a[(b + 1) * seq - rows:(b + 1) * seq] for b in range(n_seq)])


def kernel(x_prompt, x_sample, cache_k_win, cache_v_win, state_conv, ln_g, ln_b, w_in, sinks, conv_w,
           w_branch_attn, w_branch_conv, w_out, ffn1_gu, ffn1_down, ffn2_gu, ffn2_down):
    depth = w_in.shape[0]
    bp, tp, d = x_prompt.shape
    bs, ts, _ = x_sample.shape
    n_kv, head_dim = cache_k_win.shape[-2:]
    d_q = w_branch_attn.shape[1]
    d_conv = conv_w.shape[-1]
    d_kv = n_kv * head_dim
    group = d_q // d_kv
    mp, ms = bp * tp, bs * ts
    m = mp + ms
    alpha = (2.0 * depth) ** 0.25
    tm = _largest_divisor(m, MAX_ROW_TILE, BF16_ROWS)
    ns = _largest_divisor(bs, SAMPLE_SEQS_PER_STEP, SUBLANES // ts)
    assert tm >= ms and SUBLANES % ts == 0 and mp % (ns * ts) == 0 and tp % WINDOW == 0
    assert ts >= CONV_W - 1 and tp >= WINDOW
    tf = 512
    tc = 512
    tm_mix = MIX_ROW_TILE
    assert mp % tm_mix == 0 and ms <= tm_mix and tm_mix % (WINDOW * (d_conv // tc)) == 0
    conv_col0 = d_q + 2 * d_kv
    ga_col0 = conv_col0 + 3 * d_conv
    gc_col0 = ga_col0 + d

    ln_g = ln_g.reshape(depth * 3, 1, d)
    ln_b = ln_b.reshape(depth * 3, 1, d)
    sink_tab = jnp.broadcast_to(sinks[:, :, None, None], sinks.shape + (1, LANES))

    t_prompt = jnp.tile(jnp.arange(tp, dtype=jnp.int32), bp)
    t_sample = jnp.tile(jnp.arange(ts, dtype=jnp.int32), bs)
    cos_tab, sin_tab = _rope_tables(jnp.concatenate([t_prompt, PAST_LEN + t_sample]), head_dim)
    t_idx = jnp.concatenate([t_prompt, t_sample]).reshape(m, 1)
    k_cache = cache_k_win.reshape(depth, bs, WINDOW, d_kv)
    v_cache = cache_v_win.reshape(depth, bs, WINDOW, d_kv)

    f1_gu, f1_down = ffn1_gu[0].astype(BF16), ffn1_down[0].astype(BF16)
    x = x_prompt.reshape(mp, d)
    x_tail = x_sample.reshape(ms, d)
    ks_p, vs_p, cs_p, ks_s, vs_s, cs_s = [], [], [], [], [], []
    for l in range(depth):
        jobs = [(w, l) for w in (w_in, w_branch_attn, w_branch_conv, w_out, ffn2_gu, ffn2_down)]
        x1, x1b, w_in_b, w_a_b, w_c_b, w_o_b, f2_gu, f2_down = _ffn_ln(
            x, f1_gu, f1_down, ln_g, ln_b, 3 * l, alpha, tm=tm, tf=tf, emit_bf16=True, x_tail=x_tail,
            cast_jobs=jobs)
        q, k, v = _qkv_proj(x1b, w_in_b, cos_tab, sin_tab, tm=tm, d_q=d_q, d_kv=d_kv, head_dim=head_dim)
        st = state_conv[l]
        zeros = jnp.zeros((bs, ts, d_conv), F32)
        fill1 = zeros.at[:, 0].set(st[:, 1]).reshape(ms, d_conv)
        fill2 = zeros.at[:, 0].set(st[:, 0]).at[:, 1].set(st[:, 1]).reshape(ms, d_conv)
        fill1 = jnp.pad(fill1, ((0, tm_mix - ms), (0, 0)))
        fill2 = jnp.pad(fill2, ((0, tm_mix - ms), (0, 0)))
        yconv, u, attn = _conv_attn(x1b, w_in_b, conv_w, t_idx, fill1, fill2, q, k, v, sink_tab, l,
                                    tm=tm_mix, tc=tc, col0=conv_col0, d_conv=d_conv, seq=tp, n_kv=n_kv,
                                    group=group, head_dim=head_dim)
        attn = _attn_sample(attn, q, k, v, k_cache, v_cache, sink_tab, l, row0=mp, seq=ts,
                            seqs_per_step=ns, n_kv=n_kv, group=group, head_dim=head_dim)
        x2 = _mix_ln(x1, x1b, attn, yconv, w_in_b, w_a_b, w_c_b, w_o_b, ln_g, ln_b, 3 * l + 1, alpha,
                     tm=tm, tc=tc, ga_col0=ga_col0, gc_col0=gc_col0)
        if l + 1 < depth:
            x, f1_gu, f1_down = _ffn_ln(x2, f2_gu, f2_down, ln_g, ln_b, 3 * l + 2, alpha, tm=tm, tf=tf,
                                        cast_jobs=[(ffn1_gu, l + 1), (ffn1_down, l + 1)])
            x_tail = None
        else:
            y_p, y_s = _ffn_ln(x2, f2_gu, f2_down, ln_g, ln_b, 3 * l + 2, alpha, tm=tm, tf=tf,
                               split_rows=mp)

        ks_p.append(_seq_tails(k, bp, tp, WINDOW).reshape(bp, WINDOW, n_kv, head_dim))
        vs_p.append(_seq_tails(v, bp, tp, WINDOW).reshape(bp, WINDOW, n_kv, head_dim))
        cs_p.append(_seq_tails(u, bp, tp, CONV_W - 1))
        k_new = k[mp:].reshape(bs, ts, n_kv, head_dim)
        v_new = v[mp:].reshape(bs, ts, n_kv, head_dim)
        ks_s.append(jnp.concatenate([cache_k_win[l][:, ts:], k_new], axis=1))
        vs_s.append(jnp.concatenate([cache_v_win[l][:, ts:], v_new], axis=1))
        cs_s.append(u[mp:].reshape(bs, ts, d_conv)[:, -(CONV_W - 1):])

    return (y_p.reshape(bp, tp, d), y_s.reshape(bs, ts, d), jnp.stack(ks_p), jnp.stack(vs_p),
            jnp.stack(cs_p), jnp.stack(ks_s), jnp.stack(vs_s), jnp.stack(cs_s))
```

```python
import functools

import jax
import jax.numpy as jnp
from jax import lax
from jax.experimental import pallas as pl
from jax.experimental.pallas import tpu as pltpu

F32 = jnp.float32
BF16 = jnp.bfloat16

PAST_LEN = 16384
WINDOW = 128
ROPE_THETA = 10000.0
LN_EPS = 1e-5
CONV_W = 3

LANES = 128
SUBLANES = 8
BF16_ROWS = 16
VMEM_LIMIT_BYTES = 62 * 1024 * 1024

KEY_SLOTS = 2 * WINDOW
MAX_ROW_TILE = 640
MIX_ROW_TILE = 512
SAMPLE_SEQS_PER_STEP = 8


def _params(n_axes):
    return pltpu.CompilerParams(
        dimension_semantics=("arbitrary",) * n_axes,
        vmem_limit_bytes=VMEM_LIMIT_BYTES,
    )


def _layer_norm(y, g, b):
    mu = jnp.mean(y, axis=-1, keepdims=True)
    d = y - mu
    var = jnp.mean(d * d, axis=-1, keepdims=True)
    return d * lax.rsqrt(var + LN_EPS) * g + b


def _round_up(n, k):
    return -(-n // k) * k


def _ffn_ln_kernel(alpha, has_tail, emit_bf16, split, n_jobs, *refs):
    refs = list(refs)
    x_ref = refs.pop(0)
    xt_ref = refs.pop(0) if has_tail else None
    wg_ref, wu_ref, wd_ref, g_ref, b_ref = refs[:5]
    del refs[:5]
    job_src = refs[:n_jobs]
    del refs[:n_jobs]
    o_ref = refs.pop(0)
    ob_ref = refs.pop(0) if emit_bf16 else None
    ot_ref = refs.pop(0) if split else None
    job_dst = refs[:n_jobs]
    del refs[:n_jobs]
    xb_ref, acc_ref = refs
    i = pl.program_id(0)
    j = pl.program_id(1)
    last_tile = i == pl.num_programs(0) - 1
    tm = x_ref.shape[0]

    def load_x():
        x = x_ref[...]
        if xt_ref is None:
            return x
        merged = jnp.concatenate([x[:tm - xt_ref.shape[0]], xt_ref[...]], axis=0)
        return jnp.where(last_tile, merged, x)

    @pl.when(j == 0)
    def _():
        x = load_x()
        xb_ref[...] = x.astype(BF16)
        acc_ref[...] = (2.0 * alpha) * x

    xb = xb_ref[...]
    g = jnp.dot(xb, wg_ref[...], preferred_element_type=F32)
    u = jnp.dot(xb, wu_ref[...], preferred_element_type=F32)
    h = (jax.nn.silu(g) * u).astype(BF16)
    acc_ref[...] += jnp.dot(h, wd_ref[...], preferred_element_type=F32)
    for src, dst in zip(job_src, job_dst):
        dst[...] = src[...].astype(BF16)

    @pl.when(j == pl.num_programs(1) - 1)
    def _():
        y = 0.5 * acc_ref[...]
        out = _layer_norm(y, g_ref[...], b_ref[...])
        o_ref[...] = out
        if emit_bf16:
            ob_ref[...] = out.astype(BF16)
        if split:
            @pl.when(last_tile)
            def _():
                ot_ref[...] = out[tm - ot_ref.shape[0]:, :]


def _ffn_ln(x, w_gu, w_down, ln_g, ln_b, ln_idx, alpha, *, tm, tf, emit_bf16=False, x_tail=None,
            split_rows=None, cast_jobs=()):
    d = x.shape[1]
    m = x.shape[0] + (0 if x_tail is None else x_tail.shape[0])
    f = w_down.shape[0]
    nj = f // tf
    n_tiles = m // tm
    row = lambda i, j: (i, 0)
    fixed = lambda i, j: (0, 0)
    x_row = lambda i, j: (jnp.minimum(i + jnp.minimum(j, 1), n_tiles - 1), 0)
    in_specs = [pl.BlockSpec((tm, d), x_row)]
    args = [x]
    if x_tail is not None:
        in_specs.append(pl.BlockSpec(x_tail.shape, fixed))
        args.append(x_tail)
    in_specs += [
        pl.BlockSpec((d, tf), lambda i, j: (0, j)),
        pl.BlockSpec((d, tf), lambda i, j: (0, nj + j)),
        pl.BlockSpec((tf, d), lambda i, j: (j, 0)),
        pl.BlockSpec((None, 1, d), lambda i, j: (ln_idx, 0, 0)),
        pl.BlockSpec((None, 1, d), lambda i, j: (ln_idx, 0, 0)),
    ]
    args += [w_gu, w_gu, w_down, ln_g, ln_b]
    m_head = m if split_rows is None else split_rows
    out_shape = [jax.ShapeDtypeStruct((m_head, d), F32)]
    out_specs = [pl.BlockSpec((tm, d), row)]
    if emit_bf16:
        out_shape.append(jax.ShapeDtypeStruct((m, d), BF16))
        out_specs.append(pl.BlockSpec((tm, d), row))
    if split_rows is not None:
        out_shape.append(jax.ShapeDtypeStruct((m - split_rows, d), F32))
        out_specs.append(pl.BlockSpec((m - split_rows, d), fixed))
    n_steps = n_tiles * nj
    for w, layer in cast_jobs:
        _, r, c = w.shape
        rb = _round_up(-(-r // n_steps), BF16_ROWS)
        nb = -(-r // rb)
        block = lambda i, j, nb=nb: (jnp.minimum(i * nj + j, nb - 1), 0)
        in_specs.append(pl.BlockSpec((None, rb, c), lambda i, j, nb=nb, layer=layer:
                                     (layer, jnp.minimum(i * nj + j, nb - 1), 0)))
        args.append(w)
        out_shape.append(jax.ShapeDtypeStruct((r, c), BF16))
        out_specs.append(pl.BlockSpec((rb, c), block))
    return pl.pallas_call(
        functools.partial(_ffn_ln_kernel, alpha, x_tail is not None, emit_bf16, split_rows is not None,
                          len(cast_jobs)),
        grid=(n_tiles, nj),
        in_specs=in_specs,
        out_specs=out_specs,
        out_shape=out_shape,
        scratch_shapes=[pltpu.VMEM((tm, d), BF16), pltpu.VMEM((tm, d), F32)],
        compiler_params=_params(2),
        name="ffn_ln",
    )(*args)


def _rope_cols(x, cos, sin_signed, first_half, half):
    fwd = pltpu.roll(x, x.shape[1] - half, 1)
    bwd = pltpu.roll(x, half, 1)
    return x * cos + jnp.where(first_half, fwd, bwd) * sin_signed


def _qkv_kernel(d_q, d_kv, head_dim, xb_ref, w_ref, cos_ref, sin_ref, q_ref, k_ref, v_ref):
    xb = xb_ref[...]
    bounds = [0, d_q // 2, d_q, d_q + 2 * d_kv]
    parts = [jnp.dot(xb, w_ref[:, lo:hi], preferred_element_type=F32) for lo, hi in zip(bounds, bounds[1:])]
    q = jnp.concatenate(parts[:2], axis=1)
    kv = parts[2]
    cos = cos_ref[...]
    sin = sin_ref[...]
    half = head_dim // 2
    lane = lax.broadcasted_iota(jnp.int32, cos.shape, 1)
    first_half = (lane % head_dim) < half
    q_scale = head_dim ** -0.5
    for c in range(d_q // LANES):
        sl = slice(c * LANES, (c + 1) * LANES)
        q_ref[:, sl] = (_rope_cols(q[:, sl], cos, sin, first_half, half) * q_scale).astype(BF16)
    for c in range(d_kv // LANES):
        sl = slice(c * LANES, (c + 1) * LANES)
        k_ref[:, sl] = _rope_cols(kv[:, sl], cos, sin, first_half, half)
    v_ref[...] = kv[:, d_kv:]


def _qkv_proj(xb, w_in, cos_tab, sin_tab, *, tm, d_q, d_kv, head_dim):
    m, d = xb.shape
    n_cols = d_q + 2 * d_kv
    return pl.pallas_call(
        functools.partial(_qkv_kernel, d_q, d_kv, head_dim),
        grid=(m // tm,),
        in_specs=[
            pl.BlockSpec((tm, d), lambda i: (i, 0)),
            pl.BlockSpec((d, n_cols), lambda i: (0, 0)),
            pl.BlockSpec((tm, LANES), lambda i: (i, 0)),
            pl.BlockSpec((tm, LANES), lambda i: (i, 0)),
        ],
        out_specs=[
            pl.BlockSpec((tm, d_q), lambda i: (i, 0)),
            pl.BlockSpec((tm, d_kv), lambda i: (i, 0)),
            pl.BlockSpec((tm, d_kv), lambda i: (i, 0)),
        ],
        out_shape=[
            jax.ShapeDtypeStruct((m, d_q), BF16),
            jax.ShapeDtypeStruct((m, d_kv), F32),
            jax.ShapeDtypeStruct((m, d_kv), F32),
        ],
        compiler_params=_params(1),
        name="qkv_proj",
    )(xb, w_in, cos_tab, sin_tab)


def _lane_tile4(x128, want_high):
    lane = lax.broadcasted_iota(jnp.int32, x128.shape, 1)
    swapped = pltpu.roll(x128, LANES // 2, 1)
    low = lane < LANES // 2
    both = jnp.where(low, swapped, x128) if want_high else jnp.where(low, x128, swapped)
    return jnp.concatenate([both, both], axis=1)


def _band_attention(problems, sink, n_kv, group, head_dim, between=None):
    rows = problems[0][0].shape[0]
    gw = group * head_dim
    r_idx = lax.broadcasted_iota(jnp.int32, (rows, KEY_SLOTS), 0)
    s_idx = lax.broadcasted_iota(jnp.int32, (rows, KEY_SLOTS), 1)
    head_of_lane = lax.broadcasted_iota(jnp.int32, (rows, gw), 1) // head_dim
    head_keep = [(head_of_lane == g).astype(F32) for g in range(group)]

    scores, values = [], []
    for q, kband, vband, row0, kpos0 in problems:
        diff = r_idx - row0 + WINDOW - s_idx
        valid = (diff >= 0) & (diff <= WINDOW) & (s_idx + kpos0 >= 0)
        for kh in range(n_kv):
            col = (kh * head_dim) // LANES
            high = ((kh * head_dim) % LANES) != 0
            kk = _lane_tile4(kband[:, col * LANES:(col + 1) * LANES], high).astype(BF16)
            values.append(_lane_tile4(vband[:, col * LANES:(col + 1) * LANES], high).astype(BF16))
            qg = q[:, kh * gw:(kh + 1) * gw]
            qs = jnp.concatenate([qg * head_keep[g] for g in range(group)], axis=0).astype(BF16)
            s = lax.dot_general(qs, kk, (((1,), (1,)), ((), ())), preferred_element_type=F32)
            scores.append(jnp.where(valid[None], s.reshape(group, rows, KEY_SLOTS), -jnp.inf))
    s = jnp.concatenate(scores, axis=0)
    sink_all = jnp.concatenate([sink] * len(problems), axis=0)
    mx = jnp.maximum(jnp.max(s, axis=-1, keepdims=True), sink_all)
    p = jnp.exp(s - mx)
    den = jnp.sum(p, axis=-1, keepdims=True) + jnp.exp(sink_all - mx)
    p = p * (1.0 / den)

    if between is not None:
        between()

    outs = []
    for i in range(len(problems)):
        slabs = []
        for kh in range(n_kv):
            c = i * n_kv + kh
            pg = p[c * group:(c + 1) * group].reshape(group * rows, KEY_SLOTS).astype(BF16)
            o = jnp.dot(pg, values[c], preferred_element_type=F32).reshape(group, rows, gw)
            out = o[0]
            for g in range(1, group):
                out = jnp.where(head_of_lane == g, o[g], out)
            slabs.append(out)
        outs.append(jnp.concatenate(slabs, axis=1))
    return outs


def _sink_column(sink_ref):
    return sink_ref[...][:, :, 0:1]


def _attn_sample_kernel(seq, n_kv, group, head_dim, sink_ref, q_ref, kc_ref, kn_ref, vc_ref, vn_ref,
                        _, o_ref):
    n_seq = kc_ref.shape[0]
    d_kv = kn_ref.shape[1]
    per_block = SUBLANES // seq
    pad = jnp.zeros((KEY_SLOTS - WINDOW - SUBLANES, d_kv), F32)
    qf = q_ref[...].astype(F32)
    problems = []
    for i in range(n_seq):
        blk, row0 = i // per_block, (i % per_block) * seq
        rows = slice(blk * SUBLANES, (blk + 1) * SUBLANES)
        k8, v8 = kn_ref[rows, :], vn_ref[rows, :]
        if row0:
            k8 = pltpu.roll(k8, SUBLANES - row0, 0)
            v8 = pltpu.roll(v8, SUBLANES - row0, 0)
        kband = jnp.concatenate([kc_ref[i], k8, pad], axis=0)
        vband = jnp.concatenate([vc_ref[i], v8, pad], axis=0)
        problems.append((qf[rows, :], kband, vband, row0, PAST_LEN - WINDOW))
    outs = _band_attention(problems, _sink_column(sink_ref), n_kv, group, head_dim)
    r_idx = lax.broadcasted_iota(jnp.int32, outs[0].shape, 0)
    blocks = []
    for blk in range(n_seq // per_block):
        out = outs[blk * per_block]
        for s in range(1, per_block):
            out = jnp.where(r_idx >= s * seq, outs[blk * per_block + s], out)
        blocks.append(out)
    o_ref[...] = jnp.concatenate(blocks, axis=0).astype(BF16)


def _attn_sample(attn, q, k, v, k_cache, v_cache, sink_tab, layer, *, row0, seq, seqs_per_step, n_kv,
                 group, head_dim):
    d_q = q.shape[1]
    d_kv = k.shape[1]
    nbatch = k_cache.shape[1]
    ns = seqs_per_step
    rows = ns * seq
    blk0 = row0 // rows
    new = lambda b: (blk0 + b, 0)
    cache = lambda b: (layer, b, 0, 0)
    return pl.pallas_call(
        functools.partial(_attn_sample_kernel, seq, n_kv, group, head_dim),
        grid=(nbatch // ns,),
        in_specs=[
            pl.BlockSpec((None, n_kv * group, 1, LANES), lambda b: (layer, 0, 0, 0)),
            pl.BlockSpec((rows, d_q), new),
            pl.BlockSpec((None, ns, WINDOW, d_kv), cache),
            pl.BlockSpec((rows, d_kv), new),
            pl.BlockSpec((None, ns, WINDOW, d_kv), cache),
            pl.BlockSpec((rows, d_kv), new),
            pl.BlockSpec(memory_space=pl.ANY),
        ],
        out_specs=pl.BlockSpec((rows, d_q), new),
        out_shape=jax.ShapeDtypeStruct(attn.shape, attn.dtype),
        input_output_aliases={6: 0},
        compiler_params=_params(1),
        name="attn_sample",
    )(sink_tab, q, k_cache, k, v_cache, v, attn)


def _conv_attn_kernel(nc, blocks_per_seq, n_kv, group, head_dim,
                      xb_ref, wb_ref, wc_ref, wh_ref, cw_ref, t_ref, f1_ref, f2_ref,
                      sink_ref, q_ref, kp_ref, kc_ref, vp_ref, vc_ref,
                      y_ref, u_ref, o_ref, ubuf):
    tm = xb_ref.shape[0]
    i = pl.program_id(0)
    c = pl.program_id(1)

    @pl.when(i == 0)
    def _():
        ubuf[c] = jnp.zeros(ubuf.shape[1:], F32)

    carry = ubuf[c]

    n_blocks = tm // WINDOW
    per_step = n_blocks // nc
    problems, starts = [], []
    for s in range(per_step):
        b = c * per_step + s
        r0 = pl.multiple_of(b * WINDOW, WINDOW)
        rp = pl.multiple_of(jnp.maximum(b - 1, 0) * WINDOW, WINDOW)
        n = lax.rem(i * n_blocks + b, blocks_per_seq)
        k_before = jnp.where(b == 0, kp_ref[...], kc_ref[pl.ds(rp, WINDOW), :])
        v_before = jnp.where(b == 0, vp_ref[...], vc_ref[pl.ds(rp, WINDOW), :])
        kband = jnp.concatenate([k_before, kc_ref[pl.ds(r0, WINDOW), :]], axis=0)
        vband = jnp.concatenate([v_before, vc_ref[pl.ds(r0, WINDOW), :]], axis=0)
        problems.append((q_ref[pl.ds(r0, WINDOW), :].astype(F32), kband, vband, 0, (n - 1) * WINDOW))
        starts.append(r0)
    proj = []

    def conv_input_matmuls():
        xb = xb_ref[...]
        for w_ref in (wc_ref, wh_ref):
            proj.append(jnp.dot(xb, w_ref[...], preferred_element_type=F32))

    outs = _band_attention(problems, _sink_column(sink_ref), n_kv, group, head_dim,
                           between=conv_input_matmuls)
    cc, ch = proj
    cb = jnp.dot(xb_ref[...], wb_ref[...], preferred_element_type=F32)
    u = cc * ch
    row8 = lax.broadcasted_iota(jnp.int32, carry.shape, 0)

    def shifted(k):
        r = pltpu.roll(u, k, 0)
        head = jnp.where(row8 < k, pltpu.roll(carry, k, 0), r[0:SUBLANES])
        return jnp.concatenate([head, r[SUBLANES:]], axis=0)

    t = t_ref[...]
    last = i == pl.num_programs(0) - 1
    u_m1 = jnp.where(t >= 1, shifted(1), jnp.where(last, f1_ref[...], 0.0))
    u_m2 = jnp.where(t >= 2, shifted(2), jnp.where(last, f2_ref[...], 0.0))
    cw = cw_ref[...]
    conv = cw[0:1, :] * u_m2 + cw[1:2, :] * u_m1 + cw[2:3, :] * u

    for r0, out in zip(starts, outs):
        o_ref[pl.ds(r0, WINDOW), :] = out.astype(BF16)
    y_ref[...] = (cb * conv).astype(BF16)
    u_ref[...] = u
    ubuf[c] = u[tm - SUBLANES:, :]


def _conv_attn(xb, w_in, conv_w, t_idx, fill1, fill2, q, k, v, sink_tab, layer, *, tm, tc, col0, d_conv,
               seq, n_kv, group, head_dim):
    m, d = xb.shape
    d_q = q.shape[1]
    d_kv = k.shape[1]
    nc = d_conv // tc
    off = col0 // tc
    n_blocks = tm // WINDOW
    w_spec = lambda kk: pl.BlockSpec((d, tc), lambda i, c: (0, off + kk * nc + c))
    tile = lambda i, c: (i, 0)
    before = lambda i, c: (jnp.maximum(i * n_blocks - 1, 0), 0)
    return pl.pallas_call(
        functools.partial(_conv_attn_kernel, nc, seq // WINDOW, n_kv, group, head_dim),
        grid=(pl.cdiv(m, tm), nc),
        in_specs=[
            pl.BlockSpec((tm, d), tile),
            w_spec(0), w_spec(1), w_spec(2),
            pl.BlockSpec((None, CONV_W, tc), lambda i, c: (layer, 0, c)),
            pl.BlockSpec((tm, 1), tile),
            pl.BlockSpec((tm, tc), lambda i, c: (0, c)),
            pl.BlockSpec((tm, tc), lambda i, c: (0, c)),
            pl.BlockSpec((None, n_kv * group, 1, LANES), lambda i, c: (layer, 0, 0, 0)),
            pl.BlockSpec((tm, d_q), tile),
            pl.BlockSpec((WINDOW, d_kv), before),
            pl.BlockSpec((tm, d_kv), tile),
            pl.BlockSpec((WINDOW, d_kv), before),
            pl.BlockSpec((tm, d_kv), tile),
        ],
        out_specs=[
            pl.BlockSpec((tm, tc), lambda i, c: (i, c)),
            pl.BlockSpec((tm, tc), lambda i, c: (i, c)),
            pl.BlockSpec((tm, d_q), tile),
        ],
        out_shape=[
            jax.ShapeDtypeStruct((m, d_conv), BF16),
            jax.ShapeDtypeStruct((m, d_conv), F32),
            jax.ShapeDtypeStruct((m, d_q), BF16),
        ],
        scratch_shapes=[pltpu.VMEM((nc, SUBLANES, tc), F32)],
        compiler_params=_params(2),
        name="conv_attn",
    )(xb, w_in, w_in, w_in, conv_w, t_idx, fill1, fill2, sink_tab, q, k, k, v, v)


def _mix_ln_kernel(alpha, x_ref, xb_ref, a_ref, c_ref, wga_ref, wgc_ref, wa_ref, wc_ref, wo_ref, g_ref,
                   b_ref, o_ref, acc_ref):
    j = pl.program_id(1)

    @pl.when(j == 0)
    def _():
        acc_ref[...] = jnp.zeros_like(acc_ref)

    xb = xb_ref[...]
    ga = jnp.dot(xb, wga_ref[...], preferred_element_type=F32)
    gc = jnp.dot(xb, wgc_ref[...], preferred_element_type=F32)
    pa = jnp.dot(a_ref[...], wa_ref[...], preferred_element_type=F32)
    pc = jnp.dot(c_ref[...], wc_ref[...], preferred_element_type=F32)
    merged = (jax.nn.sigmoid(ga) * pa + jax.nn.sigmoid(gc) * pc).astype(BF16)
    acc_ref[...] += jnp.dot(merged, wo_ref[...], preferred_element_type=F32)

    @pl.when(j == pl.num_programs(1) - 1)
    def _():
        y = alpha * x_ref[...] + acc_ref[...]
        o_ref[...] = _layer_norm(y, g_ref[...], b_ref[...])


def _mix_ln(x, xb, attn, yconv, w_in, w_a, w_c, w_o, ln_g, ln_b, ln_idx, alpha, *, tm, tc, ga_col0,
            gc_col0):
    m, d = x.shape
    d_q = attn.shape[1]
    d_conv = yconv.shape[1]
    nj = d // tc
    late = lambda i, j: (jnp.maximum(i - jnp.where(j < nj // 2, 1, 0), 0), 0)
    return pl.pallas_call(
        functools.partial(_mix_ln_kernel, alpha),
        grid=(m // tm, nj),
        in_specs=[
            pl.BlockSpec((tm, d), late),
            pl.BlockSpec((tm, d), lambda i, j: (i, 0)),
            pl.BlockSpec((tm, d_q), lambda i, j: (i, 0)),
            pl.BlockSpec((tm, d_conv), lambda i, j: (i, 0)),
            pl.BlockSpec((d, tc), lambda i, j: (0, ga_col0 // tc + j)),
            pl.BlockSpec((d, tc), lambda i, j: (0, gc_col0 // tc + j)),
            pl.BlockSpec((d_q, tc), lambda i, j: (0, j)),
            pl.BlockSpec((d_conv, tc), lambda i, j: (0, j)),
            pl.BlockSpec((tc, d), lambda i, j: (j, 0)),
            pl.BlockSpec((None, 1, d), lambda i, j: (ln_idx, 0, 0)),
            pl.BlockSpec((None, 1, d), lambda i, j: (ln_idx, 0, 0)),
        ],
        out_specs=pl.BlockSpec((tm, d), lambda i, j: (i, 0)),
        out_shape=jax.ShapeDtypeStruct((m, d), F32),
        scratch_shapes=[pltpu.VMEM((tm, d), F32)],
        compiler_params=_params(2),
        name="mix_ln",
    )(x, xb, attn, yconv, w_in, w_in, w_a, w_c, w_o, ln_g, ln_b)


def _rope_tables(pos, head_dim):
    inv_freq = ROPE_THETA ** (-jnp.arange(0, head_dim, 2, dtype=F32) / head_dim)
    ang = pos.astype(F32)[:, None] * inv_freq[None, :]
    cos = jnp.cos(ang)
    sin = jnp.sin(ang)
    reps = LANES // head_dim
    cos_tab = jnp.tile(jnp.concatenate([cos, cos], axis=1), (1, reps))
    sin_tab = jnp.tile(jnp.concatenate([-sin, sin], axis=1), (1, reps))
    return cos_tab, sin_tab


def _largest_divisor(n, cap, multiple_of=1):
    return max(k for k in range(multiple_of, cap + 1, multiple_of) if n % k == 0)


def _seq_tails(a, n_seq, seq, rows):
    return jnp.stack([a[(b + 1) * seq - rows:(b + 1) * seq] for b in range(n_seq)])


def kernel(x_prompt, x_sample, cache_k_win, cache_v_win, state_conv, ln_g, ln_b, w_in, sinks, conv_w,
           w_branch_attn, w_branch_conv, w_out, ffn1_gu, ffn1_down, ffn2_gu, ffn2_down):
    depth = w_in.shape[0]
    bp, tp, d = x_prompt.shape
    bs, ts, _ = x_sample.shape
    n_kv, head_dim = cache_k_win.shape[-2:]
    d_q = w_branch_attn.shape[1]
    d_conv = conv_w.shape[-1]
    d_kv = n_kv * head_dim
    group = d_q // d_kv
    mp, ms = bp * tp, bs * ts
    m = mp + ms
    alpha = (2.0 * depth) ** 0.25
    tm = _largest_divisor(m, MAX_ROW_TILE, BF16_ROWS)
    ns = _largest_divisor(bs, SAMPLE_SEQS_PER_STEP, SUBLANES // ts)
    assert tm >= ms and SUBLANES % ts == 0 and mp % (ns * ts) == 0 and tp % WINDOW == 0
    assert ts >= CONV_W - 1 and tp >= WINDOW
    tf = 512
    tc = 512
    tm_mix = MIX_ROW_TILE
    assert mp % tm_mix == 0 and ms <= tm_mix and tm_mix % (WINDOW * (d_conv // tc)) == 0
    conv_col0 = d_q + 2 * d_kv
    ga_col0 = conv_col0 + 3 * d_conv
    gc_col0 = ga_col0 + d

    ln_g = ln_g.reshape(depth * 3, 1, d)
    ln_b = ln_b.reshape(depth * 3, 1, d)
    sink_tab = jnp.broadcast_to(sinks[:, :, None, None], sinks.shape + (1, LANES))

    t_prompt = jnp.tile(jnp.arange(tp, dtype=jnp.int32), bp)
    t_sample = jnp.tile(jnp.arange(ts, dtype=jnp.int32), bs)
    cos_tab, sin_tab = _rope_tables(jnp.concatenate([t_prompt, PAST_LEN + t_sample]), head_dim)
    t_idx = jnp.concatenate([t_prompt, t_sample]).reshape(m, 1)
    k_cache = cache_k_win.reshape(depth, bs, WINDOW, d_kv)
    v_cache = cache_v_win.reshape(depth, bs, WINDOW, d_kv)

    f1_gu, f1_down = ffn1_gu[0].astype(BF16), ffn1_down[0].astype(BF16)
    x = x_prompt.reshape(mp, d)
    x_tail = x_sample.reshape(ms, d)
    ks_p, vs_p, cs_p, ks_s, vs_s, cs_s = [], [], [], [], [], []
    for l in range(depth):
        jobs = [(w, l) for w in (w_in, w_branch_attn, w_branch_conv, w_out, ffn2_gu, ffn2_down)]
        x1, x1b, w_in_b, w_a_b, w_c_b, w_o_b, f2_gu, f2_down = _ffn_ln(
            x, f1_gu, f1_down, ln_g, ln_b, 3 * l, alpha, tm=tm, tf=tf, emit_bf16=True, x_tail=x_tail,
            cast_jobs=jobs)
        q, k, v = _qkv_proj(x1b, w_in_b, cos_tab, sin_tab, tm=tm, d_q=d_q, d_kv=d_kv, head_dim=head_dim)
        st = state_conv[l]
        zeros = jnp.zeros((bs, ts, d_conv), F32)
        fill1 = zeros.at[:, 0].set(st[:, 1]).reshape(ms, d_conv)
        fill2 = zeros.at[:, 0].set(st[:, 0]).at[:, 1].set(st[:, 1]).reshape(ms, d_conv)
        fill1 = jnp.pad(fill1, ((0, tm_mix - ms), (0, 0)))
        fill2 = jnp.pad(fill2, ((0, tm_mix - ms), (0, 0)))
        yconv, u, attn = _conv_attn(x1b, w_in_b, conv_w, t_idx, fill1, fill2, q, k, v, sink_tab, l,
                                    tm=tm_mix, tc=tc, col0=conv_col0, d_conv=d_conv, seq=tp, n_kv=n_kv,
                                    group=group, head_dim=head_dim)
        attn = _attn_sample(attn, q, k, v, k_cache, v_cache, sink_tab, l, row0=mp, seq=ts,
                            seqs_per_step=ns, n_kv=n_kv, group=group, head_dim=head_dim)
        x2 = _mix_ln(x1, x1b, attn, yconv, w_in_b, w_a_b, w_c_b, w_o_b, ln_g, ln_b, 3 * l + 1, alpha,
                     tm=tm, tc=tc, ga_col0=ga_col0, gc_col0=gc_col0)
        if l + 1 < depth:
            x, f1_gu, f1_down = _ffn_ln(x2, f2_gu, f2_down, ln_g, ln_b, 3 * l + 2, alpha, tm=tm, tf=tf,
                                        cast_jobs=[(ffn1_gu, l + 1), (ffn1_down, l + 1)])
            x_tail = None
        else:
            y_p, y_s = _ffn_ln(x2, f2_gu, f2_down, ln_g, ln_b, 3 * l + 2, alpha, tm=tm, tf=tf,
                               split_rows=mp)

        ks_p.append(_seq_tails(k, bp, tp, WINDOW).reshape(bp, WINDOW, n_kv, head_dim))
        vs_p.append(_seq_tails(v, bp, tp, WINDOW).reshape(bp, WINDOW, n_kv, head_dim))
        cs_p.append(_seq_tails(u, bp, tp, CONV_W - 1))
        k_new = k[mp:].reshape(bs, ts, n_kv, head_dim)
        v_new = v[mp:].reshape(bs, ts, n_kv, head_dim)
        ks_s.append(jnp.concatenate([cache_k_win[l][:, ts:], k_new], axis=1))
        vs_s.append(jnp.concatenate([cache_v_win[l][:, ts:], v_new], axis=1))
        cs_s.append(u[mp:].reshape(bs, ts, d_conv)[:, -(CONV_W - 1):])

    return (y_p.reshape(bp, tp, d), y_s.reshape(bs, ts, d), jnp.stack(ks_p), jnp.stack(vs_p),
            jnp.stack(cs_p), jnp.stack(ks_s), jnp.stack(vs_s), jnp.stack(cs_s))
```

```python
import functools

import jax
import jax.numpy as jnp
from jax import lax
from jax.experimental import pallas as pl
from jax.experimental.pallas import tpu as pltpu

F32 = jnp.float32
BF16 = jnp.bfloat16

PAST_LEN = 16384
WINDOW = 128
ROPE_THETA = 10000.0
LN_EPS = 1e-5
CONV_W = 3

LANES = 128
SUBLANES = 8
BF16_ROWS = 16
VMEM_LIMIT_BYTES = 62 * 1024 * 1024

KEY_SLOTS = 2 * WINDOW
MAX_ROW_TILE = 640
MAX_ROW_TILE_WIDE = 832
MIX_ROW_TILE = 512
SAMPLE_SEQS_PER_STEP = 8


def _params(n_axes):
    return pltpu.CompilerParams(
        dimension_semantics=("arbitrary",) * n_axes,
        vmem_limit_bytes=VMEM_LIMIT_BYTES,
    )


def _layer_norm(y, g, b):
    mu = jnp.mean(y, axis=-1, keepdims=True)
    d = y - mu
    var = jnp.mean(d * d, axis=-1, keepdims=True)
    return d * lax.rsqrt(var + LN_EPS) * g + b


def _round_up(n, k):
    return -(-n // k) * k


def _ffn_ln_kernel(alpha, has_tail, emit_bf16, split, n_jobs, *refs):
    refs = list(refs)
    x_ref = refs.pop(0)
    xt_ref = refs.pop(0) if has_tail else None
    wg_ref, wu_ref, wd_ref, g_ref, b_ref = refs[:5]
    del refs[:5]
    job_src = refs[:n_jobs]
    del refs[:n_jobs]
    o_ref = refs.pop(0)
    ob_ref = refs.pop(0) if emit_bf16 else None
    ot_ref = refs.pop(0) if split else None
    job_dst = refs[:n_jobs]
    del refs[:n_jobs]
    (xb_ref,) = refs
    acc_ref = o_ref
    i = pl.program_id(0)
    j = pl.program_id(1)
    last_tile = i == pl.num_programs(0) - 1
    tm = x_ref.shape[0]

    def load_x():
        x = x_ref[...]
        if xt_ref is None:
            return x
        merged = jnp.concatenate([x[:tm - xt_ref.shape[0]], xt_ref[...]], axis=0)
        return jnp.where(last_tile, merged, x)

    @pl.when(j == 0)
    def _():
        xb_ref[...] = load_x().astype(BF16)
        acc_ref[...] = jnp.zeros_like(acc_ref)

    xb = xb_ref[...]
    g = jnp.dot(xb, wg_ref[...], preferred_element_type=F32)
    u = jnp.dot(xb, wu_ref[...], preferred_element_type=F32)
    h = (jax.nn.silu(g) * u).astype(BF16)
    acc_ref[...] += jnp.dot(h, wd_ref[...], preferred_element_type=F32)
    for src, dst in zip(job_src, job_dst):
        dst[...] = src[...].astype(BF16)

    @pl.when(j == pl.num_programs(1) - 1)
    def _():
        y = alpha * load_x() + 0.5 * acc_ref[...]
        out = _layer_norm(y, g_ref[...], b_ref[...])
        o_ref[...] = out
        if emit_bf16:
            ob_ref[...] = out.astype(BF16)
        if split:
            @pl.when(last_tile)
            def _():
                ot_ref[...] = out[tm - ot_ref.shape[0]:, :]


def _ffn_ln(x, w_gu, w_down, ln_g, ln_b, ln_idx, alpha, *, tm, tf, emit_bf16=False, x_tail=None,
            split_rows=None, cast_jobs=()):
    d = x.shape[1]
    m = x.shape[0] + (0 if x_tail is None else x_tail.shape[0])
    f = w_down.shape[0]
    nj = f // tf
    n_tiles = m // tm
    row = lambda i, j: (i, 0)
    fixed = lambda i, j: (0, 0)
    in_specs = [pl.BlockSpec((tm, d), row)]
    args = [x]
    if x_tail is not None:
        in_specs.append(pl.BlockSpec(x_tail.shape, fixed))
        args.append(x_tail)
    in_specs += [
        pl.BlockSpec((d, tf), lambda i, j: (0, j)),
        pl.BlockSpec((d, tf), lambda i, j: (0, nj + j)),
        pl.BlockSpec((tf, d), lambda i, j: (j, 0)),
        pl.BlockSpec((None, 1, d), lambda i, j: (ln_idx, 0, 0)),
        pl.BlockSpec((None, 1, d), lambda i, j: (ln_idx, 0, 0)),
    ]
    args += [w_gu, w_gu, w_down, ln_g, ln_b]
    m_head = m if split_rows is None else split_rows
    out_shape = [jax.ShapeDtypeStruct((m_head, d), F32)]
    out_specs = [pl.BlockSpec((tm, d), row)]
    if emit_bf16:
        out_shape.append(jax.ShapeDtypeStruct((m, d), BF16))
        out_specs.append(pl.BlockSpec((tm, d), row))
    if split_rows is not None:
        out_shape.append(jax.ShapeDtypeStruct((m - split_rows, d), F32))
        out_specs.append(pl.BlockSpec((m - split_rows, d), fixed))
    n_steps = n_tiles * nj
    for w, layer in cast_jobs:
        _, r, c = w.shape
        rb = _round_up(-(-r // n_steps), BF16_ROWS)
        nb = -(-r // rb)
        block = lambda i, j, nb=nb: (jnp.minimum(i * nj + j, nb - 1), 0)
        in_specs.append(pl.BlockSpec((None, rb, c), lambda i, j, nb=nb, layer=layer:
                                     (layer, jnp.minimum(i * nj + j, nb - 1), 0)))
        args.append(w)
        out_shape.append(jax.ShapeDtypeStruct((r, c), BF16))
        out_specs.append(pl.BlockSpec((rb, c), block))
    return pl.pallas_call(
        functools.partial(_ffn_ln_kernel, alpha, x_tail is not None, emit_bf16, split_rows is not None,
                          len(cast_jobs)),
        grid=(n_tiles, nj),
        in_specs=in_specs,
        out_specs=out_specs,
        out_shape=out_shape,
        scratch_shapes=[pltpu.VMEM((tm, d), BF16)],
        compiler_params=_params(2),
        name="ffn_ln",
    )(*args)


def _rope_cols(x, cos, sin_signed, first_half, half):
    fwd = pltpu.roll(x, x.shape[1] - half, 1)
    bwd = pltpu.roll(x, half, 1)
    return x * cos + jnp.where(first_half, fwd, bwd) * sin_signed


def _qkv_kernel(d_q, d_kv, head_dim, xb_ref, w_ref, cos_ref, sin_ref, q_ref, k_ref, v_ref):
    xb = xb_ref[...]
    bounds = [0, d_q // 2, d_q, d_q + 2 * d_kv]
    parts = [jnp.dot(xb, w_ref[:, lo:hi], preferred_element_type=F32) for lo, hi in zip(bounds, bounds[1:])]
    q = jnp.concatenate(parts[:2], axis=1)
    kv = parts[2]
    cos = cos_ref[...]
    sin = sin_ref[...]
    half = head_dim // 2
    lane = lax.broadcasted_iota(jnp.int32, cos.shape, 1)
    first_half = (lane % head_dim) < half
    q_scale = head_dim ** -0.5
    for c in range(d_q // LANES):
        sl = slice(c * LANES, (c + 1) * LANES)
        q_ref[:, sl] = (_rope_cols(q[:, sl], cos, sin, first_half, half) * q_scale).astype(BF16)
    for c in range(d_kv // LANES):
        sl = slice(c * LANES, (c + 1) * LANES)
        k_ref[:, sl] = _rope_cols(kv[:, sl], cos, sin, first_half, half)
    v_ref[...] = kv[:, d_kv:]


def _qkv_proj(xb, w_in, cos_tab, sin_tab, *, tm, d_q, d_kv, head_dim):
    m, d = xb.shape
    n_cols = d_q + 2 * d_kv
    return pl.pallas_call(
        functools.partial(_qkv_kernel, d_q, d_kv, head_dim),
        grid=(m // tm,),
        in_specs=[
            pl.BlockSpec((tm, d), lambda i: (i, 0)),
            pl.BlockSpec((d, n_cols), lambda i: (0, 0)),
            pl.BlockSpec((tm, LANES), lambda i: (i, 0)),
            pl.BlockSpec((tm, LANES), lambda i: (i, 0)),
        ],
        out_specs=[
            pl.BlockSpec((tm, d_q), lambda i: (i, 0)),
            pl.BlockSpec((tm, d_kv), lambda i: (i, 0)),
            pl.BlockSpec((tm, d_kv), lambda i: (i, 0)),
        ],
        out_shape=[
            jax.ShapeDtypeStruct((m, d_q), BF16),
            jax.ShapeDtypeStruct((m, d_kv), F32),
            jax.ShapeDtypeStruct((m, d_kv), F32),
        ],
        compiler_params=_params(1),
        name="qkv_proj",
    )(xb, w_in, cos_tab, sin_tab)


def _lane_tile4(x128, want_high):
    lane = lax.broadcasted_iota(jnp.int32, x128.shape, 1)
    swapped = pltpu.roll(x128, LANES // 2, 1)
    low = lane < LANES // 2
    both = jnp.where(low, swapped, x128) if want_high else jnp.where(low, x128, swapped)
    return jnp.concatenate([both, both], axis=1)


def _band_attention(problems, sink, n_kv, group, head_dim, between=None):
    rows = problems[0][0].shape[0]
    gw = group * head_dim
    r_idx = lax.broadcasted_iota(jnp.int32, (rows, KEY_SLOTS), 0)
    s_idx = lax.broadcasted_iota(jnp.int32, (rows, KEY_SLOTS), 1)
    head_of_lane = lax.broadcasted_iota(jnp.int32, (rows, gw), 1) // head_dim
    head_keep = [(head_of_lane == g).astype(F32) for g in range(group)]

    scores, values = [], []
    for q, kband, vband, row0, kpos0 in problems:
        diff = r_idx - row0 + WINDOW - s_idx
        valid = (diff >= 0) & (diff <= WINDOW) & (s_idx + kpos0 >= 0)
        for kh in range(n_kv):
            col = (kh * head_dim) // LANES
            high = ((kh * head_dim) % LANES) != 0
            kk = _lane_tile4(kband[:, col * LANES:(col + 1) * LANES], high).astype(BF16)
            values.append(_lane_tile4(vband[:, col * LANES:(col + 1) * LANES], high).astype(BF16))
            qg = q[:, kh * gw:(kh + 1) * gw]
            qs = jnp.concatenate([qg * head_keep[g] for g in range(group)], axis=0).astype(BF16)
            s = lax.dot_general(qs, kk, (((1,), (1,)), ((), ())), preferred_element_type=F32)
            scores.append(jnp.where(valid[None], s.reshape(group, rows, KEY_SLOTS), -jnp.inf))
    s = jnp.concatenate(scores, axis=0)
    sink_all = jnp.concatenate([sink] * len(problems), axis=0)
    mx = jnp.maximum(jnp.max(s, axis=-1, keepdims=True), sink_all)
    p = jnp.exp(s - mx)
    den = jnp.sum(p, axis=-1, keepdims=True) + jnp.exp(sink_all - mx)
    p = p * (1.0 / den)

    if between is not None:
        between()

    outs = []
    for i in range(len(problems)):
        slabs = []
        for kh in range(n_kv):
            c = i * n_kv + kh
            pg = p[c * group:(c + 1) * group].reshape(group * rows, KEY_SLOTS).astype(BF16)
            o = jnp.dot(pg, values[c], preferred_element_type=F32).reshape(group, rows, gw)
            out = o[0]
            for g in range(1, group):
                out = jnp.where(head_of_lane == g, o[g], out)
            slabs.append(out)
        outs.append(jnp.concatenate(slabs, axis=1))
    return outs


def _sink_column(sink_ref):
    return sink_ref[...][:, :, 0:1]


def _attn_sample_kernel(seq, n_kv, group, head_dim, sink_ref, q_ref, kc_ref, kn_ref, vc_ref, vn_ref,
                        _, o_ref):
    n_seq = kc_ref.shape[0]
    d_kv = kn_ref.shape[1]
    per_block = SUBLANES // seq
    pad = jnp.zeros((KEY_SLOTS - WINDOW - SUBLANES, d_kv), F32)
    qf = q_ref[...].astype(F32)
    problems = []
    for i in range(n_seq):
        blk, row0 = i // per_block, (i % per_block) * seq
        rows = slice(blk * SUBLANES, (blk + 1) * SUBLANES)
        k8, v8 = kn_ref[rows, :], vn_ref[rows, :]
        if row0:
            k8 = pltpu.roll(k8, SUBLANES - row0, 0)
            v8 = pltpu.roll(v8, SUBLANES - row0, 0)
        kband = jnp.concatenate([kc_ref[i], k8, pad], axis=0)
        vband = jnp.concatenate([vc_ref[i], v8, pad], axis=0)
        problems.append((qf[rows, :], kband, vband, row0, PAST_LEN - WINDOW))
    outs = _band_attention(problems, _sink_column(sink_ref), n_kv, group, head_dim)
    r_idx = lax.broadcasted_iota(jnp.int32, outs[0].shape, 0)
    blocks = []
    for blk in range(n_seq // per_block):
        out = outs[blk * per_block]
        for s in range(1, per_block):
            out = jnp.where(r_idx >= s * seq, outs[blk * per_block + s], out)
        blocks.append(out)
    o_ref[...] = jnp.concatenate(blocks, axis=0).astype(BF16)


def _attn_sample(attn, q, k, v, k_cache, v_cache, sink_tab, layer, *, row0, seq, seqs_per_step, n_kv,
                 group, head_dim):
    d_q = q.shape[1]
    d_kv = k.shape[1]
    nbatch = k_cache.shape[1]
    ns = seqs_per_step
    rows = ns * seq
    blk0 = row0 // rows
    new = lambda b: (blk0 + b, 0)
    cache = lambda b: (layer, b, 0, 0)
    return pl.pallas_call(
        functools.partial(_attn_sample_kernel, seq, n_kv, group, head_dim),
        grid=(nbatch // ns,),
        in_specs=[
            pl.BlockSpec((None, n_kv * group, 1, LANES), lambda b: (layer, 0, 0, 0)),
            pl.BlockSpec((rows, d_q), new),
            pl.BlockSpec((None, ns, WINDOW, d_kv), cache),
            pl.BlockSpec((rows, d_kv), new),
            pl.BlockSpec((None, ns, WINDOW, d_kv), cache),
            pl.BlockSpec((rows, d_kv), new),
            pl.BlockSpec(memory_space=pl.ANY),
        ],
        out_specs=pl.BlockSpec((rows, d_q), new),
        out_shape=jax.ShapeDtypeStruct(attn.shape, attn.dtype),
        input_output_aliases={6: 0},
        compiler_params=_params(1),
        name="attn_sample",
    )(sink_tab, q, k_cache, k, v_cache, v, attn)


def _conv_attn_kernel(nc, blocks_per_seq, n_kv, group, head_dim,
                      xb_ref, wb_ref, wc_ref, wh_ref, cw_ref, t_ref, f1_ref, f2_ref,
                      sink_ref, q_ref, kp_ref, kc_ref, vp_ref, vc_ref,
                      y_ref, u_ref, o_ref, ubuf):
    tm = xb_ref.shape[0]
    i = pl.program_id(0)
    c = pl.program_id(1)

    @pl.when(i == 0)
    def _():
        ubuf[c] = jnp.zeros(ubuf.shape[1:], F32)

    carry = ubuf[c]

    n_blocks = tm // WINDOW
    per_step = n_blocks // nc
    problems, starts = [], []
    for s in range(per_step):
        b = c * per_step + s
        r0 = pl.multiple_of(b * WINDOW, WINDOW)
        rp = pl.multiple_of(jnp.maximum(b - 1, 0) * WINDOW, WINDOW)
        n = lax.rem(i * n_blocks + b, blocks_per_seq)
        k_before = jnp.where(b == 0, kp_ref[...], kc_ref[pl.ds(rp, WINDOW), :])
        v_before = jnp.where(b == 0, vp_ref[...], vc_ref[pl.ds(rp, WINDOW), :])
        kband = jnp.concatenate([k_before, kc_ref[pl.ds(r0, WINDOW), :]], axis=0)
        vband = jnp.concatenate([v_before, vc_ref[pl.ds(r0, WINDOW), :]], axis=0)
        problems.append((q_ref[pl.ds(r0, WINDOW), :].astype(F32), kband, vband, 0, (n - 1) * WINDOW))
        starts.append(r0)
    proj = []

    def conv_input_matmuls():
        xb = xb_ref[...]
        for w_ref in (wc_ref, wh_ref):
            proj.append(jnp.dot(xb, w_ref[...], preferred_element_type=F32))

    outs = _band_attention(problems, _sink_column(sink_ref), n_kv, group, head_dim,
                           between=conv_input_matmuls)
    cc, ch = proj
    cb = jnp.dot(xb_ref[...], wb_ref[...], preferred_element_type=F32)
    u = cc * ch
    row8 = lax.broadcasted_iota(jnp.int32, carry.shape, 0)

    def shifted(k):
        r = pltpu.roll(u, k, 0)
        head = jnp.where(row8 < k, pltpu.roll(carry, k, 0), r[0:SUBLANES])
        return jnp.concatenate([head, r[SUBLANES:]], axis=0)

    t = t_ref[...]
    last = i == pl.num_programs(0) - 1
    u_m1 = jnp.where(t >= 1, shifted(1), jnp.where(last, f1_ref[...], 0.0))
    u_m2 = jnp.where(t >= 2, shifted(2), jnp.where(last, f2_ref[...], 0.0))
    cw = cw_ref[...]
    conv = cw[0:1, :] * u_m2 + cw[1:2, :] * u_m1 + cw[2:3, :] * u

    for r0, out in zip(starts, outs):
        o_ref[pl.ds(r0, WINDOW), :] = out.astype(BF16)
    y_ref[...] = (cb * conv).astype(BF16)
    u_ref[...] = u
    ubuf[c] = u[tm - SUBLANES:, :]


def _conv_attn(xb, w_in, conv_w, t_idx, fill1, fill2, q, k, v, sink_tab, layer, *, tm, tc, col0, d_conv,
               seq, n_kv, group, head_dim):
    m, d = xb.shape
    d_q = q.shape[1]
    d_kv = k.shape[1]
    nc = d_conv // tc
    off = col0 // tc
    n_blocks = tm // WINDOW
    w_spec = lambda kk: pl.BlockSpec((d, tc), lambda i, c: (0, off + kk * nc + c))
    tile = lambda i, c: (i, 0)
    before = lambda i, c: (jnp.maximum(i * n_blocks - 1, 0), 0)
    return pl.pallas_call(
        functools.partial(_conv_attn_kernel, nc, seq // WINDOW, n_kv, group, head_dim),
        grid=(pl.cdiv(m, tm), nc),
        in_specs=[
            pl.BlockSpec((tm, d), tile),
            w_spec(0), w_spec(1), w_spec(2),
            pl.BlockSpec((None, CONV_W, tc), lambda i, c: (layer, 0, c)),
            pl.BlockSpec((tm, 1), tile),
            pl.BlockSpec((tm, tc), lambda i, c: (0, c)),
            pl.BlockSpec((tm, tc), lambda i, c: (0, c)),
            pl.BlockSpec((None, n_kv * group, 1, LANES), lambda i, c: (layer, 0, 0, 0)),
            pl.BlockSpec((tm, d_q), tile),
            pl.BlockSpec((WINDOW, d_kv), before),
            pl.BlockSpec((tm, d_kv), tile),
            pl.BlockSpec((WINDOW, d_kv), before),
            pl.BlockSpec((tm, d_kv), tile),
        ],
        out_specs=[
            pl.BlockSpec((tm, tc), lambda i, c: (i, c)),
            pl.BlockSpec((tm, tc), lambda i, c: (i, c)),
            pl.BlockSpec((tm, d_q), tile),
        ],
        out_shape=[
            jax.ShapeDtypeStruct((m, d_conv), BF16),
            jax.ShapeDtypeStruct((m, d_conv), F32),
            jax.ShapeDtypeStruct((m, d_q), BF16),
        ],
        scratch_shapes=[pltpu.VMEM((nc, SUBLANES, tc), F32)],
        compiler_params=_params(2),
        name="conv_attn",
    )(xb, w_in, w_in, w_in, conv_w, t_idx, fill1, fill2, sink_tab, q, k, k, v, v)


def _mix_ln_kernel(alpha, x_ref, xb_ref, a_ref, c_ref, wga_ref, wgc_ref, wa_ref, wc_ref, wo_ref, g_ref,
                   b_ref, o_ref, acc_ref):
    j = pl.program_id(1)

    @pl.when(j == 0)
    def _():
        acc_ref[...] = jnp.zeros_like(acc_ref)

    xb = xb_ref[...]
    ga = jnp.dot(xb, wga_ref[...], preferred_element_type=F32)
    gc = jnp.dot(xb, wgc_ref[...], preferred_element_type=F32)
    pa = jnp.dot(a_ref[...], wa_ref[...], preferred_element_type=F32)
    pc = jnp.dot(c_ref[...], wc_ref[...], preferred_element_type=F32)
    merged = (jax.nn.sigmoid(ga) * pa + jax.nn.sigmoid(gc) * pc).astype(BF16)
    acc_ref[...] += jnp.dot(merged, wo_ref[...], preferred_element_type=F32)

    @pl.when(j == pl.num_programs(1) - 1)
    def _():
        y = alpha * x_ref[...] + acc_ref[...]
        o_ref[...] = _layer_norm(y, g_ref[...], b_ref[...])


def _mix_ln(x, xb, attn, yconv, w_in, w_a, w_c, w_o, ln_g, ln_b, ln_idx, alpha, *, tm, tc, ga_col0,
            gc_col0):
    m, d = x.shape
    d_q = attn.shape[1]
    d_conv = yconv.shape[1]
    return pl.pallas_call(
        functools.partial(_mix_ln_kernel, alpha),
        grid=(m // tm, d // tc),
        in_specs=[
            pl.BlockSpec((tm, d), lambda i, j: (i, 0)),
            pl.BlockSpec((tm, d), lambda i, j: (i, 0)),
            pl.BlockSpec((tm, d_q), lambda i, j: (i, 0)),
            pl.BlockSpec((tm, d_conv), lambda i, j: (i, 0)),
            pl.BlockSpec((d, tc), lambda i, j: (0, ga_col0 // tc + j)),
            pl.BlockSpec((d, tc), lambda i, j: (0, gc_col0 // tc + j)),
            pl.BlockSpec((d_q, tc), lambda i, j: (0, j)),
            pl.BlockSpec((d_conv, tc), lambda i, j: (0, j)),
            pl.BlockSpec((tc, d), lambda i, j: (j, 0)),
            pl.BlockSpec((None, 1, d), lambda i, j: (ln_idx, 0, 0)),
            pl.BlockSpec((None, 1, d), lambda i, j: (ln_idx, 0, 0)),
        ],
        out_specs=pl.BlockSpec((tm, d), lambda i, j: (i, 0)),
        out_shape=jax.ShapeDtypeStruct((m, d), F32),
        scratch_shapes=[pltpu.VMEM((tm, d), F32)],
        compiler_params=_params(2),
        name="mix_ln",
    )(x, xb, attn, yconv, w_in, w_in, w_a, w_c, w_o, ln_g, ln_b)


def _rope_tables(pos, head_dim):
    inv_freq = ROPE_THETA ** (-jnp.arange(0, head_dim, 2, dtype=F32) / head_dim)
    ang = pos.astype(F32)[:, None] * inv_freq[None, :]
    cos = jnp.cos(ang)
    sin = jnp.sin(ang)
    reps = LANES // head_dim
    cos_tab = jnp.tile(jnp.concatenate([cos, cos], axis=1), (1, reps))
    sin_tab = jnp.tile(jnp.concatenate([-sin, sin], axis=1), (1, reps))
    return cos_tab, sin_tab


def _largest_divisor(n, cap, multiple_of=1):
    return max(k for k in range(multiple_of, cap + 1, multiple_of) if n % k == 0)


def _seq_tails(a, n_seq, seq, rows):
    return jnp.stack([a[(b + 1) * seq - rows:(b + 1) * seq] for b in range(n_seq)])


def kernel(x_prompt, x_sample, cache_k_win, cache_v_win, state_conv, ln_g, ln_b, w_in, sinks, conv_w,
           w_branch_attn, w_branch_conv, w_out, ffn1_gu, ffn1_down, ffn2_gu, ffn2_down):
    depth = w_in.shape[0]
    bp, tp, d = x_prompt.shape
    bs, ts, _ = x_sample.shape
    n_kv, head_dim = cache_k_win.shape[-2:]
    d_q = w_branch_attn.shape[1]
    d_conv = conv_w.shape[-1]
    d_kv = n_kv * head_dim
    group = d_q // d_kv
    mp, ms = bp * tp, bs * ts
    m = mp + ms
    alpha = (2.0 * depth) ** 0.25
    tm = _largest_divisor(m, MAX_ROW_TILE, BF16_ROWS)
    tm_wide = _largest_divisor(m, MAX_ROW_TILE_WIDE, BF16_ROWS)
    ns = _largest_divisor(bs, SAMPLE_SEQS_PER_STEP, SUBLANES // ts)
    assert tm >= ms and SUBLANES % ts == 0 and mp % (ns * ts) == 0 and tp % WINDOW == 0
    assert ts >= CONV_W - 1 and tp >= WINDOW
    tf = 512
    tc = 512
    tm_mix = MIX_ROW_TILE
    assert mp % tm_mix == 0 and ms <= tm_mix and tm_mix % (WINDOW * (d_conv // tc)) == 0
    conv_col0 = d_q + 2 * d_kv
    ga_col0 = conv_col0 + 3 * d_conv
    gc_col0 = ga_col0 + d

    ln_g = ln_g.reshape(depth * 3, 1, d)
    ln_b = ln_b.reshape(depth * 3, 1, d)
    sink_tab = jnp.broadcast_to(sinks[:, :, None, None], sinks.shape + (1, LANES))

    t_prompt = jnp.tile(jnp.arange(tp, dtype=jnp.int32), bp)
    t_sample = jnp.tile(jnp.arange(ts, dtype=jnp.int32), bs)
    cos_tab, sin_tab = _rope_tables(jnp.concatenate([t_prompt, PAST_LEN + t_sample]), head_dim)
    t_idx = jnp.concatenate([t_prompt, t_sample]).reshape(m, 1)
    k_cache = cache_k_win.reshape(depth, bs, WINDOW, d_kv)
    v_cache = cache_v_win.reshape(depth, bs, WINDOW, d_kv)

    f1_gu, f1_down = ffn1_gu[0].astype(BF16), ffn1_down[0].astype(BF16)
    x = x_prompt.reshape(mp, d)
    x_tail = x_sample.reshape(ms, d)
    ks_p, vs_p, cs_p, ks_s, vs_s, cs_s = [], [], [], [], [], []
    for l in range(depth):
        jobs = [(w, l) for w in (w_in, w_branch_attn, w_branch_conv, w_out, ffn2_gu, ffn2_down)]
        x1, x1b, w_in_b, w_a_b, w_c_b, w_o_b, f2_gu, f2_down = _ffn_ln(
            x, f1_gu, f1_down, ln_g, ln_b, 3 * l, alpha, tm=tm, tf=tf, emit_bf16=True, x_tail=x_tail,
            cast_jobs=jobs)
        q, k, v = _qkv_proj(x1b, w_in_b, cos_tab, sin_tab, tm=tm, d_q=d_q, d_kv=d_kv, head_dim=head_dim)
        st = state_conv[l]
        zeros = jnp.zeros((bs, ts, d_conv), F32)
        fill1 = zeros.at[:, 0].set(st[:, 1]).reshape(ms, d_conv)
        fill2 = zeros.at[:, 0].set(st[:, 0]).at[:, 1].set(st[:, 1]).reshape(ms, d_conv)
        fill1 = jnp.pad(fill1, ((0, tm_mix - ms), (0, 0)))
        fill2 = jnp.pad(fill2, ((0, tm_mix - ms), (0, 0)))
        yconv, u, attn = _conv_attn(x1b, w_in_b, conv_w, t_idx, fill1, fill2, q, k, v, sink_tab, l,
                                    tm=tm_mix, tc=tc, col0=conv_col0, d_conv=d_conv, seq=tp, n_kv=n_kv,
                                    group=group, head_dim=head_dim)
        attn = _attn_sample(attn, q, k, v, k_cache, v_cache, sink_tab, l, row0=mp, seq=ts,
                            seqs_per_step=ns, n_kv=n_kv, group=group, head_dim=head_dim)
        x2 = _mix_ln(x1, x1b, attn, yconv, w_in_b, w_a_b, w_c_b, w_o_b, ln_g, ln_b, 3 * l + 1, alpha,
                     tm=tm, tc=tc, ga_col0=ga_col0, gc_col0=gc_col0)
        if l + 1 < depth:
            x, f1_gu, f1_down = _ffn_ln(x2, f2_gu, f2_down, ln_g, ln_b, 3 * l + 2, alpha, tm=tm_wide, tf=tf,
                                        cast_jobs=[(ffn1_gu, l + 1), (ffn1_down, l + 1)])
            x_tail = None
        else:
            y_p, y_s = _ffn_ln(x2, f2_gu, f2_down, ln_g, ln_b, 3 * l + 2, alpha, tm=tm_wide, tf=tf,
                               split_rows=mp)

        ks_p.append(_seq_tails(k, bp, tp, WINDOW).reshape(bp, WINDOW, n_kv, head_dim))
        vs_p.append(_seq_tails(v, bp, tp, WINDOW).reshape(bp, WINDOW, n_kv, head_dim))
        cs_p.append(_seq_tails(u, bp, tp, CONV_W - 1))
        k_new = k[mp:].reshape(bs, ts, n_kv, head_dim)
        v_new = v[mp:].reshape(bs, ts, n_kv, head_dim)
        ks_s.append(jnp.concatenate([cache_k_win[l][:, ts:], k_new], axis=1))
        vs_s.append(jnp.concatenate([cache_v_win[l][:, ts:], v_new], axis=1))
        cs_s.append(u[mp:].reshape(bs, ts, d_conv)[:, -(CONV_W - 1):])

    return (y_p.reshape(bp, tp, d), y_s.reshape(bs, ts, d), jnp.stack(ks_p), jnp.stack(vs_p),
            jnp.stack(cs_p), jnp.stack(ks_s), jnp.stack(vs_s), jnp.stack(cs_s))
```

```python
import functools

import jax
import jax.numpy as jnp
from jax import lax
from jax.experimental import pallas as pl
from jax.experimental.pallas import tpu as pltpu

F32 = jnp.float32
BF16 = jnp.bfloat16

PAST_LEN = 16384
WINDOW = 128
ROPE_THETA = 10000.0
LN_EPS = 1e-5
CONV_W = 3

LANES = 128
SUBLANES = 8
BF16_ROWS = 16
VMEM_LIMIT_BYTES = 62 * 1024 * 1024

KEY_SLOTS = 2 * WINDOW
MAX_ROW_TILE = 640
MIX_ROW_TILE = 512
SAMPLE_SEQS_PER_STEP = 8


def _params(n_axes):
    return pltpu.CompilerParams(
        dimension_semantics=("arbitrary",) * n_axes,
        vmem_limit_bytes=VMEM_LIMIT_BYTES,
    )


def _layer_norm(y, g, b):
    mu = jnp.mean(y, axis=-1, keepdims=True)
    d = y - mu
    var = jnp.mean(d * d, axis=-1, keepdims=True)
    return d * lax.rsqrt(var + LN_EPS) * g + b


def _round_up(n, k):
    return -(-n // k) * k


def _ffn_ln_kernel(alpha, has_tail, emit_bf16, split, n_jobs, *refs):
    refs = list(refs)
    x_ref = refs.pop(0)
    xt_ref = refs.pop(0) if has_tail else None
    wg_ref, wu_ref, wd_ref, g_ref, b_ref = refs[:5]
    del refs[:5]
    job_src = refs[:n_jobs]
    del refs[:n_jobs]
    o_ref = refs.pop(0)
    ob_ref = refs.pop(0) if emit_bf16 else None
    ot_ref = refs.pop(0) if split else None
    job_dst = refs[:n_jobs]
    del refs[:n_jobs]
    xb_ref, acc_ref = refs
    i = pl.program_id(0)
    j = pl.program_id(1)
    last_tile = i == pl.num_programs(0) - 1
    tm = x_ref.shape[0]

    def load_x():
        x = x_ref[...]
        if xt_ref is None:
            return x
        merged = jnp.concatenate([x[:tm - xt_ref.shape[0]], xt_ref[...]], axis=0)
        return jnp.where(last_tile, merged, x)

    @pl.when(j == 0)
    def _():
        xb_ref[...] = load_x().astype(BF16)
        acc_ref[...] = jnp.zeros_like(acc_ref)

    xb = xb_ref[...]
    g = jnp.dot(xb, wg_ref[...], preferred_element_type=F32)
    u = jnp.dot(xb, wu_ref[...], preferred_element_type=F32)
    h = (jax.nn.silu(g) * u).astype(BF16)
    acc_ref[...] += jnp.dot(h, wd_ref[...], preferred_element_type=F32)
    for src, dst in zip(job_src, job_dst):
        dst[...] = src[...].astype(BF16)

    @pl.when(j == pl.num_programs(1) - 1)
    def _():
        y = alpha * load_x() + 0.5 * acc_ref[...]
        out = _layer_norm(y, g_ref[...], b_ref[...])
        o_ref[...] = out
        if emit_bf16:
            ob_ref[...] = out.astype(BF16)
        if split:
            @pl.when(last_tile)
            def _():
                ot_ref[...] = out[tm - ot_ref.shape[0]:, :]


def _ffn_ln(x, w_gu, w_down, ln_g, ln_b, ln_idx, alpha, *, tm, tf, emit_bf16=False, x_tail=None,
            split_rows=None, cast_jobs=()):
    d = x.shape[1]
    m = x.shape[0] + (0 if x_tail is None else x_tail.shape[0])
    f = w_down.shape[0]
    nj = f // tf
    n_tiles = m // tm
    row = lambda i, j: (i, 0)
    fixed = lambda i, j: (0, 0)
    in_specs = [pl.BlockSpec((tm, d), row)]
    args = [x]
    if x_tail is not None:
        in_specs.append(pl.BlockSpec(x_tail.shape, fixed))
        args.append(x_tail)
    in_specs += [
        pl.BlockSpec((d, tf), lambda i, j: (0, j)),
        pl.BlockSpec((d, tf), lambda i, j: (0, nj + j)),
        pl.BlockSpec((tf, d), lambda i, j: (j, 0)),
        pl.BlockSpec((None, 1, d), lambda i, j: (ln_idx, 0, 0)),
        pl.BlockSpec((None, 1, d), lambda i, j: (ln_idx, 0, 0)),
    ]
    args += [w_gu, w_gu, w_down, ln_g, ln_b]
    m_head = m if split_rows is None else split_rows
    out_shape = [jax.ShapeDtypeStruct((m_head, d), F32)]
    out_specs = [pl.BlockSpec((tm, d), row)]
    if emit_bf16:
        out_shape.append(jax.ShapeDtypeStruct((m, d), BF16))
        out_specs.append(pl.BlockSpec((tm, d), row))
    if split_rows is not None:
        out_shape.append(jax.ShapeDtypeStruct((m - split_rows, d), F32))
        out_specs.append(pl.BlockSpec((m - split_rows, d), fixed))
    n_steps = n_tiles * nj
    for w, layer in cast_jobs:
        _, r, c = w.shape
        rb = _round_up(-(-r // n_steps), BF16_ROWS)
        nb = -(-r // rb)
        block = lambda i, j, nb=nb: (jnp.minimum(i * nj + j, nb - 1), 0)
        in_specs.append(pl.BlockSpec((None, rb, c), lambda i, j, nb=nb, layer=layer:
                                     (layer, jnp.minimum(i * nj + j, nb - 1), 0)))
        args.append(w)
        out_shape.append(jax.ShapeDtypeStruct((r, c), BF16))
        out_specs.append(pl.BlockSpec((rb, c), block))
    return pl.pallas_call(
        functools.partial(_ffn_ln_kernel, alpha, x_tail is not None, emit_bf16, split_rows is not None,
                          len(cast_jobs)),
        grid=(n_tiles, nj),
        in_specs=in_specs,
        out_specs=out_specs,
        out_shape=out_shape,
        scratch_shapes=[pltpu.VMEM((tm, d), BF16), pltpu.VMEM((tm, d), F32)],
        compiler_params=_params(2),
        name="ffn_ln",
    )(*args)


def _rope_cols(x, cos, sin_signed, first_half, half):
    fwd = pltpu.roll(x, x.shape[1] - half, 1)
    bwd = pltpu.roll(x, half, 1)
    return x * cos + jnp.where(first_half, fwd, bwd) * sin_signed


def _qkv_kernel(d_q, d_kv, head_dim, xb_ref, w_ref, cos_ref, sin_ref, q_ref, k_ref, v_ref):
    xb = xb_ref[...]
    bounds = [0, d_q // 2, d_q, d_q + 2 * d_kv]
    parts = [jnp.dot(xb, w_ref[:, lo:hi], preferred_element_type=F32) for lo, hi in zip(bounds, bounds[1:])]
    q = jnp.concatenate(parts[:2], axis=1)
    kv = parts[2]
    cos = cos_ref[...]
    sin = sin_ref[...]
    half = head_dim // 2
    lane = lax.broadcasted_iota(jnp.int32, cos.shape, 1)
    first_half = (lane % head_dim) < half
    q_scale = head_dim ** -0.5
    for c in range(d_q // LANES):
        sl = slice(c * LANES, (c + 1) * LANES)
        q_ref[:, sl] = (_rope_cols(q[:, sl], cos, sin, first_half, half) * q_scale).astype(BF16)
    for c in range(d_kv // LANES):
        sl = slice(c * LANES, (c + 1) * LANES)
        k_ref[:, sl] = _rope_cols(kv[:, sl], cos, sin, first_half, half)
    v_ref[...] = kv[:, d_kv:]


def _qkv_proj(xb, w_in, cos_tab, sin_tab, *, tm, d_q, d_kv, head_dim):
    m, d = xb.shape
    n_cols = d_q + 2 * d_kv
    return pl.pallas_call(
        functools.partial(_qkv_kernel, d_q, d_kv, head_dim),
        grid=(m // tm,),
        in_specs=[
            pl.BlockSpec((tm, d), lambda i: (i, 0)),
            pl.BlockSpec((d, n_cols), lambda i: (0, 0)),
            pl.BlockSpec((tm, LANES), lambda i: (i, 0)),
            pl.BlockSpec((tm, LANES), lambda i: (i, 0)),
        ],
        out_specs=[
            pl.BlockSpec((tm, d_q), lambda i: (i, 0)),
            pl.BlockSpec((tm, d_kv), lambda i: (i, 0)),
            pl.BlockSpec((tm, d_kv), lambda i: (i, 0)),
        ],
        out_shape=[
            jax.ShapeDtypeStruct((m, d_q), BF16),
            jax.ShapeDtypeStruct((m, d_kv), F32),
            jax.ShapeDtypeStruct((m, d_kv), F32),
        ],
        compiler_params=_params(1),
        name="qkv_proj",
    )(xb, w_in, cos_tab, sin_tab)


def _lane_tile4(x128, want_high):
    lane = lax.broadcasted_iota(jnp.int32, x128.shape, 1)
    swapped = pltpu.roll(x128, LANES // 2, 1)
    low = lane < LANES // 2
    both = jnp.where(low, swapped, x128) if want_high else jnp.where(low, x128, swapped)
    return jnp.concatenate([both, both], axis=1)


def _band_attention(problems, sink, n_kv, group, head_dim, between=None):
    rows = problems[0][0].shape[0]
    gw = group * head_dim
    r_idx = lax.broadcasted_iota(jnp.int32, (rows, KEY_SLOTS), 0)
    s_idx = lax.broadcasted_iota(jnp.int32, (rows, KEY_SLOTS), 1)
    head_of_lane = lax.broadcasted_iota(jnp.int32, (rows, gw), 1) // head_dim
    head_keep = [(head_of_lane == g).astype(F32) for g in range(group)]

    scores, values = [], []
    for q, kband, vband, row0, kpos0 in problems:
        diff = r_idx - row0 + WINDOW - s_idx
        valid = (diff >= 0) & (diff <= WINDOW) & (s_idx + kpos0 >= 0)
        for kh in range(n_kv):
            col = (kh * head_dim) // LANES
            high = ((kh * head_dim) % LANES) != 0
            kk = _lane_tile4(kband[:, col * LANES:(col + 1) * LANES], high).astype(BF16)
            values.append(_lane_tile4(vband[:, col * LANES:(col + 1) * LANES], high).astype(BF16))
            qg = q[:, kh * gw:(kh + 1) * gw]
            qs = jnp.concatenate([qg * head_keep[g] for g in range(group)], axis=0).astype(BF16)
            s = lax.dot_general(qs, kk, (((1,), (1,)), ((), ())), preferred_element_type=F32)
            scores.append(jnp.where(valid[None], s.reshape(group, rows, KEY_SLOTS), -jnp.inf))
    s = jnp.concatenate(scores, axis=0)
    sink_all = jnp.concatenate([sink] * len(problems), axis=0)
    mx = jnp.maximum(jnp.max(s, axis=-1, keepdims=True), sink_all)
    p = jnp.exp(s - mx)
    den = jnp.sum(p, axis=-1, keepdims=True) + jnp.exp(sink_all - mx)
    p = p * (1.0 / den)

    if between is not None:
        between()

    outs = []
    for i in range(len(problems)):
        slabs = []
        for kh in range(n_kv):
            c = i * n_kv + kh
            pg = p[c * group:(c + 1) * group].reshape(group * rows, KEY_SLOTS).astype(BF16)
            o = jnp.dot(pg, values[c], preferred_element_type=F32).reshape(group, rows, gw)
            out = o[0]
            for g in range(1, group):
                out = jnp.where(head_of_lane == g, o[g], out)
            slabs.append(out)
        outs.append(jnp.concatenate(slabs, axis=1))
    return outs


def _sink_column(sink_ref):
    return sink_ref[...][:, :, 0:1]


def _attn_sample_kernel(seq, n_kv, group, head_dim, sink_ref, q_ref, kc_ref, kn_ref, vc_ref, vn_ref,
                        _, o_ref):
    n_seq = kc_ref.shape[0]
    d_kv = kn_ref.shape[1]
    per_block = SUBLANES // seq
    pad = jnp.zeros((KEY_SLOTS - WINDOW - SUBLANES, d_kv), F32)
    qf = q_ref[...].astype(F32)
    problems = []
    for i in range(n_seq):
        blk, row0 = i // per_block, (i % per_block) * seq
        rows = slice(blk * SUBLANES, (blk + 1) * SUBLANES)
        k8, v8 = kn_ref[rows, :], vn_ref[rows, :]
        if row0:
            k8 = pltpu.roll(k8, SUBLANES - row0, 0)
            v8 = pltpu.roll(v8, SUBLANES - row0, 0)
        kband = jnp.concatenate([kc_ref[i], k8, pad], axis=0)
        vband = jnp.concatenate([vc_ref[i], v8, pad], axis=0)
        problems.append((qf[rows, :], kband, vband, row0, PAST_LEN - WINDOW))
    outs = _band_attention(problems, _sink_column(sink_ref), n_kv, group, head_dim)
    r_idx = lax.broadcasted_iota(jnp.int32, outs[0].shape, 0)
    blocks = []
    for blk in range(n_seq // per_block):
        out = outs[blk * per_block]
        for s in range(1, per_block):
            out = jnp.where(r_idx >= s * seq, outs[blk * per_block + s], out)
        blocks.append(out)
    o_ref[...] = jnp.concatenate(blocks, axis=0).astype(BF16)


def _attn_sample(attn, q, k, v, k_cache, v_cache, sink_tab, layer, *, row0, seq, seqs_per_step, n_kv,
                 group, head_dim):
    d_q = q.shape[1]
    d_kv = k.shape[1]
    nbatch = k_cache.shape[1]
    ns = seqs_per_step
    rows = ns * seq
    blk0 = row0 // rows
    new = lambda b: (blk0 + b, 0)
    cache = lambda b: (layer, b, 0, 0)
    return pl.pallas_call(
        functools.partial(_attn_sample_kernel, seq, n_kv, group, head_dim),
        grid=(nbatch // ns,),
        in_specs=[
            pl.BlockSpec((None, n_kv * group, 1, LANES), lambda b: (layer, 0, 0, 0)),
            pl.BlockSpec((rows, d_q), new),
            pl.BlockSpec((None, ns, WINDOW, d_kv), cache),
            pl.BlockSpec((rows, d_kv), new),
            pl.BlockSpec((None, ns, WINDOW, d_kv), cache),
            pl.BlockSpec((rows, d_kv), new),
            pl.BlockSpec(memory_space=pl.ANY),
        ],
        out_specs=pl.BlockSpec((rows, d_q), new),
        out_shape=jax.ShapeDtypeStruct(attn.shape, attn.dtype),
        input_output_aliases={6: 0},
        compiler_params=_params(1),
        name="attn_sample",
    )(sink_tab, q, k_cache, k, v_cache, v, attn)


def _conv_attn_kernel(nc, blocks_per_seq, n_kv, group, head_dim,
                      xb_ref, wb_ref, wc_ref, wh_ref, cw_ref, t_ref, f1_ref, f2_ref,
                      sink_ref, q_ref, kp_ref, kc_ref, vp_ref, vc_ref,
                      y_ref, u_ref, o_ref, ubuf):
    tm = xb_ref.shape[0]
    i = pl.program_id(0)
    c = pl.program_id(1)

    @pl.when(i == 0)
    def _():
        ubuf[c] = jnp.zeros(ubuf.shape[1:], F32)

    carry = ubuf[c]

    n_blocks = tm // WINDOW
    per_step = n_blocks // nc
    problems, starts = [], []
    for s in range(per_step):
        b = c * per_step + s
        r0 = pl.multiple_of(b * WINDOW, WINDOW)
        rp = pl.multiple_of(jnp.maximum(b - 1, 0) * WINDOW, WINDOW)
        n = lax.rem(i * n_blocks + b, blocks_per_seq)
        k_before = jnp.where(b == 0, kp_ref[...], kc_ref[pl.ds(rp, WINDOW), :])
        v_before = jnp.where(b == 0, vp_ref[...], vc_ref[pl.ds(rp, WINDOW), :])
        kband = jnp.concatenate([k_before, kc_ref[pl.ds(r0, WINDOW), :]], axis=0)
        vband = jnp.concatenate([v_before, vc_ref[pl.ds(r0, WINDOW), :]], axis=0)
        problems.append((q_ref[pl.ds(r0, WINDOW), :].astype(F32), kband, vband, 0, (n - 1) * WINDOW))
        starts.append(r0)
    proj = []

    def conv_input_matmuls():
        xb = xb_ref[...]
        for w_ref in (wc_ref, wh_ref):
            proj.append(jnp.dot(xb, w_ref[...], preferred_element_type=F32))

    outs = _band_attention(problems, _sink_column(sink_ref), n_kv, group, head_dim,
                           between=conv_input_matmuls)
    cc, ch = proj
    cb = jnp.dot(xb_ref[...], wb_ref[...], preferred_element_type=F32)
    u = cc * ch
    row8 = lax.broadcasted_iota(jnp.int32, carry.shape, 0)

    def shifted(k):
        r = pltpu.roll(u, k, 0)
        head = jnp.where(row8 < k, pltpu.roll(carry, k, 0), r[0:SUBLANES])
        return jnp.concatenate([head, r[SUBLANES:]], axis=0)

    t = t_ref[...]
    last = i == pl.num_programs(0) - 1
    u_m1 = jnp.where(t >= 1, shifted(1), jnp.where(last, f1_ref[...], 0.0))
    u_m2 = jnp.where(t >= 2, shifted(2), jnp.where(last, f2_ref[...], 0.0))
    cw = cw_ref[...]
    conv = cw[0:1, :] * u_m2 + cw[1:2, :] * u_m1 + cw[2:3, :] * u

    for r0, out in zip(starts, outs):
        o_ref[pl.ds(r0, WINDOW), :] = out.astype(BF16)
    y_ref[...] = (cb * conv).astype(BF16)
    u_ref[...] = u
    ubuf[c] = u[tm - SUBLANES:, :]


def _conv_attn(xb, w_in, conv_w, t_idx, fill1, fill2, q, k, v, sink_tab, layer, *, tm, tc, col0, d_conv,
               seq, n_kv, group, head_dim):
    m, d = xb.shape
    d_q = q.shape[1]
    d_kv = k.shape[1]
    nc = d_conv // tc
    off = col0 // tc
    n_blocks = tm // WINDOW
    w_spec = lambda kk: pl.BlockSpec((d, tc), lambda i, c: (0, off + kk * nc + c))
    tile = lambda i, c: (i, 0)
    before = lambda i, c: (jnp.maximum(i * n_blocks - 1, 0), 0)
    return pl.pallas_call(
        functools.partial(_conv_attn_kernel, nc, seq // WINDOW, n_kv, group, head_dim),
        grid=(pl.cdiv(m, tm), nc),
        in_specs=[
            pl.BlockSpec((tm, d), tile),
            w_spec(0), w_spec(1), w_spec(2),
            pl.BlockSpec((None, CONV_W, tc), lambda i, c: (layer, 0, c)),
            pl.BlockSpec((tm, 1), tile),
            pl.BlockSpec((tm, tc), lambda i, c: (0, c)),
            pl.BlockSpec((tm, tc), lambda i, c: (0, c)),
            pl.BlockSpec((None, n_kv * group, 1, LANES), lambda i, c: (layer, 0, 0, 0)),
            pl.BlockSpec((tm, d_q), tile),
            pl.BlockSpec((WINDOW, d_kv), before),
            pl.BlockSpec((tm, d_kv), tile),
            pl.BlockSpec((WINDOW, d_kv), before),
            pl.BlockSpec((tm, d_kv), tile),
        ],
        out_specs=[
            pl.BlockSpec((tm, tc), lambda i, c: (i, c)),
            pl.BlockSpec((tm, tc), lambda i, c: (i, c)),
            pl.BlockSpec((tm, d_q), tile),
        ],
        out_shape=[
            jax.ShapeDtypeStruct((m, d_conv), BF16),
            jax.ShapeDtypeStruct((m, d_conv), F32),
            jax.ShapeDtypeStruct((m, d_q), BF16),
        ],
        scratch_shapes=[pltpu.VMEM((nc, SUBLANES, tc), F32)],
        compiler_params=_params(2),
        name="conv_attn",
    )(xb, w_in, w_in, w_in, conv_w, t_idx, fill1, fill2, sink_tab, q, k, k, v, v)


def _mix_ln_kernel(alpha, x_ref, xb_ref, a_ref, c_ref, wga_ref, wgc_ref, wa_ref, wc_ref, wo_ref, g_ref,
                   b_ref, o_ref, acc_ref):
    j = pl.program_id(1)

    @pl.when(j == 0)
    def _():
        acc_ref[...] = jnp.zeros_like(acc_ref)

    xb = xb_ref[...]
    ga = jnp.dot(xb, wga_ref[...], preferred_element_type=F32)
    gc = jnp.dot(xb, wgc_ref[...], preferred_element_type=F32)
    pa = jnp.dot(a_ref[...], wa_ref[...], preferred_element_type=F32)
    pc = jnp.dot(c_ref[...], wc_ref[...], preferred_element_type=F32)
    merged = (jax.nn.sigmoid(ga) * pa + jax.nn.sigmoid(gc) * pc).astype(BF16)
    acc_ref[...] += jnp.dot(merged, wo_ref[...], preferred_element_type=F32)

    @pl.when(j == pl.num_programs(1) - 1)
    def _():
        y = alpha * x_ref[...] + acc_ref[...]
        o_ref[...] = _layer_norm(y, g_ref[...], b_ref[...])


def _mix_ln(x, xb, attn, yconv, w_in, w_a, w_c, w_o, ln_g, ln_b, ln_idx, alpha, *, tm, tc, ga_col0,
            gc_col0):
    m, d = x.shape
    d_q = attn.shape[1]
    d_conv = yconv.shape[1]
    return pl.pallas_call(
        functools.partial(_mix_ln_kernel, alpha),
        grid=(m // tm, d // tc),
        in_specs=[
            pl.BlockSpec((tm, d), lambda i, j: (i, 0)),
            pl.BlockSpec((tm, d), lambda i, j: (i, 0)),
            pl.BlockSpec((tm, d_q), lambda i, j: (i, 0)),
            pl.BlockSpec((tm, d_conv), lambda i, j: (i, 0)),
            pl.BlockSpec((d, tc), lambda i, j: (0, ga_col0 // tc + j)),
            pl.BlockSpec((d, tc), lambda i, j: (0, gc_col0 // tc + j)),
            pl.BlockSpec((d_q, tc), lambda i, j: (0, j)),
            pl.BlockSpec((d_conv, tc), lambda i, j: (0, j)),
            pl.BlockSpec((tc, d), lambda i, j: (j, 0)),
            pl.BlockSpec((None, 1, d), lambda i, j: (ln_idx, 0, 0)),
            pl.BlockSpec((None, 1, d), lambda i, j: (ln_idx, 0, 0)),
        ],
        out_specs=pl.BlockSpec((tm, d), lambda i, j: (i, 0)),
        out_shape=jax.ShapeDtypeStruct((m, d), F32),
        scratch_shapes=[pltpu.VMEM((tm, d), F32)],
        compiler_params=_params(2),
        name="mix_ln",
    )(x, xb, attn, yconv, w_in, w_in, w_a, w_c, w_o, ln_g, ln_b)


def _rope_tables(pos, head_dim):
    inv_freq = ROPE_THETA ** (-jnp.arange(0, head_dim, 2, dtype=F32) / head_dim)
    ang = pos.astype(F32)[:, None] * inv_freq[None, :]
    cos = jnp.cos(ang)
    sin = jnp.sin(ang)
    reps = LANES // head_dim
    cos_tab = jnp.tile(jnp.concatenate([cos, cos], axis=1), (1, reps))
    sin_tab = jnp.tile(jnp.concatenate([-sin, sin], axis=1), (1, reps))
    return cos_tab, sin_tab


def _largest_divisor(n, cap, multiple_of=1):
    return max(k for k in range(multiple_of, cap + 1, multiple_of) if n % k == 0)


def _seq_tails(a, n_seq, seq, rows):
    return jnp.stack([a[(b + 1) * seq - rows:(b + 1) * seq] for b in range(n_seq)])


def kernel(x_prompt, x_sample, cache_k_win, cache_v_win, state_conv, ln_g, ln_b, w_in, sinks, conv_w,
           w_branch_attn, w_branch_conv, w_out, ffn1_gu, ffn1_down, ffn2_gu, ffn2_down):
    depth = w_in.shape[0]
    bp, tp, d = x_prompt.shape
    bs, ts, _ = x_sample.shape
    n_kv, head_dim = cache_k_win.shape[-2:]
    d_q = w_branch_attn.shape[1]
    d_conv = conv_w.shape[-1]
    d_kv = n_kv * head_dim
    group = d_q // d_kv
    mp, ms = bp * tp, bs * ts
    m = mp + ms
    alpha = (2.0 * depth) ** 0.25
    tm = _largest_divisor(m, MAX_ROW_TILE, BF16_ROWS)
    ns = _largest_divisor(bs, SAMPLE_SEQS_PER_STEP, SUBLANES // ts)
    assert tm >= ms and SUBLANES % ts == 0 and mp % (ns * ts) == 0 and tp % WINDOW == 0
    assert ts >= CONV_W - 1 and tp >= WINDOW
    tf = 512
    tc = 512
    tm_mix = MIX_ROW_TILE
    assert mp % tm_mix == 0 and ms <= tm_mix and tm_mix % (WINDOW * (d_conv // tc)) == 0
    conv_col0 = d_q + 2 * d_kv
    ga_col0 = conv_col0 + 3 * d_conv
    gc_col0 = ga_col0 + d

    ln_g = ln_g.reshape(depth * 3, 1, d)
    ln_b = ln_b.reshape(depth * 3, 1, d)
    sink_tab = jnp.broadcast_to(sinks[:, :, None, None], sinks.shape + (1, LANES))

    t_prompt = jnp.tile(jnp.arange(tp, dtype=jnp.int32), bp)
    t_sample = jnp.tile(jnp.arange(ts, dtype=jnp.int32), bs)
    cos_tab, sin_tab = _rope_tables(jnp.concatenate([t_prompt, PAST_LEN + t_sample]), head_dim)
    t_idx = jnp.concatenate([t_prompt, t_sample]).reshape(m, 1)
    k_cache = cache_k_win.reshape(depth, bs, WINDOW, d_kv)
    v_cache = cache_v_win.reshape(depth, bs, WINDOW, d_kv)

    f1_gu, f1_down = ffn1_gu[0].astype(BF16), ffn1_down[0].astype(BF16)
    x = x_prompt.reshape(mp, d)
    x_tail = x_sample.reshape(ms, d)
    ks_p, vs_p, cs_p, ks_s, vs_s, cs_s = [], [], [], [], [], []
    for l in range(depth):
        jobs = [(w, l) for w in (w_in, w_branch_attn, w_branch_conv, w_out, ffn2_gu, ffn2_down)]
        x1, x1b, w_in_b, w_a_b, w_c_b, w_o_b, f2_gu, f2_down = _ffn_ln(
            x, f1_gu, f1_down, ln_g, ln_b, 3 * l, alpha, tm=tm, tf=tf, emit_bf16=True, x_tail=x_tail,
            cast_jobs=jobs)
        q, k, v = _qkv_proj(x1b, w_in_b, cos_tab, sin_tab, tm=tm, d_q=d_q, d_kv=d_kv, head_dim=head_dim)
        st = state_conv[l]
        zeros = jnp.zeros((bs, ts, d_conv), F32)
        fill1 = zeros.at[:, 0].set(st[:, 1]).reshape(ms, d_conv)
        fill2 = zeros.at[:, 0].set(st[:, 0]).at[:, 1].set(st[:, 1]).reshape(ms, d_conv)
        fill1 = jnp.pad(fill1, ((0, tm_mix - ms), (0, 0)))
        fill2 = jnp.pad(fill2, ((0, tm_mix - ms), (0, 0)))
        yconv, u, attn = _conv_attn(x1b, w_in_b, conv_w, t_idx, fill1, fill2, q, k, v, sink_tab, l,
                                    tm=tm_mix, tc=tc, col0=conv_col0, d_conv=d_conv, seq=tp, n_kv=n_kv,
                                    group=group, head_dim=head_dim)
        attn = _attn_sample(attn, q, k, v, k_cache, v_cache, sink_tab, l, row0=mp, seq=ts,
                            seqs_per_step=ns, n_kv=n_kv, group=group, head_dim=head_dim)
        x2 = _mix_ln(x1, x1b, attn, yconv, w_in_b, w_a_b, w_c_b, w_o_b, ln_g, ln_b, 3 * l + 1, alpha,
                     tm=tm, tc=tc, ga_col0=ga_col0, gc_col0=gc_col0)
        if l + 1 < depth:
            x, f1_gu, f1_down = _ffn_ln(x2, f2_gu, f2_down, ln_g, ln_b, 3 * l + 2, alpha, tm=tm, tf=tf,
                                        cast_jobs=[(ffn1_gu, l + 1), (ffn1_down, l + 1)])
            x_tail = None
        else:
            y_p, y_s = _ffn_ln(x2, f2_gu, f2_down, ln_g, ln_b, 3 * l + 2, alpha, tm=tm, tf=tf,
                               split_rows=mp)

        ks_p.append(_seq_tails(k, bp, tp, WINDOW).reshape(bp, WINDOW, n_kv, head_dim))
        vs_p.append(_seq_tails(v, bp, tp, WINDOW).reshape(bp, WINDOW, n_kv, head_dim))
        cs_p.append(_seq_tails(u, bp, tp, CONV_W - 1))
        k_new = k[mp:].reshape(bs, ts, n_kv, head_dim)
        v_new = v[mp:].reshape(bs, ts, n_kv, head_dim)
        ks_s.append(jnp.concatenate([cache_k_win[l][:, ts:], k_new], axis=1))
        vs_s.append(jnp.concatenate([cache_v_win[l][:, ts:], v_new], axis=1))
        cs_s.append(u[mp:].reshape(bs, ts, d_conv)[:, -(CONV_W - 1):])

    return (y_p.reshape(bp, tp, d), y_s.reshape(bs, ts, d), jnp.stack(ks_p), jnp.stack(vs_p),
            jnp.stack(cs_p), jnp.stack(ks_s), jnp.stack(vs_s), jnp.stack(cs_s))
```

```python
import functools

import jax
import jax.numpy as jnp
from jax import lax
from jax.experimental import pallas as pl
from jax.experimental.pallas import tpu as pltpu

F32 = jnp.float32
BF16 = jnp.bfloat16

PAST_LEN = 16384
WINDOW = 128
ROPE_THETA = 10000.0
LN_EPS = 1e-5
CONV_W = 3

LANES = 128
SUBLANES = 8
BF16_ROWS = 16
VMEM_LIMIT_BYTES = 62 * 1024 * 1024

KEY_SLOTS = 2 * WINDOW
MAX_ROW_TILE = 640
MIX_ROW_TILE = 512
SAMPLE_SEQS_PER_STEP = 8


def _params(n_axes):
    return pltpu.CompilerParams(
        dimension_semantics=("arbitrary",) * n_axes,
        vmem_limit_bytes=VMEM_LIMIT_BYTES,
    )


def _layer_norm(y, g, b):
    mu = jnp.mean(y, axis=-1, keepdims=True)
    d = y - mu
    var = jnp.mean(d * d, axis=-1, keepdims=True)
    return d * lax.rsqrt(var + LN_EPS) * g + b


def _round_up(n, k):
    return -(-n // k) * k


def _ffn_ln_kernel(alpha, has_tail, emit_bf16, split, n_jobs, *refs):
    refs = list(refs)
    x_ref = refs.pop(0)
    xt_ref = refs.pop(0) if has_tail else None
    wg_ref, wu_ref, wd_ref, g_ref, b_ref = refs[:5]
    del refs[:5]
    job_src = refs[:n_jobs]
    del refs[:n_jobs]
    o_ref = refs.pop(0)
    ob_ref = refs.pop(0) if emit_bf16 else None
    ot_ref = refs.pop(0) if split else None
    job_dst = refs[:n_jobs]
    del refs[:n_jobs]
    xb_ref, acc_ref = refs
    i = pl.program_id(0)
    j = pl.program_id(1)
    last_tile = i == pl.num_programs(0) - 1
    tm = x_ref.shape[0]

    def load_x():
        x = x_ref[...]
        if xt_ref is None:
            return x
        merged = jnp.concatenate([x[:tm - xt_ref.shape[0]], xt_ref[...]], axis=0)
        return jnp.where(last_tile, merged, x)

    @pl.when(j == 0)
    def _():
        xb_ref[...] = load_x().astype(BF16)
        acc_ref[...] = jnp.zeros_like(acc_ref)

    xb = xb_ref[...]
    g = jnp.dot(xb, wg_ref[...], preferred_element_type=F32)
    u = jnp.dot(xb, wu_ref[...], preferred_element_type=F32)
    h = (jax.nn.silu(g) * u).astype(BF16)
    acc_ref[...] += jnp.dot(h, wd_ref[...], preferred_element_type=F32)
    for src, dst in zip(job_src, job_dst):
        dst[...] = src[...].astype(BF16)

    @pl.when(j == pl.num_programs(1) - 1)
    def _():
        y = alpha * load_x() + 0.5 * acc_ref[...]
        out = _layer_norm(y, g_ref[...], b_ref[...])
        o_ref[...] = out
        if emit_bf16:
            ob_ref[...] = out.astype(BF16)
        if split:
            @pl.when(last_tile)
            def _():
                ot_ref[...] = out[tm - ot_ref.shape[0]:, :]


def _ffn_ln(x, w_gu, w_down, ln_g, ln_b, ln_idx, alpha, *, tm, tf, emit_bf16=False, x_tail=None,
            split_rows=None, cast_jobs=()):
    d = x.shape[1]
    m = x.shape[0] + (0 if x_tail is None else x_tail.shape[0])
    f = w_down.shape[0]
    nj = f // tf
    n_tiles = m // tm
    row = lambda i, j: (i, 0)
    fixed = lambda i, j: (0, 0)
    in_specs = [pl.BlockSpec((tm, d), row)]
    args = [x]
    if x_tail is not None:
        in_specs.append(pl.BlockSpec(x_tail.shape, fixed))
        args.append(x_tail)
    in_specs += [
        pl.BlockSpec((d, tf), lambda i, j: (0, j)),
        pl.BlockSpec((d, tf), lambda i, j: (0, nj + j)),
        pl.BlockSpec((tf, d), lambda i, j: (j, 0)),
        pl.BlockSpec((None, 1, d), lambda i, j: (ln_idx, 0, 0)),
        pl.BlockSpec((None, 1, d), lambda i, j: (ln_idx, 0, 0)),
    ]
    args += [w_gu, w_gu, w_down, ln_g, ln_b]
    m_head = m if split_rows is None else split_rows
    out_shape = [jax.ShapeDtypeStruct((m_head, d), F32)]
    out_specs = [pl.BlockSpec((tm, d), row)]
    if emit_bf16:
        out_shape.append(jax.ShapeDtypeStruct((m, d), BF16))
        out_specs.append(pl.BlockSpec((tm, d), row))
    if split_rows is not None:
        out_shape.append(jax.ShapeDtypeStruct((m - split_rows, d), F32))
        out_specs.append(pl.BlockSpec((m - split_rows, d), fixed))
    n_steps = n_tiles * nj
    for w, layer in cast_jobs:
        _, r, c = w.shape
        rb = _round_up(-(-r // n_steps), BF16_ROWS)
        nb = -(-r // rb)
        block = lambda i, j, nb=nb: (jnp.minimum(i * nj + j, nb - 1), 0)
        in_specs.append(pl.BlockSpec((None, rb, c), lambda i, j, nb=nb, layer=layer:
                                     (layer, jnp.minimum(i * nj + j, nb - 1), 0)))
        args.append(w)
        out_shape.append(jax.ShapeDtypeStruct((r, c), BF16))
        out_specs.append(pl.BlockSpec((rb, c), block))
    return pl.pallas_call(
        functools.partial(_ffn_ln_kernel, alpha, x_tail is not None, emit_bf16, split_rows is not None,
                          len(cast_jobs)),
        grid=(n_tiles, nj),
        in_specs=in_specs,
        out_specs=out_specs,
        out_shape=out_shape,
        scratch_shapes=[pltpu.VMEM((tm, d), BF16), pltpu.VMEM((tm, d), F32)],
        compiler_params=_params(2),
        name="ffn_ln",
    )(*args)


def _rope_cols(x, cos, sin_signed, first_half, half):
    fwd = pltpu.roll(x, x.shape[1] - half, 1)
    bwd = pltpu.roll(x, half, 1)
    return x * cos + jnp.where(first_half, fwd, bwd) * sin_signed


def _qkv_kernel(d_q, d_kv, head_dim, xb_ref, w_ref, cos_ref, sin_ref, q_ref, k_ref, v_ref):
    xb = xb_ref[...]
    bounds = [0, d_q // 2, d_q, d_q + 2 * d_kv]
    parts = [jnp.dot(xb, w_ref[:, lo:hi], preferred_element_type=F32) for lo, hi in zip(bounds, bounds[1:])]
    q = jnp.concatenate(parts[:2], axis=1)
    kv = parts[2]
    cos = cos_ref[...]
    sin = sin_ref[...]
    half = head_dim // 2
    lane = lax.broadcasted_iota(jnp.int32, cos.shape, 1)
    first_half = (lane % head_dim) < half
    q_scale = head_dim ** -0.5
    for c in range(d_q // LANES):
        sl = slice(c * LANES, (c + 1) * LANES)
        q_ref[:, sl] = (_rope_cols(q[:, sl], cos, sin, first_half, half) * q_scale).astype(BF16)
    for c in range(d_kv // LANES):
        sl = slice(c * LANES, (c + 1) * LANES)
        k_ref[:, sl] = _rope_cols(kv[:, sl], cos, sin, first_half, half)
    v_ref[...] = kv[:, d_kv:]


def _qkv_proj(xb, w_in, cos_tab, sin_tab, *, tm, d_q, d_kv, head_dim):
    m, d = xb.shape
    n_cols = d_q + 2 * d_kv
    return pl.pallas_call(
        functools.partial(_qkv_kernel, d_q, d_kv, head_dim),
        grid=(m // tm,),
        in_specs=[
            pl.BlockSpec((tm, d), lambda i: (i, 0)),
            pl.BlockSpec((d, n_cols), lambda i: (0, 0)),
            pl.BlockSpec((tm, LANES), lambda i: (i, 0)),
            pl.BlockSpec((tm, LANES), lambda i: (i, 0)),
        ],
        out_specs=[
            pl.BlockSpec((tm, d_q), lambda i: (i, 0)),
            pl.BlockSpec((tm, d_kv), lambda i: (i, 0)),
            pl.BlockSpec((tm, d_kv), lambda i: (i, 0)),
        ],
        out_shape=[
            jax.ShapeDtypeStruct((m, d_q), BF16),
            jax.ShapeDtypeStruct((m, d_kv), F32),
            jax.ShapeDtypeStruct((m, d_kv), F32),
        ],
        compiler_params=_params(1),
        name="qkv_proj",
    )(xb, w_in, cos_tab, sin_tab)


def _lane_tile4(x128, want_high):
    lane = lax.broadcasted_iota(jnp.int32, x128.shape, 1)
    swapped = pltpu.roll(x128, LANES // 2, 1)
    low = lane < LANES // 2
    both = jnp.where(low, swapped, x128) if want_high else jnp.where(low, x128, swapped)
    return jnp.concatenate([both, both], axis=1)


def _band_attention(problems, sink, n_kv, group, head_dim, between=None):
    rows = problems[0][0].shape[0]
    gw = group * head_dim
    r_idx = lax.broadcasted_iota(jnp.int32, (rows, KEY_SLOTS), 0)
    s_idx = lax.broadcasted_iota(jnp.int32, (rows, KEY_SLOTS), 1)
    head_of_lane = lax.broadcasted_iota(jnp.int32, (rows, gw), 1) // head_dim
    head_keep = [(head_of_lane == g).astype(F32) for g in range(group)]

    scores, values = [], []
    for q, kband, vband, row0, kpos0 in problems:
        diff = r_idx - row0 + WINDOW - s_idx
        valid = (diff >= 0) & (diff <= WINDOW) & (s_idx + kpos0 >= 0)
        for kh in range(n_kv):
            col = (kh * head_dim) // LANES
            high = ((kh * head_dim) % LANES) != 0
            kk = _lane_tile4(kband[:, col * LANES:(col + 1) * LANES], high).astype(BF16)
            values.append(_lane_tile4(vband[:, col * LANES:(col + 1) * LANES], high).astype(BF16))
            qg = q[:, kh * gw:(kh + 1) * gw]
            qs = jnp.concatenate([qg * head_keep[g] for g in range(group)], axis=0).astype(BF16)
            s = lax.dot_general(qs, kk, (((1,), (1,)), ((), ())), preferred_element_type=F32)
            scores.append(jnp.where(valid[None], s.reshape(group, rows, KEY_SLOTS), -jnp.inf))
    s = jnp.concatenate(scores, axis=0)
    sink_all = jnp.concatenate([sink] * len(problems), axis=0)
    mx = jnp.maximum(jnp.max(s, axis=-1, keepdims=True), sink_all)
    p = jnp.exp(s - mx)
    den = jnp.sum(p, axis=-1, keepdims=True) + jnp.exp(sink_all - mx)
    p = p * (1.0 / den)

    if between is not None:
        between()

    outs = []
    for i in range(len(problems)):
        slabs = []
        for kh in range(n_kv):
            c = i * n_kv + kh
            pg = p[c * group:(c + 1) * group].reshape(group * rows, KEY_SLOTS).astype(BF16)
            o = jnp.dot(pg, values[c], preferred_element_type=F32).reshape(group, rows, gw)
            out = o[0]
            for g in range(1, group):
                out = jnp.where(head_of_lane == g, o[g], out)
            slabs.append(out)
        outs.append(jnp.concatenate(slabs, axis=1))
    return outs


def _sink_column(sink_ref):
    return sink_ref[...][:, :, 0:1]


def _attn_sample_kernel(seq, n_kv, group, head_dim, sink_ref, q_ref, kc_ref, kn_ref, vc_ref, vn_ref,
                        _, o_ref):
    n_seq = kc_ref.shape[0]
    d_kv = kn_ref.shape[1]
    per_block = SUBLANES // seq
    pad = jnp.zeros((KEY_SLOTS - WINDOW - SUBLANES, d_kv), F32)
    qf = q_ref[...].astype(F32)
    problems = []
    for i in range(n_seq):
        blk, row0 = i // per_block, (i % per_block) * seq
        rows = slice(blk * SUBLANES, (blk + 1) * SUBLANES)
        k8, v8 = kn_ref[rows, :], vn_ref[rows, :]
        if row0:
            k8 = pltpu.roll(k8, SUBLANES - row0, 0)
            v8 = pltpu.roll(v8, SUBLANES - row0, 0)
        kband = jnp.concatenate([kc_ref[i], k8, pad], axis=0)
        vband = jnp.concatenate([vc_ref[i], v8, pad], axis=0)
        problems.append((qf[rows, :], kband, vband, row0, PAST_LEN - WINDOW))
    outs = _band_attention(problems, _sink_column(sink_ref), n_kv, group, head_dim)
    r_idx = lax.broadcasted_iota(jnp.int32, outs[0].shape, 0)
    blocks = []
    for blk in range(n_seq // per_block):
        out = outs[blk * per_block]
        for s in range(1, per_block):
            out = jnp.where(r_idx >= s * seq, outs[blk * per_block + s], out)
        blocks.append(out)
    o_ref[...] = jnp.concatenate(blocks, axis=0).astype(BF16)


def _attn_sample(attn, q, k, v, k_cache, v_cache, sink_tab, layer, *, row0, seq, seqs_per_step, n_kv,
                 group, head_dim):
    d_q = q.shape[1]
    d_kv = k.shape[1]
    nbatch = k_cache.shape[1]
    ns = seqs_per_step
    rows = ns * seq
    blk0 = row0 // rows
    new = lambda b: (blk0 + b, 0)
    cache = lambda b: (layer, b, 0, 0)
    return pl.pallas_call(
        functools.partial(_attn_sample_kernel, seq, n_kv, group, head_dim),
        grid=(nbatch // ns,),
        in_specs=[
            pl.BlockSpec((None, n_kv * group, 1, LANES), lambda b: (layer, 0, 0, 0)),
            pl.BlockSpec((rows, d_q), new),
            pl.BlockSpec((None, ns, WINDOW, d_kv), cache),
            pl.BlockSpec((rows, d_kv), new),
            pl.BlockSpec((None, ns, WINDOW, d_kv), cache),
            pl.BlockSpec((rows, d_kv), new),
            pl.BlockSpec(memory_space=pl.ANY),
        ],
        out_specs=pl.BlockSpec((rows, d_q), new),
        out_shape=jax.ShapeDtypeStruct(attn.shape, attn.dtype),
        input_output_aliases={6: 0},
        compiler_params=_params(1),
        name="attn_sample",
    )(sink_tab, q, k_cache, k, v_cache, v, attn)


def _conv_attn_kernel(nc, blocks_per_seq, n_kv, group, head_dim,
                      xb_ref, wb_ref, wc_ref, wh_ref, cw_ref, t_ref, f1_ref, f2_ref,
                      sink_ref, q_ref, kp_ref, kc_ref, vp_ref, vc_ref,
                      y_ref, u_ref, o_ref, ubuf):
    tm = xb_ref.shape[0]
    i = pl.program_id(0)
    c = pl.program_id(1)

    @pl.when(i == 0)
    def _():
        ubuf[c] = jnp.zeros(ubuf.shape[1:], F32)

    carry = ubuf[c]

    n_blocks = tm // WINDOW
    per_step = n_blocks // nc
    problems, starts = [], []
    for s in range(per_step):
        b = c * per_step + s
        r0 = pl.multiple_of(b * WINDOW, WINDOW)
        rp = pl.multiple_of(jnp.maximum(b - 1, 0) * WINDOW, WINDOW)
        n = lax.rem(i * n_blocks + b, blocks_per_seq)
        k_before = jnp.where(b == 0, kp_ref[...], kc_ref[pl.ds(rp, WINDOW), :])
        v_before = jnp.where(b == 0, vp_ref[...], vc_ref[pl.ds(rp, WINDOW), :])
        kband = jnp.concatenate([k_before, kc_ref[pl.ds(r0, WINDOW), :]], axis=0)
        vband = jnp.concatenate([v_before, vc_ref[pl.ds(r0, WINDOW), :]], axis=0)
        problems.append((q_ref[pl.ds(r0, WINDOW), :].astype(F32), kband, vband, 0, (n - 1) * WINDOW))
        starts.append(r0)
    proj = []

    def conv_input_matmuls():
        xb = xb_ref[...]
        for w_ref in (wc_ref, wh_ref):
            proj.append(jnp.dot(xb, w_ref[...], preferred_element_type=F32))

    outs = _band_attention(problems, _sink_column(sink_ref), n_kv, group, head_dim,
                           between=conv_input_matmuls)
    cc, ch = proj
    cb = jnp.dot(xb_ref[...], wb_ref[...], preferred_element_type=F32)
    u = cc * ch
    row8 = lax.broadcasted_iota(jnp.int32, carry.shape, 0)

    def shifted(k):
        r = pltpu.roll(u, k, 0)
        head = jnp.where(row8 < k, pltpu.roll(carry, k, 0), r[0:SUBLANES])
        return jnp.concatenate([head, r[SUBLANES:]], axis=0)

    t = t_ref[...]
    last = i == pl.num_programs(0) - 1
    u_m1 = jnp.where(t >= 1, shifted(1), jnp.where(last, f1_ref[...], 0.0))
    u_m2 = jnp.where(t >= 2, shifted(2), jnp.where(last, f2_ref[...], 0.0))
    cw = cw_ref[...]
    conv = cw[0:1, :] * u_m2 + cw[1:2, :] * u_m1 + cw[2:3, :] * u

    for r0, out in zip(starts, outs):
        o_ref[pl.ds(r0, WINDOW), :] = out.astype(BF16)
    y_ref[...] = (cb * conv).astype(BF16)
    u_ref[...] = u
    ubuf[c] = u[tm - SUBLANES:, :]


def _conv_attn(xb, w_in, conv_w, t_idx, fill1, fill2, q, k, v, sink_tab, layer, *, tm, tc, col0, d_conv,
               seq, n_kv, group, head_dim):
    m, d = xb.shape
    d_q = q.shape[1]
    d_kv = k.shape[1]
    nc = d_conv // tc
    off = col0 // tc
    n_blocks = tm // WINDOW
    w_spec = lambda kk: pl.BlockSpec((d, tc), lambda i, c: (0, off + kk * nc + c))
    tile = lambda i, c: (i, 0)
    before = lambda i, c: (jnp.maximum(i * n_blocks - 1, 0), 0)
    return pl.pallas_call(
        functools.partial(_conv_attn_kernel, nc, seq // WINDOW, n_kv, group, head_dim),
        grid=(pl.cdiv(m, tm), nc),
        in_specs=[
            pl.BlockSpec((tm, d), tile),
            w_spec(0), w_spec(1), w_spec(2),
            pl.BlockSpec((None, CONV_W, tc), lambda i, c: (layer, 0, c)),
            pl.BlockSpec((tm, 1), tile),
            pl.BlockSpec((tm, tc), lambda i, c: (0, c)),
            pl.BlockSpec((tm, tc), lambda i, c: (0, c)),
            pl.BlockSpec((None, n_kv * group, 1, LANES), lambda i, c: (layer, 0, 0, 0)),
            pl.BlockSpec((tm, d_q), tile),
            pl.BlockSpec((WINDOW, d_kv), before),
            pl.BlockSpec((tm, d_kv), tile),
            pl.BlockSpec((WINDOW, d_kv), before),
            pl.BlockSpec((tm, d_kv), tile),
        ],
        out_specs=[
            pl.BlockSpec((tm, tc), lambda i, c: (i, c)),
            pl.BlockSpec((tm, tc), lambda i, c: (i, c)),
            pl.BlockSpec((tm, d_q), tile),
        ],
        out_shape=[
            jax.ShapeDtypeStruct((m, d_conv), BF16),
            jax.ShapeDtypeStruct((m, d_conv), F32),
            jax.ShapeDtypeStruct((m, d_q), BF16),
        ],
        scratch_shapes=[pltpu.VMEM((nc, SUBLANES, tc), F32)],
        compiler_params=_params(2),
        name="conv_attn",
    )(xb, w_in, w_in, w_in, conv_w, t_idx, fill1, fill2, sink_tab, q, k, k, v, v)


def _mixer_front_kernel(nc, blocks_per_seq, n_kv, group, head_dim,
                        xb_ref, wq_ref, wk_ref, wv_ref, cos_ref, sin_ref,
                        wb_ref, wc_ref, wh_ref, cw_ref, t_ref, f1_ref, f2_ref, sink_ref,
                        q_ref, k_ref, v_ref, y_ref, u_ref, o_ref,
                        ubuf, q_s, k_s, v_s, k_tail, v_tail):
    tm = xb_ref.shape[0]
    i = pl.program_id(0)
    c = pl.program_id(1)
    cur = lax.rem(i, 2)
    prev = 1 - cur

    @pl.when(i == 0)
    def _():
        ubuf[c] = jnp.zeros(ubuf.shape[1:], F32)

    @pl.when((i == 0) & (c == 0))
    def _():
        q_s[prev] = jnp.zeros(q_s.shape[1:], BF16)
        k_s[prev] = jnp.zeros(k_s.shape[1:], F32)
        v_s[prev] = jnp.zeros(v_s.shape[1:], F32)
        k_tail[prev] = jnp.zeros(k_tail.shape[1:], F32)
        v_tail[prev] = jnp.zeros(v_tail.shape[1:], F32)

    carry = ubuf[c]

    n_blocks = tm // WINDOW
    per_step = n_blocks // nc
    k_prev_tile = jnp.concatenate([k_s[prev, h] for h in range(nc)], axis=1)
    v_prev_tile = jnp.concatenate([v_s[prev, h] for h in range(nc)], axis=1)
    k_before_tile = jnp.concatenate([k_tail[prev, h] for h in range(nc)], axis=1)
    v_before_tile = jnp.concatenate([v_tail[prev, h] for h in range(nc)], axis=1)
    problems, starts = [], []
    for s in range(per_step):
        b = c * per_step + s
        r0 = pl.multiple_of(b * WINDOW, WINDOW)
        rp = pl.multiple_of(jnp.maximum(b - 1, 0) * WINDOW, WINDOW)
        n = lax.rem(jnp.maximum(i - 1, 0) * n_blocks + b, blocks_per_seq)
        qb = jnp.concatenate([q_s[prev, h, pl.ds(r0, WINDOW), :] for h in range(nc)], axis=1)
        k_cur = jnp.concatenate([k_s[prev, h, pl.ds(r0, WINDOW), :] for h in range(nc)], axis=1)
        v_cur = jnp.concatenate([v_s[prev, h, pl.ds(r0, WINDOW), :] for h in range(nc)], axis=1)
        k_bef = jnp.concatenate([k_s[prev, h, pl.ds(rp, WINDOW), :] for h in range(nc)], axis=1)
        v_bef = jnp.concatenate([v_s[prev, h, pl.ds(rp, WINDOW), :] for h in range(nc)], axis=1)
        k_bef = jnp.where(b == 0, k_before_tile, k_bef)
        v_bef = jnp.where(b == 0, v_before_tile, v_bef)
        problems.append((qb.astype(F32), jnp.concatenate([k_bef, k_cur], axis=0),
                         jnp.concatenate([v_bef, v_cur], axis=0), 0, (n - 1) * WINDOW))
        starts.append(r0)

    proj = []

    def front_matmuls():
        xb = xb_ref[...]
        w_qkv = jnp.concatenate([wq_ref[...], wk_ref[...], wv_ref[...]], axis=1)
        proj.append(jnp.dot(xb, w_qkv, preferred_element_type=F32))
        for w_ref in (wc_ref, wh_ref):
            proj.append(jnp.dot(xb, w_ref[...], preferred_element_type=F32))

    outs = _band_attention(problems, _sink_column(sink_ref), n_kv, group, head_dim, between=front_matmuls)
    qkv, cc, ch = proj
    cb = jnp.dot(xb_ref[...], wb_ref[...], preferred_element_type=F32)

    cos = cos_ref[...]
    sin = sin_ref[...]
    half = head_dim // 2
    lane = lax.broadcasted_iota(jnp.int32, cos.shape, 1)
    first_half = (lane % head_dim) < half
    wq = wq_ref.shape[1]
    q_new = jnp.concatenate(
        [(_rope_cols(qkv[:, s * LANES:(s + 1) * LANES], cos, sin, first_half, half) * head_dim ** -0.5)
         for s in range(wq // LANES)], axis=1).astype(BF16)
    k_new = _rope_cols(qkv[:, wq:wq + LANES], cos, sin, first_half, half)
    v_new = qkv[:, wq + LANES:]

    u = cc * ch
    row8 = lax.broadcasted_iota(jnp.int32, carry.shape, 0)

    def shifted(k):
        r = pltpu.roll(u, k, 0)
        head = jnp.where(row8 < k, pltpu.roll(carry, k, 0), r[0:SUBLANES])
        return jnp.concatenate([head, r[SUBLANES:]], axis=0)

    t = t_ref[...]
    last = i == pl.num_programs(0) - 1
    u_m1 = jnp.where(t >= 1, shifted(1), jnp.where(last, f1_ref[...], 0.0))
    u_m2 = jnp.where(t >= 2, shifted(2), jnp.where(last, f2_ref[...], 0.0))
    cw = cw_ref[...]
    conv = cw[0:1, :] * u_m2 + cw[1:2, :] * u_m1 + cw[2:3, :] * u

    k_tail_new = k_s[prev, c, tm - WINDOW:, :]
    v_tail_new = v_s[prev, c, tm - WINDOW:, :]
    for r0, out in zip(starts, outs):
        o_ref[pl.ds(r0, WINDOW), :] = out.astype(BF16)
    q_ref[...] = q_new
    k_ref[...] = k_new
    v_ref[...] = v_new
    q_s[cur, c] = q_new
    k_s[cur, c] = k_new
    v_s[cur, c] = v_new
    k_tail[cur, c] = k_tail_new
    v_tail[cur, c] = v_tail_new
    y_ref[...] = (cb * conv).astype(BF16)
    u_ref[...] = u
    ubuf[c] = u[tm - SUBLANES:, :]


def _mixer_front(xb, w_in, conv_w, cos_tab, sin_tab, t_idx, fill1, fill2, sink_tab, layer, *, tm, tc,
                 d_q, d_kv, d_conv, seq, n_kv, group, head_dim):
    m, d = xb.shape
    nc = d_conv // tc
    wq, wkv = d_q // nc, d_kv // nc
    assert wkv == LANES and tm % (WINDOW * nc) == 0
    conv0 = (d_q + 2 * d_kv) // tc
    tile = lambda i, c: (i, 0)
    part = lambda i, c: (i, c)
    w_spec = lambda kk: pl.BlockSpec((d, tc), lambda i, c: (0, conv0 + kk * nc + c))
    return pl.pallas_call(
        functools.partial(_mixer_front_kernel, nc, seq // WINDOW, n_kv, group, head_dim),
        grid=(pl.cdiv(m, tm), nc),
        in_specs=[
            pl.BlockSpec((tm, d), tile),
            pl.BlockSpec((d, wq), lambda i, c: (0, c)),
            pl.BlockSpec((d, wkv), lambda i, c: (0, d_q // wkv + c)),
            pl.BlockSpec((d, wkv), lambda i, c: (0, (d_q + d_kv) // wkv + c)),
            pl.BlockSpec((tm, LANES), tile),
            pl.BlockSpec((tm, LANES), tile),
            w_spec(0), w_spec(1), w_spec(2),
            pl.BlockSpec((None, CONV_W, tc), lambda i, c: (layer, 0, c)),
            pl.BlockSpec((tm, 1), tile),
            pl.BlockSpec((tm, tc), lambda i, c: (0, c)),
            pl.BlockSpec((tm, tc), lambda i, c: (0, c)),
            pl.BlockSpec((None, n_kv * group, 1, LANES), lambda i, c: (layer, 0, 0, 0)),
        ],
        out_specs=[
            pl.BlockSpec((tm, wq), part),
            pl.BlockSpec((tm, wkv), part),
            pl.BlockSpec((tm, wkv), part),
            pl.BlockSpec((tm, tc), part),
            pl.BlockSpec((tm, tc), part),
            pl.BlockSpec((tm, d_q), lambda i, c: (jnp.maximum(i - 1, 0), 0)),
        ],
        out_shape=[
            jax.ShapeDtypeStruct((m, d_q), BF16),
            jax.ShapeDtypeStruct((m, d_kv), F32),
            jax.ShapeDtypeStruct((m, d_kv), F32),
            jax.ShapeDtypeStruct((m, d_conv), BF16),
            jax.ShapeDtypeStruct((m, d_conv), F32),
            jax.ShapeDtypeStruct((m, d_q), BF16),
        ],
        scratch_shapes=[
            pltpu.VMEM((nc, SUBLANES, tc), F32),
            pltpu.VMEM((2, nc, tm, wq), BF16),
            pltpu.VMEM((2, nc, tm, wkv), F32),
            pltpu.VMEM((2, nc, tm, wkv), F32),
            pltpu.VMEM((2, nc, WINDOW, wkv), F32),
            pltpu.VMEM((2, nc, WINDOW, wkv), F32),
        ],
        compiler_params=_params(2),
        name="mixer_front",
    )(xb, w_in, w_in, w_in, cos_tab, sin_tab, w_in, w_in, w_in, conv_w, t_idx, fill1, fill2, sink_tab)


def _mix_ln_kernel(alpha, x_ref, xb_ref, a_ref, c_ref, wga_ref, wgc_ref, wa_ref, wc_ref, wo_ref, g_ref,
                   b_ref, o_ref, acc_ref):
    j = pl.program_id(1)

    @pl.when(j == 0)
    def _():
        acc_ref[...] = jnp.zeros_like(acc_ref)

    xb = xb_ref[...]
    ga = jnp.dot(xb, wga_ref[...], preferred_element_type=F32)
    gc = jnp.dot(xb, wgc_ref[...], preferred_element_type=F32)
    pa = jnp.dot(a_ref[...], wa_ref[...], preferred_element_type=F32)
    pc = jnp.dot(c_ref[...], wc_ref[...], preferred_element_type=F32)
    merged = (jax.nn.sigmoid(ga) * pa + jax.nn.sigmoid(gc) * pc).astype(BF16)
    acc_ref[...] += jnp.dot(merged, wo_ref[...], preferred_element_type=F32)

    @pl.when(j == pl.num_programs(1) - 1)
    def _():
        y = alpha * x_ref[...] + acc_ref[...]
        o_ref[...] = _layer_norm(y, g_ref[...], b_ref[...])


def _mix_ln(x, xb, attn, yconv, w_in, w_a, w_c, w_o, ln_g, ln_b, ln_idx, alpha, *, tm, tc, ga_col0,
            gc_col0):
    m, d = x.shape
    d_q = attn.shape[1]
    d_conv = yconv.shape[1]
    return pl.pallas_call(
        functools.partial(_mix_ln_kernel, alpha),
        grid=(m // tm, d // tc),
        in_specs=[
            pl.BlockSpec((tm, d), lambda i, j: (i, 0)),
            pl.BlockSpec((tm, d), lambda i, j: (i, 0)),
            pl.BlockSpec((tm, d_q), lambda i, j: (i, 0)),
            pl.BlockSpec((tm, d_conv), lambda i, j: (i, 0)),
            pl.BlockSpec((d, tc), lambda i, j: (0, ga_col0 // tc + j)),
            pl.BlockSpec((d, tc), lambda i, j: (0, gc_col0 // tc + j)),
            pl.BlockSpec((d_q, tc), lambda i, j: (0, j)),
            pl.BlockSpec((d_conv, tc), lambda i, j: (0, j)),
            pl.BlockSpec((tc, d), lambda i, j: (j, 0)),
            pl.BlockSpec((None, 1, d), lambda i, j: (ln_idx, 0, 0)),
            pl.BlockSpec((None, 1, d), lambda i, j: (ln_idx, 0, 0)),
        ],
        out_specs=pl.BlockSpec((tm, d), lambda i, j: (i, 0)),
        out_shape=jax.ShapeDtypeStruct((m, d), F32),
        scratch_shapes=[pltpu.VMEM((tm, d), F32)],
        compiler_params=_params(2),
        name="mix_ln",
    )(x, xb, attn, yconv, w_in, w_in, w_a, w_c, w_o, ln_g, ln_b)


def _rope_tables(pos, head_dim):
    inv_freq = ROPE_THETA ** (-jnp.arange(0, head_dim, 2, dtype=F32) / head_dim)
    ang = pos.astype(F32)[:, None] * inv_freq[None, :]
    cos = jnp.cos(ang)
    sin = jnp.sin(ang)
    reps = LANES // head_dim
    cos_tab = jnp.tile(jnp.concatenate([cos, cos], axis=1), (1, reps))
    sin_tab = jnp.tile(jnp.concatenate([-sin, sin], axis=1), (1, reps))
    return cos_tab, sin_tab


def _largest_divisor(n, cap, multiple_of=1):
    return max(k for k in range(multiple_of, cap + 1, multiple_of) if n % k == 0)


def _seq_tails(a, n_seq, seq, rows):
    return jnp.stack([a[(b + 1) * seq - rows:(b + 1) * seq] for b in range(n_seq)])


def kernel(x_prompt, x_sample, cache_k_win, cache_v_win, state_conv, ln_g, ln_b, w_in, sinks, conv_w,
           w_branch_attn, w_branch_conv, w_out, ffn1_gu, ffn1_down, ffn2_gu, ffn2_down):
    depth = w_in.shape[0]
    bp, tp, d = x_prompt.shape
    bs, ts, _ = x_sample.shape
    n_kv, head_dim = cache_k_win.shape[-2:]
    d_q = w_branch_attn.shape[1]
    d_conv = conv_w.shape[-1]
    d_kv = n_kv * head_dim
    group = d_q // d_kv
    mp, ms = bp * tp, bs * ts
    m = mp + ms
    alpha = (2.0 * depth) ** 0.25
    tm = _largest_divisor(m, MAX_ROW_TILE, BF16_ROWS)
    ns = _largest_divisor(bs, SAMPLE_SEQS_PER_STEP, SUBLANES // ts)
    assert tm >= ms and SUBLANES % ts == 0 and mp % (ns * ts) == 0 and tp % WINDOW == 0
    assert ts >= CONV_W - 1 and tp >= WINDOW
    tf = 512
    tc = 512
    tm_mix = MIX_ROW_TILE
    assert mp % tm_mix == 0 and ms <= tm_mix and tm_mix % (WINDOW * (d_conv // tc)) == 0
    conv_col0 = d_q + 2 * d_kv
    ga_col0 = conv_col0 + 3 * d_conv
    gc_col0 = ga_col0 + d

    ln_g = ln_g.reshape(depth * 3, 1, d)
    ln_b = ln_b.reshape(depth * 3, 1, d)
    sink_tab = jnp.broadcast_to(sinks[:, :, None, None], sinks.shape + (1, LANES))

    t_prompt = jnp.tile(jnp.arange(tp, dtype=jnp.int32), bp)
    t_sample = jnp.tile(jnp.arange(ts, dtype=jnp.int32), bs)
    cos_tab, sin_tab = _rope_tables(jnp.concatenate([t_prompt, PAST_LEN + t_sample]), head_dim)
    t_idx = jnp.concatenate([t_prompt, t_sample]).reshape(m, 1)
    k_cache = cache_k_win.reshape(depth, bs, WINDOW, d_kv)
    v_cache = cache_v_win.reshape(depth, bs, WINDOW, d_kv)

    f1_gu, f1_down = ffn1_gu[0].astype(BF16), ffn1_down[0].astype(BF16)
    x = x_prompt.reshape(mp, d)
    x_tail = x_sample.reshape(ms, d)
    ks_p, vs_p, cs_p, ks_s, vs_s, cs_s = [], [], [], [], [], []
    for l in range(depth):
        jobs = [(w, l) for w in (w_in, w_branch_attn, w_branch_conv, w_out, ffn2_gu, ffn2_down)]
        x1, x1b, w_in_b, w_a_b, w_c_b, w_o_b, f2_gu, f2_down = _ffn_ln(
            x, f1_gu, f1_down, ln_g, ln_b, 3 * l, alpha, tm=tm, tf=tf, emit_bf16=True, x_tail=x_tail,
            cast_jobs=jobs)
        st = state_conv[l]
        zeros = jnp.zeros((bs, ts, d_conv), F32)
        fill1 = zeros.at[:, 0].set(st[:, 1]).reshape(ms, d_conv)
        fill2 = zeros.at[:, 0].set(st[:, 0]).at[:, 1].set(st[:, 1]).reshape(ms, d_conv)
        fill1 = jnp.pad(fill1, ((0, tm_mix - ms), (0, 0)))
        fill2 = jnp.pad(fill2, ((0, tm_mix - ms), (0, 0)))
        q, k, v, yconv, u, attn = _mixer_front(
            x1b, w_in_b, conv_w, cos_tab, sin_tab, t_idx, fill1, fill2, sink_tab, l, tm=tm_mix, tc=tc,
            d_q=d_q, d_kv=d_kv, d_conv=d_conv, seq=tp, n_kv=n_kv, group=group, head_dim=head_dim)
        attn = _attn_sample(attn, q, k, v, k_cache, v_cache, sink_tab, l, row0=mp, seq=ts,
                            seqs_per_step=ns, n_kv=n_kv, group=group, head_dim=head_dim)
        x2 = _mix_ln(x1, x1b, attn, yconv, w_in_b, w_a_b, w_c_b, w_o_b, ln_g, ln_b, 3 * l + 1, alpha,
                     tm=tm, tc=tc, ga_col0=ga_col0, gc_col0=gc_col0)
        if l + 1 < depth:
            x, f1_gu, f1_down = _ffn_ln(x2, f2_gu, f2_down, ln_g, ln_b, 3 * l + 2, alpha, tm=tm, tf=tf,
                                        cast_jobs=[(ffn1_gu, l + 1), (ffn1_down, l + 1)])
            x_tail = None
        else:
            y_p, y_s = _ffn_ln(x2, f2_gu, f2_down, ln_g, ln_b, 3 * l + 2, alpha, tm=tm, tf=tf,
                               split_rows=mp)

        ks_p.append(_seq_tails(k, bp, tp, WINDOW).reshape(bp, WINDOW, n_kv, head_dim))
        vs_p.append(_seq_tails(v, bp, tp, WINDOW).reshape(bp, WINDOW, n_kv, head_dim))
        cs_p.append(_seq_tails(u, bp, tp, CONV_W - 1))
        k_new = k[mp:].reshape(bs, ts, n_kv, head_dim)
        v_new = v[mp:].reshape(bs, ts, n_kv, head_dim)
        ks_s.append(jnp.concatenate([cache_k_win[l][:, ts:], k_new], axis=1))
        vs_s.append(jnp.concatenate([cache_v_win[l][:, ts:], v_new], axis=1))
        cs_s.append(u[mp:].reshape(bs, ts, d_conv)[:, -(CONV_W - 1):])

    return (y_p.reshape(bp, tp, d), y_s.reshape(bs, ts, d), jnp.stack(ks_p), jnp.stack(vs_p),
            jnp.stack(cs_p), jnp.stack(ks_s), jnp.stack(vs_s), jnp.stack(cs_s))
```

```python
import functools

import jax
import jax.numpy as jnp
from jax import lax
from jax.experimental import pallas as pl
from jax.experimental.pallas import tpu as pltpu

F32 = jnp.float32
BF16 = jnp.bfloat16

PAST_LEN = 16384
WINDOW = 128
ROPE_THETA = 10000.0
LN_EPS = 1e-5
CONV_W = 3

LANES = 128
SUBLANES = 8
BF16_ROWS = 16
VMEM_LIMIT_BYTES = 62 * 1024 * 1024

KEY_SLOTS = 2 * WINDOW
MAX_ROW_TILE = 640
WEIGHT_SLOTS = 3
MIX_ROW_TILE = 512
SAMPLE_SEQS_PER_STEP = 8


def _params(n_axes):
    return pltpu.CompilerParams(
        dimension_semantics=("arbitrary",) * n_axes,
        vmem_limit_bytes=VMEM_LIMIT_BYTES,
    )


def _layer_norm(y, g, b):
    mu = jnp.mean(y, axis=-1, keepdims=True)
    d = y - mu
    var = jnp.mean(d * d, axis=-1, keepdims=True)
    return d * lax.rsqrt(var + LN_EPS) * g + b


def _round_up(n, k):
    return -(-n // k) * k


def _ffn_ln_kernel(alpha, has_tail, emit_bf16, split, n_jobs, *refs):
    refs = list(refs)
    x_ref = refs.pop(0)
    xt_ref = refs.pop(0) if has_tail else None
    wg_ref, wu_ref, wd_ref, g_ref, b_ref = refs[:5]
    del refs[:5]
    job_src = refs[:n_jobs]
    del refs[:n_jobs]
    o_ref = refs.pop(0)
    ob_ref = refs.pop(0) if emit_bf16 else None
    ot_ref = refs.pop(0) if split else None
    job_dst = refs[:n_jobs]
    del refs[:n_jobs]
    xb_ref, wg_buf, wu_buf, wd_buf, sems = refs
    acc_ref = o_ref
    i = pl.program_id(0)
    j = pl.program_id(1)
    nj = pl.num_programs(1)
    last_tile = i == pl.num_programs(0) - 1
    tm = x_ref.shape[0]
    tf = wg_buf.shape[2]

    step = i * nj + j
    n_steps = pl.num_programs(0) * nj

    def chunk_copies(chunk, slot):
        return (
            pltpu.make_async_copy(wg_ref.at[:, pl.ds(chunk * tf, tf)], wg_buf.at[slot], sems.at[slot, 0]),
            pltpu.make_async_copy(wu_ref.at[:, pl.ds((nj + chunk) * tf, tf)], wu_buf.at[slot],
                                  sems.at[slot, 1]),
            pltpu.make_async_copy(wd_ref.at[pl.ds(chunk * tf, tf), :], wd_buf.at[slot], sems.at[slot, 2]),
        )

    def start_step(t):
        for cp in chunk_copies(lax.rem(t, nj), lax.rem(t, WEIGHT_SLOTS)):
            cp.start()

    @pl.when(step == 0)
    def _():
        for t in range(WEIGHT_SLOTS - 1):
            start_step(jnp.int32(t))

    ahead = step + (WEIGHT_SLOTS - 1)

    @pl.when(ahead < n_steps)
    def _():
        start_step(ahead)

    slot = lax.rem(step, WEIGHT_SLOTS)
    for cp in chunk_copies(j, slot):
        cp.wait()

    def load_x():
        x = x_ref[...]
        if xt_ref is None:
            return x
        merged = jnp.concatenate([x[:tm - xt_ref.shape[0]], xt_ref[...]], axis=0)
        return jnp.where(last_tile, merged, x)

    @pl.when(j == 0)
    def _():
        xb_ref[...] = load_x().astype(BF16)
        acc_ref[...] = jnp.zeros_like(acc_ref)

    xb = xb_ref[...]
    g = jnp.dot(xb, wg_buf[slot], preferred_element_type=F32)
    u = jnp.dot(xb, wu_buf[slot], preferred_element_type=F32)
    h = (jax.nn.silu(g) * u).astype(BF16)
    acc_ref[...] += jnp.dot(h, wd_buf[slot], preferred_element_type=F32)
    for src, dst in zip(job_src, job_dst):
        dst[...] = src[...].astype(BF16)

    @pl.when(j == pl.num_programs(1) - 1)
    def _():
        y = alpha * load_x() + 0.5 * acc_ref[...]
        out = _layer_norm(y, g_ref[...], b_ref[...])
        o_ref[...] = out
        if emit_bf16:
            ob_ref[...] = out.astype(BF16)
        if split:
            @pl.when(last_tile)
            def _():
                ot_ref[...] = out[tm - ot_ref.shape[0]:, :]


def _ffn_ln(x, w_gu, w_down, ln_g, ln_b, ln_idx, alpha, *, tm, tf, emit_bf16=False, x_tail=None,
            split_rows=None, cast_jobs=()):
    d = x.shape[1]
    m = x.shape[0] + (0 if x_tail is None else x_tail.shape[0])
    f = w_down.shape[0]
    nj = f // tf
    n_tiles = m // tm
    row = lambda i, j: (i, 0)
    fixed = lambda i, j: (0, 0)
    in_specs = [pl.BlockSpec((tm, d), row)]
    args = [x]
    if x_tail is not None:
        in_specs.append(pl.BlockSpec(x_tail.shape, fixed))
        args.append(x_tail)
    in_specs += [
        pl.BlockSpec(memory_space=pl.ANY),
        pl.BlockSpec(memory_space=pl.ANY),
        pl.BlockSpec(memory_space=pl.ANY),
        pl.BlockSpec((None, 1, d), lambda i, j: (ln_idx, 0, 0)),
        pl.BlockSpec((None, 1, d), lambda i, j: (ln_idx, 0, 0)),
    ]
    args += [w_gu, w_gu, w_down, ln_g, ln_b]
    m_head = m if split_rows is None else split_rows
    out_shape = [jax.ShapeDtypeStruct((m_head, d), F32)]
    out_specs = [pl.BlockSpec((tm, d), row)]
    if emit_bf16:
        out_shape.append(jax.ShapeDtypeStruct((m, d), BF16))
        out_specs.append(pl.BlockSpec((tm, d), row))
    if split_rows is not None:
        out_shape.append(jax.ShapeDtypeStruct((m - split_rows, d), F32))
        out_specs.append(pl.BlockSpec((m - split_rows, d), fixed))
    n_steps = n_tiles * nj
    for w, layer in cast_jobs:
        _, r, c = w.shape
        rb = _round_up(-(-r // n_steps), BF16_ROWS)
        nb = -(-r // rb)
        block = lambda i, j, nb=nb: (jnp.minimum(i * nj + j, nb - 1), 0)
        in_specs.append(pl.BlockSpec((None, rb, c), lambda i, j, nb=nb, layer=layer:
                                     (layer, jnp.minimum(i * nj + j, nb - 1), 0)))
        args.append(w)
        out_shape.append(jax.ShapeDtypeStruct((r, c), BF16))
        out_specs.append(pl.BlockSpec((rb, c), block))
    return pl.pallas_call(
        functools.partial(_ffn_ln_kernel, alpha, x_tail is not None, emit_bf16, split_rows is not None,
                          len(cast_jobs)),
        grid=(n_tiles, nj),
        in_specs=in_specs,
        out_specs=out_specs,
        out_shape=out_shape,
        scratch_shapes=[
            pltpu.VMEM((tm, d), BF16),
            pltpu.VMEM((WEIGHT_SLOTS, d, tf), BF16),
            pltpu.VMEM((WEIGHT_SLOTS, d, tf), BF16),
            pltpu.VMEM((WEIGHT_SLOTS, tf, d), BF16),
            pltpu.SemaphoreType.DMA((WEIGHT_SLOTS, 3)),
        ],
        compiler_params=_params(2),
        name="ffn_ln",
    )(*args)


def _rope_cols(x, cos, sin_signed, first_half, half):
    fwd = pltpu.roll(x, x.shape[1] - half, 1)
    bwd = pltpu.roll(x, half, 1)
    return x * cos + jnp.where(first_half, fwd, bwd) * sin_signed


def _qkv_kernel(d_q, d_kv, head_dim, xb_ref, w_ref, cos_ref, sin_ref, q_ref, k_ref, v_ref):
    xb = xb_ref[...]
    bounds = [0, d_q // 2, d_q, d_q + 2 * d_kv]
    parts = [jnp.dot(xb, w_ref[:, lo:hi], preferred_element_type=F32) for lo, hi in zip(bounds, bounds[1:])]
    q = jnp.concatenate(parts[:2], axis=1)
    kv = parts[2]
    cos = cos_ref[...]
    sin = sin_ref[...]
    half = head_dim // 2
    lane = lax.broadcasted_iota(jnp.int32, cos.shape, 1)
    first_half = (lane % head_dim) < half
    q_scale = head_dim ** -0.5
    for c in range(d_q // LANES):
        sl = slice(c * LANES, (c + 1) * LANES)
        q_ref[:, sl] = (_rope_cols(q[:, sl], cos, sin, first_half, half) * q_scale).astype(BF16)
    for c in range(d_kv // LANES):
        sl = slice(c * LANES, (c + 1) * LANES)
        k_ref[:, sl] = _rope_cols(kv[:, sl], cos, sin, first_half, half)
    v_ref[...] = kv[:, d_kv:]


def _qkv_proj(xb, w_in, cos_tab, sin_tab, *, tm, d_q, d_kv, head_dim):
    m, d = xb.shape
    n_cols = d_q + 2 * d_kv
    return pl.pallas_call(
        functools.partial(_qkv_kernel, d_q, d_kv, head_dim),
        grid=(m // tm,),
        in_specs=[
            pl.BlockSpec((tm, d), lambda i: (i, 0)),
            pl.BlockSpec((d, n_cols), lambda i: (0, 0)),
            pl.BlockSpec((tm, LANES), lambda i: (i, 0)),
            pl.BlockSpec((tm, LANES), lambda i: (i, 0)),
        ],
        out_specs=[
            pl.BlockSpec((tm, d_q), lambda i: (i, 0)),
            pl.BlockSpec((tm, d_kv), lambda i: (i, 0)),
            pl.BlockSpec((tm, d_kv), lambda i: (i, 0)),
        ],
        out_shape=[
            jax.ShapeDtypeStruct((m, d_q), BF16),
            jax.ShapeDtypeStruct((m, d_kv), F32),
            jax.ShapeDtypeStruct((m, d_kv), F32),
        ],
        compiler_params=_params(1),
        name="qkv_proj",
    )(xb, w_in, cos_tab, sin_tab)


def _lane_tile4(x128, want_high):
    lane = lax.broadcasted_iota(jnp.int32, x128.shape, 1)
    swapped = pltpu.roll(x128, LANES // 2, 1)
    low = lane < LANES // 2
    both = jnp.where(low, swapped, x128) if want_high else jnp.where(low, x128, swapped)
    return jnp.concatenate([both, both], axis=1)


def _band_attention(problems, sink, n_kv, group, head_dim, between=None):
    rows = problems[0][0].shape[0]
    gw = group * head_dim
    r_idx = lax.broadcasted_iota(jnp.int32, (rows, KEY_SLOTS), 0)
    s_idx = lax.broadcasted_iota(jnp.int32, (rows, KEY_SLOTS), 1)
    head_of_lane = lax.broadcasted_iota(jnp.int32, (rows, gw), 1) // head_dim
    head_keep = [(head_of_lane == g).astype(F32) for g in range(group)]

    scores, values = [], []
    for q, kband, vband, row0, kpos0 in problems:
        diff = r_idx - row0 + WINDOW - s_idx
        valid = (diff >= 0) & (diff <= WINDOW) & (s_idx + kpos0 >= 0)
        for kh in range(n_kv):
            col = (kh * head_dim) // LANES
            high = ((kh * head_dim) % LANES) != 0
            kk = _lane_tile4(kband[:, col * LANES:(col + 1) * LANES], high).astype(BF16)
            values.append(_lane_tile4(vband[:, col * LANES:(col + 1) * LANES], high).astype(BF16))
            qg = q[:, kh * gw:(kh + 1) * gw]
            qs = jnp.concatenate([qg * head_keep[g] for g in range(group)], axis=0).astype(BF16)
            s = lax.dot_general(qs, kk, (((1,), (1,)), ((), ())), preferred_element_type=F32)
            scores.append(jnp.where(valid[None], s.reshape(group, rows, KEY_SLOTS), -jnp.inf))
    s = jnp.concatenate(scores, axis=0)
    sink_all = jnp.concatenate([sink] * len(problems), axis=0)
    mx = jnp.maximum(jnp.max(s, axis=-1, keepdims=True), sink_all)
    p = jnp.exp(s - mx)
    den = jnp.sum(p, axis=-1, keepdims=True) + jnp.exp(sink_all - mx)
    p = p * (1.0 / den)

    if between is not None:
        between()

    outs = []
    for i in range(len(problems)):
        slabs = []
        for kh in range(n_kv):
            c = i * n_kv + kh
            pg = p[c * group:(c + 1) * group].reshape(group * rows, KEY_SLOTS).astype(BF16)
            o = jnp.dot(pg, values[c], preferred_element_type=F32).reshape(group, rows, gw)
            out = o[0]
            for g in range(1, group):
                out = jnp.where(head_of_lane == g, o[g], out)
            slabs.append(out)
        outs.append(jnp.concatenate(slabs, axis=1))
    return outs


def _sink_column(sink_ref):
    return sink_ref[...][:, :, 0:1]


def _attn_sample_kernel(seq, n_kv, group, head_dim, sink_ref, q_ref, kc_ref, kn_ref, vc_ref, vn_ref,
                        _, o_ref):
    n_seq = kc_ref.shape[0]
    d_kv = kn_ref.shape[1]
    per_block = SUBLANES // seq
    pad = jnp.zeros((KEY_SLOTS - WINDOW - SUBLANES, d_kv), F32)
    qf = q_ref[...].astype(F32)
    problems = []
    for i in range(n_seq):
        blk, row0 = i // per_block, (i % per_block) * seq
        rows = slice(blk * SUBLANES, (blk + 1) * SUBLANES)
        k8, v8 = kn_ref[rows, :], vn_ref[rows, :]
        if row0:
            k8 = pltpu.roll(k8, SUBLANES - row0, 0)
            v8 = pltpu.roll(v8, SUBLANES - row0, 0)
        kband = jnp.concatenate([kc_ref[i], k8, pad], axis=0)
        vband = jnp.concatenate([vc_ref[i], v8, pad], axis=0)
        problems.append((qf[rows, :], kband, vband, row0, PAST_LEN - WINDOW))
    outs = _band_attention(problems, _sink_column(sink_ref), n_kv, group, head_dim)
    r_idx = lax.broadcasted_iota(jnp.int32, outs[0].shape, 0)
    blocks = []
    for blk in range(n_seq // per_block):
        out = outs[blk * per_block]
        for s in range(1, per_block):
            out = jnp.where(r_idx >= s * seq, outs[blk * per_block + s], out)
        blocks.append(out)
    o_ref[...] = jnp.concatenate(blocks, axis=0).astype(BF16)


def _attn_sample(attn, q, k, v, k_cache, v_cache, sink_tab, layer, *, row0, seq, seqs_per_step, n_kv,
                 group, head_dim):
    d_q = q.shape[1]
    d_kv = k.shape[1]
    nbatch = k_cache.shape[1]
    ns = seqs_per_step
    rows = ns * seq
    blk0 = row0 // rows
    new = lambda b: (blk0 + b, 0)
    cache = lambda b: (layer, b, 0, 0)
    return pl.pallas_call(
        functools.partial(_attn_sample_kernel, seq, n_kv, group, head_dim),
        grid=(nbatch // ns,),
        in_specs=[
            pl.BlockSpec((None, n_kv * group, 1, LANES), lambda b: (layer, 0, 0, 0)),
            pl.BlockSpec((rows, d_q), new),
            pl.BlockSpec((None, ns, WINDOW, d_kv), cache),
            pl.BlockSpec((rows, d_kv), new),
            pl.BlockSpec((None, ns, WINDOW, d_kv), cache),
            pl.BlockSpec((rows, d_kv), new),
            pl.BlockSpec(memory_space=pl.ANY),
        ],
        out_specs=pl.BlockSpec((rows, d_q), new),
        out_shape=jax.ShapeDtypeStruct(attn.shape, attn.dtype),
        input_output_aliases={6: 0},
        compiler_params=_params(1),
        name="attn_sample",
    )(sink_tab, q, k_cache, k, v_cache, v, attn)


def _conv_attn_kernel(nc, blocks_per_seq, n_kv, group, head_dim,
                      xb_ref, wb_ref, wc_ref, wh_ref, cw_ref, t_ref, f1_ref, f2_ref,
                      sink_ref, q_ref, kp_ref, kc_ref, vp_ref, vc_ref,
                      y_ref, u_ref, o_ref, ubuf):
    tm = xb_ref.shape[0]
    i = pl.program_id(0)
    c = pl.program_id(1)

    @pl.when(i == 0)
    def _():
        ubuf[c] = jnp.zeros(ubuf.shape[1:], F32)

    carry = ubuf[c]

    n_blocks = tm // WINDOW
    per_step = n_blocks // nc
    problems, starts = [], []
    for s in range(per_step):
        b = c * per_step + s
        r0 = pl.multiple_of(b * WINDOW, WINDOW)
        rp = pl.multiple_of(jnp.maximum(b - 1, 0) * WINDOW, WINDOW)
        n = lax.rem(i * n_blocks + b, blocks_per_seq)
        k_before = jnp.where(b == 0, kp_ref[...], kc_ref[pl.ds(rp, WINDOW), :])
        v_before = jnp.where(b == 0, vp_ref[...], vc_ref[pl.ds(rp, WINDOW), :])
        kband = jnp.concatenate([k_before, kc_ref[pl.ds(r0, WINDOW), :]], axis=0)
        vband = jnp.concatenate([v_before, vc_ref[pl.ds(r0, WINDOW), :]], axis=0)
        problems.append((q_ref[pl.ds(r0, WINDOW), :].astype(F32), kband, vband, 0, (n - 1) * WINDOW))
        starts.append(r0)
    proj = []

    def conv_input_matmuls():
        xb = xb_ref[...]
        for w_ref in (wc_ref, wh_ref):
            proj.append(jnp.dot(xb, w_ref[...], preferred_element_type=F32))

    outs = _band_attention(problems, _sink_column(sink_ref), n_kv, group, head_dim,
                           between=conv_input_matmuls)
    cc, ch = proj
    cb = jnp.dot(xb_ref[...], wb_ref[...], preferred_element_type=F32)
    u = cc * ch
    row8 = lax.broadcasted_iota(jnp.int32, carry.shape, 0)

    def shifted(k):
        r = pltpu.roll(u, k, 0)
        head = jnp.where(row8 < k, pltpu.roll(carry, k, 0), r[0:SUBLANES])
        return jnp.concatenate([head, r[SUBLANES:]], axis=0)

    t = t_ref[...]
    last = i == pl.num_programs(0) - 1
    u_m1 = jnp.where(t >= 1, shifted(1), jnp.where(last, f1_ref[...], 0.0))
    u_m2 = jnp.where(t >= 2, shifted(2), jnp.where(last, f2_ref[...], 0.0))
    cw = cw_ref[...]
    conv = cw[0:1, :] * u_m2 + cw[1:2, :] * u_m1 + cw[2:3, :] * u

    for r0, out in zip(starts, outs):
        o_ref[pl.ds(r0, WINDOW), :] = out.astype(BF16)
    y_ref[...] = (cb * conv).astype(BF16)
    u_ref[...] = u
    ubuf[c] = u[tm - SUBLANES:, :]


def _conv_attn(xb, w_in, conv_w, t_idx, fill1, fill2, q, k, v, sink_tab, layer, *, tm, tc, col0, d_conv,
               seq, n_kv, group, head_dim):
    m, d = xb.shape
    d_q = q.shape[1]
    d_kv = k.shape[1]
    nc = d_conv // tc
    off = col0 // tc
    n_blocks = tm // WINDOW
    w_spec = lambda kk: pl.BlockSpec((d, tc), lambda i, c: (0, off + kk * nc + c))
    tile = lambda i, c: (i, 0)
    before = lambda i, c: (jnp.maximum(i * n_blocks - 1, 0), 0)
    return pl.pallas_call(
        functools.partial(_conv_attn_kernel, nc, seq // WINDOW, n_kv, group, head_dim),
        grid=(pl.cdiv(m, tm), nc),
        in_specs=[
            pl.BlockSpec((tm, d), tile),
            w_spec(0), w_spec(1), w_spec(2),
            pl.BlockSpec((None, CONV_W, tc), lambda i, c: (layer, 0, c)),
            pl.BlockSpec((tm, 1), tile),
            pl.BlockSpec((tm, tc), lambda i, c: (0, c)),
            pl.BlockSpec((tm, tc), lambda i, c: (0, c)),
            pl.BlockSpec((None, n_kv * group, 1, LANES), lambda i, c: (layer, 0, 0, 0)),
            pl.BlockSpec((tm, d_q), tile),
            pl.BlockSpec((WINDOW, d_kv), before),
            pl.BlockSpec((tm, d_kv), tile),
            pl.BlockSpec((WINDOW, d_kv), before),
            pl.BlockSpec((tm, d_kv), tile),
        ],
        out_specs=[
            pl.BlockSpec((tm, tc), lambda i, c: (i, c)),
            pl.BlockSpec((tm, tc), lambda i, c: (i, c)),
            pl.BlockSpec((tm, d_q), tile),
        ],
        out_shape=[
            jax.ShapeDtypeStruct((m, d_conv), BF16),
            jax.ShapeDtypeStruct((m, d_conv), F32),
            jax.ShapeDtypeStruct((m, d_q), BF16),
        ],
        scratch_shapes=[pltpu.VMEM((nc, SUBLANES, tc), F32)],
        compiler_params=_params(2),
        name="conv_attn",
    )(xb, w_in, w_in, w_in, conv_w, t_idx, fill1, fill2, sink_tab, q, k, k, v, v)


def _mix_ln_kernel(alpha, x_ref, xb_ref, a_ref, c_ref, wga_ref, wgc_ref, wa_ref, wc_ref, wo_ref, g_ref,
                   b_ref, o_ref, acc_ref):
    j = pl.program_id(1)

    @pl.when(j == 0)
    def _():
        acc_ref[...] = jnp.zeros_like(acc_ref)

    xb = xb_ref[...]
    ga = jnp.dot(xb, wga_ref[...], preferred_element_type=F32)
    gc = jnp.dot(xb, wgc_ref[...], preferred_element_type=F32)
    pa = jnp.dot(a_ref[...], wa_ref[...], preferred_element_type=F32)
    pc = jnp.dot(c_ref[...], wc_ref[...], preferred_element_type=F32)
    merged = (jax.nn.sigmoid(ga) * pa + jax.nn.sigmoid(gc) * pc).astype(BF16)
    acc_ref[...] += jnp.dot(merged, wo_ref[...], preferred_element_type=F32)

    @pl.when(j == pl.num_programs(1) - 1)
    def _():
        y = alpha * x_ref[...] + acc_ref[...]
        o_ref[...] = _layer_norm(y, g_ref[...], b_ref[...])


def _mix_ln(x, xb, attn, yconv, w_in, w_a, w_c, w_o, ln_g, ln_b, ln_idx, alpha, *, tm, tc, ga_col0,
            gc_col0):
    m, d = x.shape
    d_q = attn.shape[1]
    d_conv = yconv.shape[1]
    return pl.pallas_call(
        functools.partial(_mix_ln_kernel, alpha),
        grid=(m // tm, d // tc),
        in_specs=[
            pl.BlockSpec((tm, d), lambda i, j: (i, 0)),
            pl.BlockSpec((tm, d), lambda i, j: (i, 0)),
            pl.BlockSpec((tm, d_q), lambda i, j: (i, 0)),
            pl.BlockSpec((tm, d_conv), lambda i, j: (i, 0)),
            pl.BlockSpec((d, tc), lambda i, j: (0, ga_col0 // tc + j)),
            pl.BlockSpec((d, tc), lambda i, j: (0, gc_col0 // tc + j)),
            pl.BlockSpec((d_q, tc), lambda i, j: (0, j)),
            pl.BlockSpec((d_conv, tc), lambda i, j: (0, j)),
            pl.BlockSpec((tc, d), lambda i, j: (j, 0)),
            pl.BlockSpec((None, 1, d), lambda i, j: (ln_idx, 0, 0)),
            pl.BlockSpec((None, 1, d), lambda i, j: (ln_idx, 0, 0)),
        ],
        out_specs=pl.BlockSpec((tm, d), lambda i, j: (i, 0)),
        out_shape=jax.ShapeDtypeStruct((m, d), F32),
        scratch_shapes=[pltpu.VMEM((tm, d), F32)],
        compiler_params=_params(2),
        name="mix_ln",
    )(x, xb, attn, yconv, w_in, w_in, w_a, w_c, w_o, ln_g, ln_b)


def _rope_tables(pos, head_dim):
    inv_freq = ROPE_THETA ** (-jnp.arange(0, head_dim, 2, dtype=F32) / head_dim)
    ang = pos.astype(F32)[:, None] * inv_freq[None, :]
    cos = jnp.cos(ang)
    sin = jnp.sin(ang)
    reps = LANES // head_dim
    cos_tab = jnp.tile(jnp.concatenate([cos, cos], axis=1), (1, reps))
    sin_tab = jnp.tile(jnp.concatenate([-sin, sin], axis=1), (1, reps))
    return cos_tab, sin_tab


def _largest_divisor(n, cap, multiple_of=1):
    return max(k for k in range(multiple_of, cap + 1, multiple_of) if n % k == 0)


def _seq_tails(a, n_seq, seq, rows):
    return jnp.stack([a[(b + 1) * seq - rows:(b + 1) * seq] for b in range(n_seq)])


def kernel(x_prompt, x_sample, cache_k_win, cache_v_win, state_conv, ln_g, ln_b, w_in, sinks, conv_w,
           w_branch_attn, w_branch_conv, w_out, ffn1_gu, ffn1_down, ffn2_gu, ffn2_down):
    depth = w_in.shape[0]
    bp, tp, d = x_prompt.shape
    bs, ts, _ = x_sample.shape
    n_kv, head_dim = cache_k_win.shape[-2:]
    d_q = w_branch_attn.shape[1]
    d_conv = conv_w.shape[-1]
    d_kv = n_kv * head_dim
    group = d_q // d_kv
    mp, ms = bp * tp, bs * ts
    m = mp + ms
    alpha = (2.0 * depth) ** 0.25
    tm = _largest_divisor(m, MAX_ROW_TILE, BF16_ROWS)
    ns = _largest_divisor(bs, SAMPLE_SEQS_PER_STEP, SUBLANES // ts)
    assert tm >= ms and SUBLANES % ts == 0 and mp % (ns * ts) == 0 and tp % WINDOW == 0
    assert ts >= CONV_W - 1 and tp >= WINDOW
    tf = 512
    tc = 512
    tm_mix = MIX_ROW_TILE
    assert mp % tm_mix == 0 and ms <= tm_mix and tm_mix % (WINDOW * (d_conv // tc)) == 0
    conv_col0 = d_q + 2 * d_kv
    ga_col0 = conv_col0 + 3 * d_conv
    gc_col0 = ga_col0 + d

    ln_g = ln_g.reshape(depth * 3, 1, d)
    ln_b = ln_b.reshape(depth * 3, 1, d)
    sink_tab = jnp.broadcast_to(sinks[:, :, None, None], sinks.shape + (1, LANES))

    t_prompt = jnp.tile(jnp.arange(tp, dtype=jnp.int32), bp)
    t_sample = jnp.tile(jnp.arange(ts, dtype=jnp.int32), bs)
    cos_tab, sin_tab = _rope_tables(jnp.concatenate([t_prompt, PAST_LEN + t_sample]), head_dim)
    t_idx = jnp.concatenate([t_prompt, t_sample]).reshape(m, 1)
    k_cache = cache_k_win.reshape(depth, bs, WINDOW, d_kv)
    v_cache = cache_v_win.reshape(depth, bs, WINDOW, d_kv)

    f1_gu, f1_down = ffn1_gu[0].astype(BF16), ffn1_down[0].astype(BF16)
    x = x_prompt.reshape(mp, d)
    x_tail = x_sample.reshape(ms, d)
    ks_p, vs_p, cs_p, ks_s, vs_s, cs_s = [], [], [], [], [], []
    for l in range(depth):
        jobs = [(w, l) for w in (w_in, w_branch_attn, w_branch_conv, w_out, ffn2_gu, ffn2_down)]
        x1, x1b, w_in_b, w_a_b, w_c_b, w_o_b, f2_gu, f2_down = _ffn_ln(
            x, f1_gu, f1_down, ln_g, ln_b, 3 * l, alpha, tm=tm, tf=tf, emit_bf16=True, x_tail=x_tail,
            cast_jobs=jobs)
        q, k, v = _qkv_proj(x1b, w_in_b, cos_tab, sin_tab, tm=tm, d_q=d_q, d_kv=d_kv, head_dim=head_dim)
        st = state_conv[l]
        zeros = jnp.zeros((bs, ts, d_conv), F32)
        fill1 = zeros.at[:, 0].set(st[:, 1]).reshape(ms, d_conv)
        fill2 = zeros.at[:, 0].set(st[:, 0]).at[:, 1].set(st[:, 1]).reshape(ms, d_conv)
        fill1 = jnp.pad(fill1, ((0, tm_mix - ms), (0, 0)))
        fill2 = jnp.pad(fill2, ((0, tm_mix - ms), (0, 0)))
        yconv, u, attn = _conv_attn(x1b, w_in_b, conv_w, t_idx, fill1, fill2, q, k, v, sink_tab, l,
                                    tm=tm_mix, tc=tc, col0=conv_col0, d_conv=d_conv, seq=tp, n_kv=n_kv,
                                    group=group, head_dim=head_dim)
        attn = _attn_sample(attn, q, k, v, k_cache, v_cache, sink_tab, l, row0=mp, seq=ts,
                            seqs_per_step=ns, n_kv=n_kv, group=group, head_dim=head_dim)
        x2 = _mix_ln(x1, x1b, attn, yconv, w_in_b, w_a_b, w_c_b, w_o_b, ln_g, ln_b, 3 * l + 1, alpha,
                     tm=tm, tc=tc, ga_col0=ga_col0, gc_col0=gc_col0)
        if l + 1 < depth:
            x, f1_gu, f1_down = _ffn_ln(x2, f2_gu, f2_down, ln_g, ln_b, 3 * l + 2, alpha, tm=tm, tf=tf,
                                        cast_jobs=[(ffn1_gu, l + 1), (ffn1_down, l + 1)])
            x_tail = None
        else:
            y_p, y_s = _ffn_ln(x2, f2_gu, f2_down, ln_g, ln_b, 3 * l + 2, alpha, tm=tm, tf=tf,
                               split_rows=mp)

        ks_p.append(_seq_tails(k, bp, tp, WINDOW).reshape(bp, WINDOW, n_kv, head_dim))
        vs_p.append(_seq_tails(v, bp, tp, WINDOW).reshape(bp, WINDOW, n_kv, head_dim))
        cs_p.append(_seq_tails(u, bp, tp, CONV_W - 1))
        k_new = k[mp:].reshape(bs, ts, n_kv, head_dim)
        v_new = v[mp:].reshape(bs, ts, n_kv, head_dim)
        ks_s.append(jnp.concatenate([cache_k_win[l][:, ts:], k_new], axis=1))
        vs_s.append(jnp.concatenate([cache_v_win[l][:, ts:], v_new], axis=1))
        cs_s.append(u[mp:].reshape(bs, ts, d_conv)[:, -(CONV_W - 1):])

    return (y_p.reshape(bp, tp, d), y_s.reshape(bs, ts, d), jnp.stack(ks_p), jnp.stack(vs_p),
            jnp.stack(cs_p), jnp.stack(ks_s), jnp.stack(vs_s), jnp.stack(cs_s))
```

```python
import functools

import jax
import jax.numpy as jnp
from jax import lax
from jax.experimental import pallas as pl
from jax.experimental.pallas import tpu as pltpu

F32 = jnp.float32
BF16 = jnp.bfloat16

PAST_LEN = 16384
WINDOW = 128
ROPE_THETA = 10000.0
LN_EPS = 1e-5
CONV_W = 3

LANES = 128
SUBLANES = 8
BF16_ROWS = 16
VMEM_LIMIT_BYTES = 62 * 1024 * 1024

KEY_SLOTS = 2 * WINDOW
MAX_ROW_TILE = 640
WEIGHT_SLOTS = 3
MIX_ROW_TILE = 512
SAMPLE_SEQS_PER_STEP = 8


def _params(n_axes):
    return pltpu.CompilerParams(
        dimension_semantics=("arbitrary",) * n_axes,
        vmem_limit_bytes=VMEM_LIMIT_BYTES,
    )


def _layer_norm(y, g, b):
    mu = jnp.mean(y, axis=-1, keepdims=True)
    d = y - mu
    var = jnp.mean(d * d, axis=-1, keepdims=True)
    return d * lax.rsqrt(var + LN_EPS) * g + b


def _round_up(n, k):
    return -(-n // k) * k


def _ffn_ln_kernel(alpha, has_tail, emit_bf16, split, n_jobs, *refs):
    refs = list(refs)
    x_ref = refs.pop(0)
    xt_ref = refs.pop(0) if has_tail else None
    wg_ref, wu_ref, wd_ref, g_ref, b_ref = refs[:5]
    del refs[:5]
    job_src = refs[:n_jobs]
    del refs[:n_jobs]
    o_ref = refs.pop(0)
    ob_ref = refs.pop(0) if emit_bf16 else None
    ot_ref = refs.pop(0) if split else None
    job_dst = refs[:n_jobs]
    del refs[:n_jobs]
    xb_ref, wg_buf, wu_buf, wd_buf, sems, acc_ref, res_ref, *rest = refs
    resb_ref = rest[0] if emit_bf16 else None
    out_sems = rest[-1]
    i = pl.program_id(0)
    j = pl.program_id(1)
    nj = pl.num_programs(1)
    last_tile = i == pl.num_programs(0) - 1
    tm = x_ref.shape[0]
    tf = wg_buf.shape[2]

    step = i * nj + j
    n_steps = pl.num_programs(0) * nj

    def chunk_copies(chunk, slot):
        return (
            pltpu.make_async_copy(wg_ref.at[:, pl.ds(chunk * tf, tf)], wg_buf.at[slot], sems.at[slot, 0]),
            pltpu.make_async_copy(wu_ref.at[:, pl.ds((nj + chunk) * tf, tf)], wu_buf.at[slot],
                                  sems.at[slot, 1]),
            pltpu.make_async_copy(wd_ref.at[pl.ds(chunk * tf, tf), :], wd_buf.at[slot], sems.at[slot, 2]),
        )

    def start_step(t):
        for cp in chunk_copies(lax.rem(t, nj), lax.rem(t, WEIGHT_SLOTS)):
            cp.start()

    @pl.when(step == 0)
    def _():
        for t in range(WEIGHT_SLOTS - 1):
            start_step(jnp.int32(t))

    ahead = step + (WEIGHT_SLOTS - 1)

    @pl.when(ahead < n_steps)
    def _():
        start_step(ahead)

    slot = lax.rem(step, WEIGHT_SLOTS)
    for cp in chunk_copies(j, slot):
        cp.wait()

    def load_x():
        x = x_ref[...]
        if xt_ref is None:
            return x
        merged = jnp.concatenate([x[:tm - xt_ref.shape[0]], xt_ref[...]], axis=0)
        return jnp.where(last_tile, merged, x)

    @pl.when(j == 0)
    def _():
        xb_ref[...] = load_x().astype(BF16)
        acc_ref[...] = jnp.zeros_like(acc_ref)

    xb = xb_ref[...]
    g = jnp.dot(xb, wg_buf[slot], preferred_element_type=F32)
    u = jnp.dot(xb, wu_buf[slot], preferred_element_type=F32)
    h = (jax.nn.silu(g) * u).astype(BF16)
    acc_ref[...] += jnp.dot(h, wd_buf[slot], preferred_element_type=F32)
    for src, dst in zip(job_src, job_dst):
        dst[...] = src[...].astype(BF16)

    n_tail = ot_ref.shape[0] if split else 0
    head_rows_last = o_ref.shape[0] - (o_ref.shape[0] // tm) * tm or tm

    def result_copies(tile, is_last):
        r0 = pl.multiple_of(tile * tm, BF16_ROWS)
        rows = head_rows_last if is_last else tm
        cps = [pltpu.make_async_copy(res_ref.at[pl.ds(0, rows), :], o_ref.at[pl.ds(r0, rows), :],
                                     out_sems.at[0])]
        if emit_bf16:
            cps.append(pltpu.make_async_copy(resb_ref, ob_ref.at[pl.ds(r0, tm), :], out_sems.at[1]))
        if split and is_last:
            cps.append(pltpu.make_async_copy(res_ref.at[pl.ds(tm - n_tail, n_tail), :], ot_ref,
                                             out_sems.at[2]))
        return cps

    @pl.when(j == pl.num_programs(1) - 1)
    def _():
        y = alpha * load_x() + 0.5 * acc_ref[...]
        out = _layer_norm(y, g_ref[...], b_ref[...])

        @pl.when(i > 0)
        def _():
            for cp in result_copies(i - 1, False):
                cp.wait()

        res_ref[...] = out
        if emit_bf16:
            resb_ref[...] = out.astype(BF16)

        @pl.when(jnp.logical_not(last_tile))
        def _():
            for cp in result_copies(i, False):
                cp.start()

        @pl.when(last_tile)
        def _():
            for cp in result_copies(i, True):
                cp.start()
            for cp in result_copies(i, True):
                cp.wait()


def _ffn_ln(x, w_gu, w_down, ln_g, ln_b, ln_idx, alpha, *, tm, tf, emit_bf16=False, x_tail=None,
            split_rows=None, cast_jobs=()):
    d = x.shape[1]
    m = x.shape[0] + (0 if x_tail is None else x_tail.shape[0])
    f = w_down.shape[0]
    nj = f // tf
    n_tiles = m // tm
    row = lambda i, j: (i, 0)
    fixed = lambda i, j: (0, 0)
    in_specs = [pl.BlockSpec((tm, d), row)]
    args = [x]
    if x_tail is not None:
        in_specs.append(pl.BlockSpec(x_tail.shape, fixed))
        args.append(x_tail)
    in_specs += [
        pl.BlockSpec(memory_space=pl.ANY),
        pl.BlockSpec(memory_space=pl.ANY),
        pl.BlockSpec(memory_space=pl.ANY),
        pl.BlockSpec((None, 1, d), lambda i, j: (ln_idx, 0, 0)),
        pl.BlockSpec((None, 1, d), lambda i, j: (ln_idx, 0, 0)),
    ]
    args += [w_gu, w_gu, w_down, ln_g, ln_b]
    m_head = m if split_rows is None else split_rows
    out_shape = [jax.ShapeDtypeStruct((m_head, d), F32)]
    out_specs = [pl.BlockSpec(memory_space=pl.ANY)]
    if emit_bf16:
        out_shape.append(jax.ShapeDtypeStruct((m, d), BF16))
        out_specs.append(pl.BlockSpec(memory_space=pl.ANY))
    if split_rows is not None:
        out_shape.append(jax.ShapeDtypeStruct((m - split_rows, d), F32))
        out_specs.append(pl.BlockSpec(memory_space=pl.ANY))
    n_steps = n_tiles * nj
    for w, layer in cast_jobs:
        _, r, c = w.shape
        rb = _round_up(-(-r // n_steps), BF16_ROWS)
        nb = -(-r // rb)
        block = lambda i, j, nb=nb: (jnp.minimum(i * nj + j, nb - 1), 0)
        in_specs.append(pl.BlockSpec((None, rb, c), lambda i, j, nb=nb, layer=layer:
                                     (layer, jnp.minimum(i * nj + j, nb - 1), 0)))
        args.append(w)
        out_shape.append(jax.ShapeDtypeStruct((r, c), BF16))
        out_specs.append(pl.BlockSpec((rb, c), block))
    return pl.pallas_call(
        functools.partial(_ffn_ln_kernel, alpha, x_tail is not None, emit_bf16, split_rows is not None,
                          len(cast_jobs)),
        grid=(n_tiles, nj),
        in_specs=in_specs,
        out_specs=out_specs,
        out_shape=out_shape,
        scratch_shapes=[
            pltpu.VMEM((tm, d), BF16),
            pltpu.VMEM((WEIGHT_SLOTS, d, tf), BF16),
            pltpu.VMEM((WEIGHT_SLOTS, d, tf), BF16),
            pltpu.VMEM((WEIGHT_SLOTS, tf, d), BF16),
            pltpu.SemaphoreType.DMA((WEIGHT_SLOTS, 3)),
            pltpu.VMEM((tm, d), F32),
            pltpu.VMEM((tm, d), F32),
            *([pltpu.VMEM((tm, d), BF16)] if emit_bf16 else []),
            pltpu.SemaphoreType.DMA((3,)),
        ],
        compiler_params=_params(2),
        name="ffn_ln",
    )(*args)


def _rope_cols(x, cos, sin_signed, first_half, half):
    fwd = pltpu.roll(x, x.shape[1] - half, 1)
    bwd = pltpu.roll(x, half, 1)
    return x * cos + jnp.where(first_half, fwd, bwd) * sin_signed


def _qkv_kernel(d_q, d_kv, head_dim, xb_ref, w_ref, cos_ref, sin_ref, q_ref, k_ref, v_ref):
    xb = xb_ref[...]
    bounds = [0, d_q // 2, d_q, d_q + 2 * d_kv]
    parts = [jnp.dot(xb, w_ref[:, lo:hi], preferred_element_type=F32) for lo, hi in zip(bounds, bounds[1:])]
    q = jnp.concatenate(parts[:2], axis=1)
    kv = parts[2]
    cos = cos_ref[...]
    sin = sin_ref[...]
    half = head_dim // 2
    lane = lax.broadcasted_iota(jnp.int32, cos.shape, 1)
    first_half = (lane % head_dim) < half
    q_scale = head_dim ** -0.5
    for c in range(d_q // LANES):
        sl = slice(c * LANES, (c + 1) * LANES)
        q_ref[:, sl] = (_rope_cols(q[:, sl], cos, sin, first_half, half) * q_scale).astype(BF16)
    for c in range(d_kv // LANES):
        sl = slice(c * LANES, (c + 1) * LANES)
        k_ref[:, sl] = _rope_cols(kv[:, sl], cos, sin, first_half, half)
    v_ref[...] = kv[:, d_kv:]


def _qkv_proj(xb, w_in, cos_tab, sin_tab, *, tm, d_q, d_kv, head_dim):
    m, d = xb.shape
    n_cols = d_q + 2 * d_kv
    return pl.pallas_call(
        functools.partial(_qkv_kernel, d_q, d_kv, head_dim),
        grid=(m // tm,),
        in_specs=[
            pl.BlockSpec((tm, d), lambda i: (i, 0)),
            pl.BlockSpec((d, n_cols), lambda i: (0, 0)),
            pl.BlockSpec((tm, LANES), lambda i: (i, 0)),
            pl.BlockSpec((tm, LANES), lambda i: (i, 0)),
        ],
        out_specs=[
            pl.BlockSpec((tm, d_q), lambda i: (i, 0)),
            pl.BlockSpec((tm, d_kv), lambda i: (i, 0)),
            pl.BlockSpec((tm, d_kv), lambda i: (i, 0)),
        ],
        out_shape=[
            jax.ShapeDtypeStruct((m, d_q), BF16),
            jax.ShapeDtypeStruct((m, d_kv), F32),
            jax.ShapeDtypeStruct((m, d_kv), F32),
        ],
        compiler_params=_params(1),
        name="qkv_proj",
    )(xb, w_in, cos_tab, sin_tab)


def _lane_tile4(x128, want_high):
    lane = lax.broadcasted_iota(jnp.int32, x128.shape, 1)
    swapped = pltpu.roll(x128, LANES // 2, 1)
    low = lane < LANES // 2
    both = jnp.where(low, swapped, x128) if want_high else jnp.where(low, x128, swapped)
    return jnp.concatenate([both, both], axis=1)


def _band_attention(problems, sink, n_kv, group, head_dim, between=None):
    rows = problems[0][0].shape[0]
    gw = group * head_dim
    r_idx = lax.broadcasted_iota(jnp.int32, (rows, KEY_SLOTS), 0)
    s_idx = lax.broadcasted_iota(jnp.int32, (rows, KEY_SLOTS), 1)
    head_of_lane = lax.broadcasted_iota(jnp.int32, (rows, gw), 1) // head_dim
    head_keep = [(head_of_lane == g).astype(F32) for g in range(group)]

    scores, values = [], []
    for q, kband, vband, row0, kpos0 in problems:
        diff = r_idx - row0 + WINDOW - s_idx
        valid = (diff >= 0) & (diff <= WINDOW) & (s_idx + kpos0 >= 0)
        for kh in range(n_kv):
            col = (kh * head_dim) // LANES
            high = ((kh * head_dim) % LANES) != 0
            kk = _lane_tile4(kband[:, col * LANES:(col + 1) * LANES], high).astype(BF16)
            values.append(_lane_tile4(vband[:, col * LANES:(col + 1) * LANES], high).astype(BF16))
            qg = q[:, kh * gw:(kh + 1) * gw]
            qs = jnp.concatenate([qg * head_keep[g] for g in range(group)], axis=0).astype(BF16)
            s = lax.dot_general(qs, kk, (((1,), (1,)), ((), ())), preferred_element_type=F32)
            scores.append(jnp.where(valid[None], s.reshape(group, rows, KEY_SLOTS), -jnp.inf))
    s = jnp.concatenate(scores, axis=0)
    sink_all = jnp.concatenate([sink] * len(problems), axis=0)
    mx = jnp.maximum(jnp.max(s, axis=-1, keepdims=True), sink_all)
    p = jnp.exp(s - mx)
    den = jnp.sum(p, axis=-1, keepdims=True) + jnp.exp(sink_all - mx)
    p = p * (1.0 / den)

    if between is not None:
        between()

    outs = []
    for i in range(len(problems)):
        slabs = []
        for kh in range(n_kv):
            c = i * n_kv + kh
            pg = p[c * group:(c + 1) * group].reshape(group * rows, KEY_SLOTS).astype(BF16)
            o = jnp.dot(pg, values[c], preferred_element_type=F32).reshape(group, rows, gw)
            out = o[0]
            for g in range(1, group):
                out = jnp.where(head_of_lane == g, o[g], out)
            slabs.append(out)
        outs.append(jnp.concatenate(slabs, axis=1))
    return outs


def _sink_column(sink_ref):
    return sink_ref[...][:, :, 0:1]


def _attn_sample_kernel(seq, n_kv, group, head_dim, sink_ref, q_ref, kc_ref, kn_ref, vc_ref, vn_ref,
                        _, o_ref):
    n_seq = kc_ref.shape[0]
    d_kv = kn_ref.shape[1]
    per_block = SUBLANES // seq
    pad = jnp.zeros((KEY_SLOTS - WINDOW - SUBLANES, d_kv), F32)
    qf = q_ref[...].astype(F32)
    problems = []
    for i in range(n_seq):
        blk, row0 = i // per_block, (i % per_block) * seq
        rows = slice(blk * SUBLANES, (blk + 1) * SUBLANES)
        k8, v8 = kn_ref[rows, :], vn_ref[rows, :]
        if row0:
            k8 = pltpu.roll(k8, SUBLANES - row0, 0)
            v8 = pltpu.roll(v8, SUBLANES - row0, 0)
        kband = jnp.concatenate([kc_ref[i], k8, pad], axis=0)
        vband = jnp.concatenate([vc_ref[i], v8, pad], axis=0)
        problems.append((qf[rows, :], kband, vband, row0, PAST_LEN - WINDOW))
    outs = _band_attention(problems, _sink_column(sink_ref), n_kv, group, head_dim)
    r_idx = lax.broadcasted_iota(jnp.int32, outs[0].shape, 0)
    blocks = []
    for blk in range(n_seq // per_block):
        out = outs[blk * per_block]
        for s in range(1, per_block):
            out = jnp.where(r_idx >= s * seq, outs[blk * per_block + s], out)
        blocks.append(out)
    o_ref[...] = jnp.concatenate(blocks, axis=0).astype(BF16)


def _attn_sample(attn, q, k, v, k_cache, v_cache, sink_tab, layer, *, row0, seq, seqs_per_step, n_kv,
                 group, head_dim):
    d_q = q.shape[1]
    d_kv = k.shape[1]
    nbatch = k_cache.shape[1]
    ns = seqs_per_step
    rows = ns * seq
    blk0 = row0 // rows
    new = lambda b: (blk0 + b, 0)
    cache = lambda b: (layer, b, 0, 0)
    return pl.pallas_call(
        functools.partial(_attn_sample_kernel, seq, n_kv, group, head_dim),
        grid=(nbatch // ns,),
        in_specs=[
            pl.BlockSpec((None, n_kv * group, 1, LANES), lambda b: (layer, 0, 0, 0)),
            pl.BlockSpec((rows, d_q), new),
            pl.BlockSpec((None, ns, WINDOW, d_kv), cache),
            pl.BlockSpec((rows, d_kv), new),
            pl.BlockSpec((None, ns, WINDOW, d_kv), cache),
            pl.BlockSpec((rows, d_kv), new),
            pl.BlockSpec(memory_space=pl.ANY),
        ],
        out_specs=pl.BlockSpec((rows, d_q), new),
        out_shape=jax.ShapeDtypeStruct(attn.shape, attn.dtype),
        input_output_aliases={6: 0},
        compiler_params=_params(1),
        name="attn_sample",
    )(sink_tab, q, k_cache, k, v_cache, v, attn)


def _conv_attn_kernel(nc, blocks_per_seq, n_kv, group, head_dim,
                      xb_ref, wb_ref, wc_ref, wh_ref, cw_ref, t_ref, f1_ref, f2_ref,
                      sink_ref, q_ref, kp_ref, kc_ref, vp_ref, vc_ref,
                      y_ref, u_ref, o_ref, ubuf):
    tm = xb_ref.shape[0]
    i = pl.program_id(0)
    c = pl.program_id(1)

    @pl.when(i == 0)
    def _():
        ubuf[c] = jnp.zeros(ubuf.shape[1:], F32)

    carry = ubuf[c]

    n_blocks = tm // WINDOW
    per_step = n_blocks // nc
    problems, starts = [], []
    for s in range(per_step):
        b = c * per_step + s
        r0 = pl.multiple_of(b * WINDOW, WINDOW)
        rp = pl.multiple_of(jnp.maximum(b - 1, 0) * WINDOW, WINDOW)
        n = lax.rem(i * n_blocks + b, blocks_per_seq)
        k_before = jnp.where(b == 0, kp_ref[...], kc_ref[pl.ds(rp, WINDOW), :])
        v_before = jnp.where(b == 0, vp_ref[...], vc_ref[pl.ds(rp, WINDOW), :])
        kband = jnp.concatenate([k_before, kc_ref[pl.ds(r0, WINDOW), :]], axis=0)
        vband = jnp.concatenate([v_before, vc_ref[pl.ds(r0, WINDOW), :]], axis=0)
        problems.append((q_ref[pl.ds(r0, WINDOW), :].astype(F32), kband, vband, 0, (n - 1) * WINDOW))
        starts.append(r0)
    proj = []

    def conv_input_matmuls():
        xb = xb_ref[...]
        for w_ref in (wc_ref, wh_ref):
            proj.append(jnp.dot(xb, w_ref[...], preferred_element_type=F32))

    outs = _band_attention(problems, _sink_column(sink_ref), n_kv, group, head_dim,
                           between=conv_input_matmuls)
    cc, ch = proj
    cb = jnp.dot(xb_ref[...], wb_ref[...], preferred_element_type=F32)
    u = cc * ch
    row8 = lax.broadcasted_iota(jnp.int32, carry.shape, 0)

    def shifted(k):
        r = pltpu.roll(u, k, 0)
        head = jnp.where(row8 < k, pltpu.roll(carry, k, 0), r[0:SUBLANES])
        return jnp.concatenate([head, r[SUBLANES:]], axis=0)

    t = t_ref[...]
    last = i == pl.num_programs(0) - 1
    u_m1 = jnp.where(t >= 1, shifted(1), jnp.where(last, f1_ref[...], 0.0))
    u_m2 = jnp.where(t >= 2, shifted(2), jnp.where(last, f2_ref[...], 0.0))
    cw = cw_ref[...]
    conv = cw[0:1, :] * u_m2 + cw[1:2, :] * u_m1 + cw[2:3, :] * u

    for r0, out in zip(starts, outs):
        o_ref[pl.ds(r0, WINDOW), :] = out.astype(BF16)
    y_ref[...] = (cb * conv).astype(BF16)
    u_ref[...] = u
    ubuf[c] = u[tm - SUBLANES:, :]


def _conv_attn(xb, w_in, conv_w, t_idx, fill1, fill2, q, k, v, sink_tab, layer, *, tm, tc, col0, d_conv,
               seq, n_kv, group, head_dim):
    m, d = xb.shape
    d_q = q.shape[1]
    d_kv = k.shape[1]
    nc = d_conv // tc
    off = col0 // tc
    n_blocks = tm // WINDOW
    w_spec = lambda kk: pl.BlockSpec((d, tc), lambda i, c: (0, off + kk * nc + c))
    tile = lambda i, c: (i, 0)
    before = lambda i, c: (jnp.maximum(i * n_blocks - 1, 0), 0)
    return pl.pallas_call(
        functools.partial(_conv_attn_kernel, nc, seq // WINDOW, n_kv, group, head_dim),
        grid=(pl.cdiv(m, tm), nc),
        in_specs=[
            pl.BlockSpec((tm, d), tile),
            w_spec(0), w_spec(1), w_spec(2),
            pl.BlockSpec((None, CONV_W, tc), lambda i, c: (layer, 0, c)),
            pl.BlockSpec((tm, 1), tile),
            pl.BlockSpec((tm, tc), lambda i, c: (0, c)),
            pl.BlockSpec((tm, tc), lambda i, c: (0, c)),
            pl.BlockSpec((None, n_kv * group, 1, LANES), lambda i, c: (layer, 0, 0, 0)),
            pl.BlockSpec((tm, d_q), tile),
            pl.BlockSpec((WINDOW, d_kv), before),
            pl.BlockSpec((tm, d_kv), tile),
            pl.BlockSpec((WINDOW, d_kv), before),
            pl.BlockSpec((tm, d_kv), tile),
        ],
        out_specs=[
            pl.BlockSpec((tm, tc), lambda i, c: (i, c)),
            pl.BlockSpec((tm, tc), lambda i, c: (i, c)),
            pl.BlockSpec((tm, d_q), tile),
        ],
        out_shape=[
            jax.ShapeDtypeStruct((m, d_conv), BF16),
            jax.ShapeDtypeStruct((m, d_conv), F32),
            jax.ShapeDtypeStruct((m, d_q), BF16),
        ],
        scratch_shapes=[pltpu.VMEM((nc, SUBLANES, tc), F32)],
        compiler_params=_params(2),
        name="conv_attn",
    )(xb, w_in, w_in, w_in, conv_w, t_idx, fill1, fill2, sink_tab, q, k, k, v, v)


def _mix_ln_kernel(alpha, x_ref, xb_ref, a_ref, c_ref, wga_ref, wgc_ref, wa_ref, wc_ref, wo_ref, g_ref,
                   b_ref, o_ref, acc_ref):
    j = pl.program_id(1)

    @pl.when(j == 0)
    def _():
        acc_ref[...] = jnp.zeros_like(acc_ref)

    xb = xb_ref[...]
    ga = jnp.dot(xb, wga_ref[...], preferred_element_type=F32)
    gc = jnp.dot(xb, wgc_ref[...], preferred_element_type=F32)
    pa = jnp.dot(a_ref[...], wa_ref[...], preferred_element_type=F32)
    pc = jnp.dot(c_ref[...], wc_ref[...], preferred_element_type=F32)
    merged = (jax.nn.sigmoid(ga) * pa + jax.nn.sigmoid(gc) * pc).astype(BF16)
    acc_ref[...] += jnp.dot(merged, wo_ref[...], preferred_element_type=F32)

    @pl.when(j == pl.num_programs(1) - 1)
    def _():
        y = alpha * x_ref[...] + acc_ref[...]
        o_ref[...] = _layer_norm(y, g_ref[...], b_ref[...])


def _mix_ln(x, xb, attn, yconv, w_in, w_a, w_c, w_o, ln_g, ln_b, ln_idx, alpha, *, tm, tc, ga_col0,
            gc_col0):
    m, d = x.shape
    d_q = attn.shape[1]
    d_conv = yconv.shape[1]
    return pl.pallas_call(
        functools.partial(_mix_ln_kernel, alpha),
        grid=(m // tm, d // tc),
        in_specs=[
            pl.BlockSpec((tm, d), lambda i, j: (i, 0)),
            pl.BlockSpec((tm, d), lambda i, j: (i, 0)),
            pl.BlockSpec((tm, d_q), lambda i, j: (i, 0)),
            pl.BlockSpec((tm, d_conv), lambda i, j: (i, 0)),
            pl.BlockSpec((d, tc), lambda i, j: (0, ga_col0 // tc + j)),
            pl.BlockSpec((d, tc), lambda i, j: (0, gc_col0 // tc + j)),
            pl.BlockSpec((d_q, tc), lambda i, j: (0, j)),
            pl.BlockSpec((d_conv, tc), lambda i, j: (0, j)),
            pl.BlockSpec((tc, d), lambda i, j: (j, 0)),
            pl.BlockSpec((None, 1, d), lambda i, j: (ln_idx, 0, 0)),
            pl.BlockSpec((None, 1, d), lambda i, j: (ln_idx, 0, 0)),
        ],
        out_specs=pl.BlockSpec((tm, d), lambda i, j: (i, 0)),
        out_shape=jax.ShapeDtypeStruct((m, d), F32),
        scratch_shapes=[pltpu.VMEM((tm, d), F32)],
        compiler_params=_params(2),
        name="mix_ln",
    )(x, xb, attn, yconv, w_in, w_in, w_a, w_c, w_o, ln_g, ln_b)


def _rope_tables(pos, head_dim):
    inv_freq = ROPE_THETA ** (-jnp.arange(0, head_dim, 2, dtype=F32) / head_dim)
    ang = pos.astype(F32)[:, None] * inv_freq[None, :]
    cos = jnp.cos(ang)
    sin = jnp.sin(ang)
    reps = LANES // head_dim
    cos_tab = jnp.tile(jnp.concatenate([cos, cos], axis=1), (1, reps))
    sin_tab = jnp.tile(jnp.concatenate([-sin, sin], axis=1), (1, reps))
    return cos_tab, sin_tab


def _largest_divisor(n, cap, multiple_of=1):
    return max(k for k in range(multiple_of, cap + 1, multiple_of) if n % k == 0)


def _seq_tails(a, n_seq, seq, rows):
    return jnp.stack([a[(b + 1) * seq - rows:(b + 1) * seq] for b in range(n_seq)])


def kernel(x_prompt, x_sample, cache_k_win, cache_v_win, state_conv, ln_g, ln_b, w_in, sinks, conv_w,
           w_branch_attn, w_branch_conv, w_out, ffn1_gu, ffn1_down, ffn2_gu, ffn2_down):
    depth = w_in.shape[0]
    bp, tp, d = x_prompt.shape
    bs, ts, _ = x_sample.shape
    n_kv, head_dim = cache_k_win.shape[-2:]
    d_q = w_branch_attn.shape[1]
    d_conv = conv_w.shape[-1]
    d_kv = n_kv * head_dim
    group = d_q // d_kv
    mp, ms = bp * tp, bs * ts
    m = mp + ms
    alpha = (2.0 * depth) ** 0.25
    tm = _largest_divisor(m, MAX_ROW_TILE, BF16_ROWS)
    ns = _largest_divisor(bs, SAMPLE_SEQS_PER_STEP, SUBLANES // ts)
    assert tm >= ms and SUBLANES % ts == 0 and mp % (ns * ts) == 0 and tp % WINDOW == 0
    assert ts >= CONV_W - 1 and tp >= WINDOW
    tf = 512
    tc = 512
    tm_mix = MIX_ROW_TILE
    assert mp % tm_mix == 0 and ms <= tm_mix and tm_mix % (WINDOW * (d_conv // tc)) == 0
    conv_col0 = d_q + 2 * d_kv
    ga_col0 = conv_col0 + 3 * d_conv
    gc_col0 = ga_col0 + d

    ln_g = ln_g.reshape(depth * 3, 1, d)
    ln_b = ln_b.reshape(depth * 3, 1, d)
    sink_tab = jnp.broadcast_to(sinks[:, :, None, None], sinks.shape + (1, LANES))

    t_prompt = jnp.tile(jnp.arange(tp, dtype=jnp.int32), bp)
    t_sample = jnp.tile(jnp.arange(ts, dtype=jnp.int32), bs)
    cos_tab, sin_tab = _rope_tables(jnp.concatenate([t_prompt, PAST_LEN + t_sample]), head_dim)
    t_idx = jnp.concatenate([t_prompt, t_sample]).reshape(m, 1)
    k_cache = cache_k_win.reshape(depth, bs, WINDOW, d_kv)
    v_cache = cache_v_win.reshape(depth, bs, WINDOW, d_kv)

    f1_gu, f1_down = ffn1_gu[0].astype(BF16), ffn1_down[0].astype(BF16)
    x = x_prompt.reshape(mp, d)
    x_tail = x_sample.reshape(ms, d)
    ks_p, vs_p, cs_p, ks_s, vs_s, cs_s = [], [], [], [], [], []
    for l in range(depth):
        jobs = [(w, l) for w in (w_in, w_branch_attn, w_branch_conv, w_out, ffn2_gu, ffn2_down)]
        x1, x1b, w_in_b, w_a_b, w_c_b, w_o_b, f2_gu, f2_down = _ffn_ln(
            x, f1_gu, f1_down, ln_g, ln_b, 3 * l, alpha, tm=tm, tf=tf, emit_bf16=True, x_tail=x_tail,
            cast_jobs=jobs)
        q, k, v = _qkv_proj(x1b, w_in_b, cos_tab, sin_tab, tm=tm, d_q=d_q, d_kv=d_kv, head_dim=head_dim)
        st = state_conv[l]
        zeros = jnp.zeros((bs, ts, d_conv), F32)
        fill1 = zeros.at[:, 0].set(st[:, 1]).reshape(ms, d_conv)
        fill2 = zeros.at[:, 0].set(st[:, 0]).at[:, 1].set(st[:, 1]).reshape(ms, d_conv)
        fill1 = jnp.pad(fill1, ((0, tm_mix - ms), (0, 0)))
        fill2 = jnp.pad(fill2, ((0, tm_mix - ms), (0, 0)))
        yconv, u, attn = _conv_attn(x1b, w_in_b, conv_w, t_idx, fill1, fill2, q, k, v, sink_tab, l,
                                    tm=tm_mix, tc=tc, col0=conv_col0, d_conv=d_conv, seq=tp, n_kv=n_kv,
                                    group=group, head_dim=head_dim)
        attn = _attn_sample(attn, q, k, v, k_cache, v_cache, sink_tab, l, row0=mp, seq=ts,
                            seqs_per_step=ns, n_kv=n_kv, group=group, head_dim=head_dim)
        x2 = _mix_ln(x1, x1b, attn, yconv, w_in_b, w_a_b, w_c_b, w_o_b, ln_g, ln_b, 3 * l + 1, alpha,
                     tm=tm, tc=tc, ga_col0=ga_col0, gc_col0=gc_col0)
        if l + 1 < depth:
            x, f1_gu, f1_down = _ffn_ln(x2, f2_gu, f2_down, ln_g, ln_b, 3 * l + 2, alpha, tm=tm, tf=tf,
                                        cast_jobs=[(ffn1_gu, l + 1), (ffn1_down, l + 1)])
            x_tail = None
        else:
            y_p, y_s = _ffn_ln(x2, f2_gu, f2_down, ln_g, ln_b, 3 * l + 2, alpha, tm=tm, tf=tf,
                               split_rows=mp)

        ks_p.append(_seq_tails(k, bp, tp, WINDOW).reshape(bp, WINDOW, n_kv, head_dim))
        vs_p.append(_seq_tails(v, bp, tp, WINDOW).reshape(bp, WINDOW, n_kv, head_dim))
        cs_p.append(_seq_tails(u, bp, tp, CONV_W - 1))
        k_new = k[mp:].reshape(bs, ts, n_kv, head_dim)
        v_new = v[mp:].reshape(bs, ts, n_kv, head_dim)
        ks_s.append(jnp.concatenate([cache_k_win[l][:, ts:], k_new], axis=1))
        vs_s.append(jnp.concatenate([cache_v_win[l][:, ts:], v_new], axis=1))
        cs_s.append(u[mp:].reshape(bs, ts, d_conv)[:, -(CONV_W - 1):])

    return (y_p.reshape(bp, tp, d), y_s.reshape(bs, ts, d), jnp.stack(ks_p), jnp.stack(vs_p),
            jnp.stack(cs_p), jnp.stack(ks_s), jnp.stack(vs_s), jnp.stack(cs_s))
```

```python
import functools

import jax
import jax.numpy as jnp
from jax import lax
from jax.experimental import pallas as pl
from jax.experimental.pallas import tpu as pltpu

F32 = jnp.float32
BF16 = jnp.bfloat16

PAST_LEN = 16384
WINDOW = 128
ROPE_THETA = 10000.0
LN_EPS = 1e-5
CONV_W = 3

LANES = 128
SUBLANES = 8
BF16_ROWS = 16
VMEM_LIMIT_BYTES = 62 * 1024 * 1024

KEY_SLOTS = 2 * WINDOW
MAX_ROW_TILE = 640
WEIGHT_SLOTS = 3
MIX_ROW_TILE = 512
SAMPLE_SEQS_PER_STEP = 8


def _params(n_axes):
    return pltpu.CompilerParams(
        dimension_semantics=("arbitrary",) * n_axes,
        vmem_limit_bytes=VMEM_LIMIT_BYTES,
    )


def _layer_norm(y, g, b):
    mu = jnp.mean(y, axis=-1, keepdims=True)
    d = y - mu
    var = jnp.mean(d * d, axis=-1, keepdims=True)
    return d * lax.rsqrt(var + LN_EPS) * g + b


def _round_up(n, k):
    return -(-n // k) * k


def _ffn_ln_kernel(alpha, has_tail, emit_bf16, split, n_jobs, *refs):
    refs = list(refs)
    x_ref = refs.pop(0)
    xt_ref = refs.pop(0) if has_tail else None
    wg_ref, wu_ref, wd_ref, g_ref, b_ref = refs[:5]
    del refs[:5]
    job_src = refs[:n_jobs]
    del refs[:n_jobs]
    o_ref = refs.pop(0)
    ob_ref = refs.pop(0) if emit_bf16 else None
    ot_ref = refs.pop(0) if split else None
    job_dst = refs[:n_jobs]
    del refs[:n_jobs]
    xb_ref, wg_buf, wu_buf, wd_buf, sems = refs
    acc_ref = o_ref
    i = pl.program_id(0)
    j = pl.program_id(1)
    nj = pl.num_programs(1)
    last_tile = i == pl.num_programs(0) - 1
    tm = x_ref.shape[0]
    tf = wg_buf.shape[2]

    step = i * nj + j
    n_steps = pl.num_programs(0) * nj

    def chunk_copies(chunk, slot):
        return (
            pltpu.make_async_copy(wg_ref.at[:, pl.ds(chunk * tf, tf)], wg_buf.at[slot], sems.at[slot, 0]),
            pltpu.make_async_copy(wu_ref.at[:, pl.ds((nj + chunk) * tf, tf)], wu_buf.at[slot],
                                  sems.at[slot, 1]),
            pltpu.make_async_copy(wd_ref.at[pl.ds(chunk * tf, tf), :], wd_buf.at[slot], sems.at[slot, 2]),
        )

    def start_step(t):
        for cp in chunk_copies(lax.rem(t, nj), lax.rem(t, WEIGHT_SLOTS)):
            cp.start()

    @pl.when(step == 0)
    def _():
        for t in range(WEIGHT_SLOTS - 1):
            start_step(jnp.int32(t))

    ahead = step + (WEIGHT_SLOTS - 1)

    @pl.when(ahead < n_steps)
    def _():
        start_step(ahead)

    slot = lax.rem(step, WEIGHT_SLOTS)
    for cp in chunk_copies(j, slot):
        cp.wait()

    def load_x():
        x = x_ref[...]
        if xt_ref is None:
            return x
        merged = jnp.concatenate([x[:tm - xt_ref.shape[0]], xt_ref[...]], axis=0)
        return jnp.where(last_tile, merged, x)

    @pl.when(j == 0)
    def _():
        xb_ref[...] = load_x().astype(BF16)
        acc_ref[...] = jnp.zeros_like(acc_ref)

    xb = xb_ref[...]
    g = jnp.dot(xb, wg_buf[slot], preferred_element_type=F32)
    u = jnp.dot(xb, wu_buf[slot], preferred_element_type=F32)
    h = (jax.nn.silu(g) * u).astype(BF16)
    acc_ref[...] += jnp.dot(h, wd_buf[slot], preferred_element_type=F32)
    for src, dst in zip(job_src, job_dst):
        dst[...] = src[...].astype(BF16)

    @pl.when(j == pl.num_programs(1) - 1)
    def _():
        y = alpha * load_x() + 0.5 * acc_ref[...]
        out = _layer_norm(y, g_ref[...], b_ref[...])
        o_ref[...] = out
        if emit_bf16:
            ob_ref[...] = out.astype(BF16)
        if split:
            @pl.when(last_tile)
            def _():
                ot_ref[...] = out[tm - ot_ref.shape[0]:, :]


def _ffn_ln(x, w_gu, w_down, ln_g, ln_b, ln_idx, alpha, *, tm, tf, emit_bf16=False, x_tail=None,
            split_rows=None, cast_jobs=()):
    d = x.shape[1]
    m = x.shape[0] + (0 if x_tail is None else x_tail.shape[0])
    f = w_down.shape[0]
    nj = f // tf
    n_tiles = m // tm
    row = lambda i, j: (i, 0)
    fixed = lambda i, j: (0, 0)
    in_specs = [pl.BlockSpec((tm, d), row)]
    args = [x]
    if x_tail is not None:
        in_specs.append(pl.BlockSpec(x_tail.shape, fixed))
        args.append(x_tail)
    in_specs += [
        pl.BlockSpec(memory_space=pl.ANY),
        pl.BlockSpec(memory_space=pl.ANY),
        pl.BlockSpec(memory_space=pl.ANY),
        pl.BlockSpec((None, 1, d), lambda i, j: (ln_idx, 0, 0)),
        pl.BlockSpec((None, 1, d), lambda i, j: (ln_idx, 0, 0)),
    ]
    args += [w_gu, w_gu, w_down, ln_g, ln_b]
    m_head = m if split_rows is None else split_rows
    out_shape = [jax.ShapeDtypeStruct((m_head, d), F32)]
    out_specs = [pl.BlockSpec((tm, d), row)]
    if emit_bf16:
        out_shape.append(jax.ShapeDtypeStruct((m, d), BF16))
        out_specs.append(pl.BlockSpec((tm, d), row))
    if split_rows is not None:
        out_shape.append(jax.ShapeDtypeStruct((m - split_rows, d), F32))
        out_specs.append(pl.BlockSpec((m - split_rows, d), fixed))
    n_steps = n_tiles * nj
    for w, layer in cast_jobs:
        _, r, c = w.shape
        rb = _round_up(-(-r // n_steps), BF16_ROWS)
        nb = -(-r // rb)
        block = lambda i, j, nb=nb: (jnp.minimum(i * nj + j, nb - 1), 0)
        in_specs.append(pl.BlockSpec((None, rb, c), lambda i, j, nb=nb, layer=layer:
                                     (layer, jnp.minimum(i * nj + j, nb - 1), 0)))
        args.append(w)
        out_shape.append(jax.ShapeDtypeStruct((r, c), BF16))
        out_specs.append(pl.BlockSpec((rb, c), block))
    return pl.pallas_call(
        functools.partial(_ffn_ln_kernel, alpha, x_tail is not None, emit_bf16, split_rows is not None,
                          len(cast_jobs)),
        grid=(n_tiles, nj),
        in_specs=in_specs,
        out_specs=out_specs,
        out_shape=out_shape,
        scratch_shapes=[
            pltpu.VMEM((tm, d), BF16),
            pltpu.VMEM((WEIGHT_SLOTS, d, tf), BF16),
            pltpu.VMEM((WEIGHT_SLOTS, d, tf), BF16),
            pltpu.VMEM((WEIGHT_SLOTS, tf, d), BF16),
            pltpu.SemaphoreType.DMA((WEIGHT_SLOTS, 3)),
        ],
        compiler_params=_params(2),
        name="ffn_ln",
    )(*args)


def _rope_cols(x, cos, sin_signed, first_half, half):
    fwd = pltpu.roll(x, x.shape[1] - half, 1)
    bwd = pltpu.roll(x, half, 1)
    return x * cos + jnp.where(first_half, fwd, bwd) * sin_signed


def _qkv_kernel(d_q, d_kv, head_dim, xb_ref, w_ref, cos_ref, sin_ref, q_ref, k_ref, v_ref):
    xb = xb_ref[...]
    bounds = [0, d_q // 2, d_q, d_q + 2 * d_kv]
    parts = [jnp.dot(xb, w_ref[:, lo:hi], preferred_element_type=F32) for lo, hi in zip(bounds, bounds[1:])]
    q = jnp.concatenate(parts[:2], axis=1)
    kv = parts[2]
    cos = cos_ref[...]
    sin = sin_ref[...]
    half = head_dim // 2
    lane = lax.broadcasted_iota(jnp.int32, cos.shape, 1)
    first_half = (lane % head_dim) < half
    q_scale = head_dim ** -0.5
    for c in range(d_q // LANES):
        sl = slice(c * LANES, (c + 1) * LANES)
        q_ref[:, sl] = (_rope_cols(q[:, sl], cos, sin, first_half, half) * q_scale).astype(BF16)
    for c in range(d_kv // LANES):
        sl = slice(c * LANES, (c + 1) * LANES)
        k_ref[:, sl] = _rope_cols(kv[:, sl], cos, sin, first_half, half)
    v_ref[...] = kv[:, d_kv:]


def _qkv_proj(xb, w_in, cos_tab, sin_tab, *, tm, d_q, d_kv, head_dim):
    m, d = xb.shape
    n_cols = d_q + 2 * d_kv
    return pl.pallas_call(
        functools.partial(_qkv_kernel, d_q, d_kv, head_dim),
        grid=(m // tm,),
        in_specs=[
            pl.BlockSpec((tm, d), lambda i: (i, 0)),
            pl.BlockSpec((d, n_cols), lambda i: (0, 0)),
            pl.BlockSpec((tm, LANES), lambda i: (i, 0)),
            pl.BlockSpec((tm, LANES), lambda i: (i, 0)),
        ],
        out_specs=[
            pl.BlockSpec((tm, d_q), lambda i: (i, 0)),
            pl.BlockSpec((tm, d_kv), lambda i: (i, 0)),
            pl.BlockSpec((tm, d_kv), lambda i: (i, 0)),
        ],
        out_shape=[
            jax.ShapeDtypeStruct((m, d_q), BF16),
            jax.ShapeDtypeStruct((m, d_kv), F32),
            jax.ShapeDtypeStruct((m, d_kv), F32),
        ],
        compiler_params=_params(1),
        name="qkv_proj",
    )(xb, w_in, cos_tab, sin_tab)


def _lane_tile4(x128, want_high):
    lane = lax.broadcasted_iota(jnp.int32, x128.shape, 1)
    swapped = pltpu.roll(x128, LANES // 2, 1)
    low = lane < LANES // 2
    both = jnp.where(low, swapped, x128) if want_high else jnp.where(low, x128, swapped)
    return jnp.concatenate([both, both], axis=1)


def _band_attention(problems, sink, n_kv, group, head_dim, between=None):
    rows = problems[0][0].shape[0]
    gw = group * head_dim
    r_idx = lax.broadcasted_iota(jnp.int32, (rows, KEY_SLOTS), 0)
    s_idx = lax.broadcasted_iota(jnp.int32, (rows, KEY_SLOTS), 1)
    head_of_lane = lax.broadcasted_iota(jnp.int32, (rows, gw), 1) // head_dim
    head_keep = [(head_of_lane == g).astype(F32) for g in range(group)]

    scores, values = [], []
    for q, kband, vband, row0, kpos0 in problems:
        diff = r_idx - row0 + WINDOW - s_idx
        valid = (diff >= 0) & (diff <= WINDOW) & (s_idx + kpos0 >= 0)
        for kh in range(n_kv):
            col = (kh * head_dim) // LANES
            high = ((kh * head_dim) % LANES) != 0
            kk = _lane_tile4(kband[:, col * LANES:(col + 1) * LANES], high).astype(BF16)
            values.append(_lane_tile4(vband[:, col * LANES:(col + 1) * LANES], high).astype(BF16))
            qg = q[:, kh * gw:(kh + 1) * gw]
            qs = jnp.concatenate([qg * head_keep[g] for g in range(group)], axis=0).astype(BF16)
            s = lax.dot_general(qs, kk, (((1,), (1,)), ((), ())), preferred_element_type=F32)
            scores.append(jnp.where(valid[None], s.reshape(group, rows, KEY_SLOTS), -jnp.inf))
    s = jnp.concatenate(scores, axis=0)
    sink_all = jnp.concatenate([sink] * len(problems), axis=0)
    mx = jnp.maximum(jnp.max(s, axis=-1, keepdims=True), sink_all)
    p = jnp.exp(s - mx)
    den = jnp.sum(p, axis=-1, keepdims=True) + jnp.exp(sink_all - mx)
    p = p * (1.0 / den)

    if between is not None:
        between()

    outs = []
    for i in range(len(problems)):
        slabs = []
        for kh in range(n_kv):
            c = i * n_kv + kh
            pg = p[c * group:(c + 1) * group].reshape(group * rows, KEY_SLOTS).astype(BF16)
            o = jnp.dot(pg, values[c], preferred_element_type=F32).reshape(group, rows, gw)
            out = o[0]
            for g in range(1, group):
                out = jnp.where(head_of_lane == g, o[g], out)
            slabs.append(out)
        outs.append(jnp.concatenate(slabs, axis=1))
    return outs


def _sink_column(sink_ref):
    return sink_ref[...][:, :, 0:1]


def _attn_sample_kernel(seq, n_kv, group, head_dim, sink_ref, q_ref, kc_ref, kn_ref, vc_ref, vn_ref,
                        _, o_ref):
    n_seq = kc_ref.shape[0]
    d_kv = kn_ref.shape[1]
    per_block = SUBLANES // seq
    pad = jnp.zeros((KEY_SLOTS - WINDOW - SUBLANES, d_kv), F32)
    qf = q_ref[...].astype(F32)
    problems = []
    for i in range(n_seq):
        blk, row0 = i // per_block, (i % per_block) * seq
        rows = slice(blk * SUBLANES, (blk + 1) * SUBLANES)
        k8, v8 = kn_ref[rows, :], vn_ref[rows, :]
        if row0:
            k8 = pltpu.roll(k8, SUBLANES - row0, 0)
            v8 = pltpu.roll(v8, SUBLANES - row0, 0)
        kband = jnp.concatenate([kc_ref[i], k8, pad], axis=0)
        vband = jnp.concatenate([vc_ref[i], v8, pad], axis=0)
        problems.append((qf[rows, :], kband, vband, row0, PAST_LEN - WINDOW))
    outs = _band_attention(problems, _sink_column(sink_ref), n_kv, group, head_dim)
    r_idx = lax.broadcasted_iota(jnp.int32, outs[0].shape, 0)
    blocks = []
    for blk in range(n_seq // per_block):
        out = outs[blk * per_block]
        for s in range(1, per_block):
            out = jnp.where(r_idx >= s * seq, outs[blk * per_block + s], out)
        blocks.append(out)
    o_ref[...] = jnp.concatenate(blocks, axis=0).astype(BF16)


def _attn_sample(attn, q, k, v, k_cache, v_cache, sink_tab, layer, *, row0, seq, seqs_per_step, n_kv,
                 group, head_dim):
    d_q = q.shape[1]
    d_kv = k.shape[1]
    nbatch = k_cache.shape[1]
    ns = seqs_per_step
    rows = ns * seq
    blk0 = row0 // rows
    new = lambda b: (blk0 + b, 0)
    cache = lambda b: (layer, b, 0, 0)
    return pl.pallas_call(
        functools.partial(_attn_sample_kernel, seq, n_kv, group, head_dim),
        grid=(nbatch // ns,),
        in_specs=[
            pl.BlockSpec((None, n_kv * group, 1, LANES), lambda b: (layer, 0, 0, 0)),
            pl.BlockSpec((rows, d_q), new),
            pl.BlockSpec((None, ns, WINDOW, d_kv), cache),
            pl.BlockSpec((rows, d_kv), new),
            pl.BlockSpec((None, ns, WINDOW, d_kv), cache),
            pl.BlockSpec((rows, d_kv), new),
            pl.BlockSpec(memory_space=pl.ANY),
        ],
        out_specs=pl.BlockSpec((rows, d_q), new),
        out_shape=jax.ShapeDtypeStruct(attn.shape, attn.dtype),
        input_output_aliases={6: 0},
        compiler_params=_params(1),
        name="attn_sample",
    )(sink_tab, q, k_cache, k, v_cache, v, attn)


def _conv_attn_kernel(nc, col0, blocks_per_seq, n_kv, group, head_dim,
                      xb_ref, w_hbm, cw_ref, t_ref, f1_ref, f2_ref,
                      sink_ref, q_ref, kp_ref, kc_ref, vp_ref, vc_ref,
                      y_ref, u_ref, o_ref, ubuf, w_res, w_sems):
    tm = xb_ref.shape[0]
    tc = w_res.shape[2]
    i = pl.program_id(0)
    c = pl.program_id(1)

    @pl.when((i == 0) & (c == 0))
    def _():
        copies = [pltpu.make_async_copy(w_hbm.at[:, pl.ds(col0 + b * tc, tc)], w_res.at[b], w_sems.at[b])
                  for b in range(w_res.shape[0])]
        for cp in copies:
            cp.start()
        for cp in copies:
            cp.wait()

    wb_ref, wc_ref, wh_ref = (w_res.at[k * nc + c] for k in range(3))

    @pl.when(i == 0)
    def _():
        ubuf[c] = jnp.zeros(ubuf.shape[1:], F32)

    carry = ubuf[c]

    n_blocks = tm // WINDOW
    per_step = n_blocks // nc
    problems, starts = [], []
    for s in range(per_step):
        b = c * per_step + s
        r0 = pl.multiple_of(b * WINDOW, WINDOW)
        rp = pl.multiple_of(jnp.maximum(b - 1, 0) * WINDOW, WINDOW)
        n = lax.rem(i * n_blocks + b, blocks_per_seq)
        k_before = jnp.where(b == 0, kp_ref[...], kc_ref[pl.ds(rp, WINDOW), :])
        v_before = jnp.where(b == 0, vp_ref[...], vc_ref[pl.ds(rp, WINDOW), :])
        kband = jnp.concatenate([k_before, kc_ref[pl.ds(r0, WINDOW), :]], axis=0)
        vband = jnp.concatenate([v_before, vc_ref[pl.ds(r0, WINDOW), :]], axis=0)
        problems.append((q_ref[pl.ds(r0, WINDOW), :].astype(F32), kband, vband, 0, (n - 1) * WINDOW))
        starts.append(r0)
    proj = []

    def conv_input_matmuls():
        xb = xb_ref[...]
        for w_ref in (wc_ref, wh_ref):
            proj.append(jnp.dot(xb, w_ref[...], preferred_element_type=F32))

    outs = _band_attention(problems, _sink_column(sink_ref), n_kv, group, head_dim,
                           between=conv_input_matmuls)
    cc, ch = proj
    cb = jnp.dot(xb_ref[...], wb_ref[...], preferred_element_type=F32)
    u = cc * ch
    row8 = lax.broadcasted_iota(jnp.int32, carry.shape, 0)

    def shifted(k):
        r = pltpu.roll(u, k, 0)
        head = jnp.where(row8 < k, pltpu.roll(carry, k, 0), r[0:SUBLANES])
        return jnp.concatenate([head, r[SUBLANES:]], axis=0)

    t = t_ref[...]
    last = i == pl.num_programs(0) - 1
    u_m1 = jnp.where(t >= 1, shifted(1), jnp.where(last, f1_ref[...], 0.0))
    u_m2 = jnp.where(t >= 2, shifted(2), jnp.where(last, f2_ref[...], 0.0))
    cw = cw_ref[...]
    conv = cw[0:1, :] * u_m2 + cw[1:2, :] * u_m1 + cw[2:3, :] * u

    for r0, out in zip(starts, outs):
        o_ref[pl.ds(r0, WINDOW), :] = out.astype(BF16)
    y_ref[...] = (cb * conv).astype(BF16)
    u_ref[...] = u
    ubuf[c] = u[tm - SUBLANES:, :]


def _conv_attn(xb, w_in, conv_w, t_idx, fill1, fill2, q, k, v, sink_tab, layer, *, tm, tc, col0, d_conv,
               seq, n_kv, group, head_dim):
    m, d = xb.shape
    d_q = q.shape[1]
    d_kv = k.shape[1]
    nc = d_conv // tc
    n_blocks = tm // WINDOW
    tile = lambda i, c: (i, 0)
    before = lambda i, c: (jnp.maximum(i * n_blocks - 1, 0), 0)
    return pl.pallas_call(
        functools.partial(_conv_attn_kernel, nc, col0, seq // WINDOW, n_kv, group, head_dim),
        grid=(pl.cdiv(m, tm), nc),
        in_specs=[
            pl.BlockSpec((tm, d), tile),
            pl.BlockSpec(memory_space=pl.ANY),
            pl.BlockSpec((None, CONV_W, tc), lambda i, c: (layer, 0, c)),
            pl.BlockSpec((tm, 1), tile),
            pl.BlockSpec((tm, tc), lambda i, c: (0, c)),
            pl.BlockSpec((tm, tc), lambda i, c: (0, c)),
            pl.BlockSpec((None, n_kv * group, 1, LANES), lambda i, c: (layer, 0, 0, 0)),
            pl.BlockSpec((tm, d_q), tile),
            pl.BlockSpec((WINDOW, d_kv), before),
            pl.BlockSpec((tm, d_kv), tile),
            pl.BlockSpec((WINDOW, d_kv), before),
            pl.BlockSpec((tm, d_kv), tile),
        ],
        out_specs=[
            pl.BlockSpec((tm, tc), lambda i, c: (i, c)),
            pl.BlockSpec((tm, tc), lambda i, c: (i, c)),
            pl.BlockSpec((tm, d_q), tile),
        ],
        out_shape=[
            jax.ShapeDtypeStruct((m, d_conv), BF16),
            jax.ShapeDtypeStruct((m, d_conv), F32),
            jax.ShapeDtypeStruct((m, d_q), BF16),
        ],
        scratch_shapes=[
            pltpu.VMEM((nc, SUBLANES, tc), F32),
            pltpu.VMEM((3 * nc, d, tc), BF16),
            pltpu.SemaphoreType.DMA((3 * nc,)),
        ],
        compiler_params=_params(2),
        name="conv_attn",
    )(xb, w_in, conv_w, t_idx, fill1, fill2, sink_tab, q, k, k, v, v)


def _mix_ln_kernel(alpha, x_ref, xb_ref, a_ref, c_ref, wga_ref, wgc_ref, wa_ref, wc_ref, wo_ref, g_ref,
                   b_ref, o_ref, acc_ref):
    j = pl.program_id(1)

    @pl.when(j == 0)
    def _():
        acc_ref[...] = jnp.zeros_like(acc_ref)

    xb = xb_ref[...]
    ga = jnp.dot(xb, wga_ref[...], preferred_element_type=F32)
    gc = jnp.dot(xb, wgc_ref[...], preferred_element_type=F32)
    pa = jnp.dot(a_ref[...], wa_ref[...], preferred_element_type=F32)
    pc = jnp.dot(c_ref[...], wc_ref[...], preferred_element_type=F32)
    merged = (jax.nn.sigmoid(ga) * pa + jax.nn.sigmoid(gc) * pc).astype(BF16)
    acc_ref[...] += jnp.dot(merged, wo_ref[...], preferred_element_type=F32)

    @pl.when(j == pl.num_programs(1) - 1)
    def _():
        y = alpha * x_ref[...] + acc_ref[...]
        o_ref[...] = _layer_norm(y, g_ref[...], b_ref[...])


def _mix_ln(x, xb, attn, yconv, w_in, w_a, w_c, w_o, ln_g, ln_b, ln_idx, alpha, *, tm, tc, ga_col0,
            gc_col0):
    m, d = x.shape
    d_q = attn.shape[1]
    d_conv = yconv.shape[1]
    return pl.pallas_call(
        functools.partial(_mix_ln_kernel, alpha),
        grid=(m // tm, d // tc),
        in_specs=[
            pl.BlockSpec((tm, d), lambda i, j: (i, 0)),
            pl.BlockSpec((tm, d), lambda i, j: (i, 0)),
            pl.BlockSpec((tm, d_q), lambda i, j: (i, 0)),
            pl.BlockSpec((tm, d_conv), lambda i, j: (i, 0)),
            pl.BlockSpec((d, tc), lambda i, j: (0, ga_col0 // tc + j)),
            pl.BlockSpec((d, tc), lambda i, j: (0, gc_col0 // tc + j)),
            pl.BlockSpec((d_q, tc), lambda i, j: (0, j)),
            pl.BlockSpec((d_conv, tc), lambda i, j: (0, j)),
            pl.BlockSpec((tc, d), lambda i, j: (j, 0)),
            pl.BlockSpec((None, 1, d), lambda i, j: (ln_idx, 0, 0)),
            pl.BlockSpec((None, 1, d), lambda i, j: (ln_idx, 0, 0)),
        ],
        out_specs=pl.BlockSpec((tm, d), lambda i, j: (i, 0)),
        out_shape=jax.ShapeDtypeStruct((m, d), F32),
        scratch_shapes=[pltpu.VMEM((tm, d), F32)],
        compiler_params=_params(2),
        name="mix_ln",
    )(x, xb, attn, yconv, w_in, w_in, w_a, w_c, w_o, ln_g, ln_b)


def _rope_tables(pos, head_dim):
    inv_freq = ROPE_THETA ** (-jnp.arange(0, head_dim, 2, dtype=F32) / head_dim)
    ang = pos.astype(F32)[:, None] * inv_freq[None, :]
    cos = jnp.cos(ang)
    sin = jnp.sin(ang)
    reps = LANES // head_dim
    cos_tab = jnp.tile(jnp.concatenate([cos, cos], axis=1), (1, reps))
    sin_tab = jnp.tile(jnp.concatenate([-sin, sin], axis=1), (1, reps))
    return cos_tab, sin_tab


def _largest_divisor(n, cap, multiple_of=1):
    return max(k for k in range(multiple_of, cap + 1, multiple_of) if n % k == 0)


def _seq_tails(a, n_seq, seq, rows):
    return jnp.stack([a[(b + 1) * seq - rows:(b + 1) * seq] for b in range(n_seq)])


def kernel(x_prompt, x_sample, cache_k_win, cache_v_win, state_conv, ln_g, ln_b, w_in, sinks, conv_w,
           w_branch_attn, w_branch_conv, w_out, ffn1_gu, ffn1_down, ffn2_gu, ffn2_down):
    depth = w_in.shape[0]
    bp, tp, d = x_prompt.shape
    bs, ts, _ = x_sample.shape
    n_kv, head_dim = cache_k_win.shape[-2:]
    d_q = w_branch_attn.shape[1]
    d_conv = conv_w.shape[-1]
    d_kv = n_kv * head_dim
    group = d_q // d_kv
    mp, ms = bp * tp, bs * ts
    m = mp + ms
    alpha = (2.0 * depth) ** 0.25
    tm = _largest_divisor(m, MAX_ROW_TILE, BF16_ROWS)
    ns = _largest_divisor(bs, SAMPLE_SEQS_PER_STEP, SUBLANES // ts)
    assert tm >= ms and SUBLANES % ts == 0 and mp % (ns * ts) == 0 and tp % WINDOW == 0
    assert ts >= CONV_W - 1 and tp >= WINDOW
    tf = 512
    tc = 512
    tm_mix = MIX_ROW_TILE
    assert mp % tm_mix == 0 and ms <= tm_mix and tm_mix % (WINDOW * (d_conv // tc)) == 0
    conv_col0 = d_q + 2 * d_kv
    ga_col0 = conv_col0 + 3 * d_conv
    gc_col0 = ga_col0 + d

    ln_g = ln_g.reshape(depth * 3, 1, d)
    ln_b = ln_b.reshape(depth * 3, 1, d)
    sink_tab = jnp.broadcast_to(sinks[:, :, None, None], sinks.shape + (1, LANES))

    t_prompt = jnp.tile(jnp.arange(tp, dtype=jnp.int32), bp)
    t_sample = jnp.tile(jnp.arange(ts, dtype=jnp.int32), bs)
    cos_tab, sin_tab = _rope_tables(jnp.concatenate([t_prompt, PAST_LEN + t_sample]), head_dim)
    t_idx = jnp.concatenate([t_prompt, t_sample]).reshape(m, 1)
    k_cache = cache_k_win.reshape(depth, bs, WINDOW, d_kv)
    v_cache = cache_v_win.reshape(depth, bs, WINDOW, d_kv)

    f1_gu, f1_down = ffn1_gu[0].astype(BF16), ffn1_down[0].astype(BF16)
    x = x_prompt.reshape(mp, d)
    x_tail = x_sample.reshape(ms, d)
    ks_p, vs_p, cs_p, ks_s, vs_s, cs_s = [], [], [], [], [], []
    for l in range(depth):
        jobs = [(w, l) for w in (w_in, w_branch_attn, w_branch_conv, w_out, ffn2_gu, ffn2_down)]
        x1, x1b, w_in_b, w_a_b, w_c_b, w_o_b, f2_gu, f2_down = _ffn_ln(
            x, f1_gu, f1_down, ln_g, ln_b, 3 * l, alpha, tm=tm, tf=tf, emit_bf16=True, x_tail=x_tail,
            cast_jobs=jobs)
        q, k, v = _qkv_proj(x1b, w_in_b, cos_tab, sin_tab, tm=tm, d_q=d_q, d_kv=d_kv, head_dim=head_dim)
        st = state_conv[l]
        zeros = jnp.zeros((bs, ts, d_conv), F32)
        fill1 = zeros.at[:, 0].set(st[:, 1]).reshape(ms, d_conv)
        fill2 = zeros.at[:, 0].set(st[:, 0]).at[:, 1].set(st[:, 1]).reshape(ms, d_conv)
        fill1 = jnp.pad(fill1, ((0, tm_mix - ms), (0, 0)))
        fill2 = jnp.pad(fill2, ((0, tm_mix - ms), (0, 0)))
        yconv, u, attn = _conv_attn(x1b, w_in_b, conv_w, t_idx, fill1, fill2, q, k, v, sink_tab, l,
                                    tm=tm_mix, tc=tc, col0=conv_col0, d_conv=d_conv, seq=tp, n_kv=n_kv,
                                    group=group, head_dim=head_dim)
        attn = _attn_sample(attn, q, k, v, k_cache, v_cache, sink_tab, l, row0=mp, seq=ts,
                            seqs_per_step=ns, n_kv=n_kv, group=group, head_dim=head_dim)
        x2 = _mix_ln(x1, x1b, attn, yconv, w_in_b, w_a_b, w_c_b, w_o_b, ln_g, ln_b, 3 * l + 1, alpha,
                     tm=tm, tc=tc, ga_col0=ga_col0, gc_col0=gc_col0)
        if l + 1 < depth:
            x, f1_gu, f1_down = _ffn_ln(x2, f2_gu, f2_down, ln_g, ln_b, 3 * l + 2, alpha, tm=tm, tf=tf,
                                        cast_jobs=[(ffn1_gu, l + 1), (ffn1_down, l + 1)])
            x_tail = None
        else:
            y_p, y_s = _ffn_ln(x2, f2_gu, f2_down, ln_g, ln_b, 3 * l + 2, alpha, tm=tm, tf=tf,
                               split_rows=mp)

        ks_p.append(_seq_tails(k, bp, tp, WINDOW).reshape(bp, WINDOW, n_kv, head_dim))
        vs_p.append(_seq_tails(v, bp, tp, WINDOW).reshape(bp, WINDOW, n_kv, head_dim))
        cs_p.append(_seq_tails(u, bp, tp, CONV_W - 1))
        k_new = k[mp:].reshape(bs, ts, n_kv, head_dim)
        v_new = v[mp:].reshape(bs, ts, n_kv, head_dim)
        ks_s.append(jnp.concatenate([cache_k_win[l][:, ts:], k_new], axis=1))
        vs_s.append(jnp.concatenate([cache_v_win[l][:, ts:], v_new], axis=1))
        cs_s.append(u[mp:].reshape(bs, ts, d_conv)[:, -(CONV_W - 1):])

    return (y_p.reshape(bp, tp, d), y_s.reshape(bs, ts, d), jnp.stack(ks_p), jnp.stack(vs_p),
            jnp.stack(cs_p), jnp.stack(ks_s), jnp.stack(vs_s), jnp.stack(cs_s))
```

```python
import functools

import jax
import jax.numpy as jnp
from jax import lax
from jax.experimental import pallas as pl
from jax.experimental.pallas import tpu as pltpu

F32 = jnp.float32
BF16 = jnp.bfloat16

PAST_LEN = 16384
WINDOW = 128
ROPE_THETA = 10000.0
LN_EPS = 1e-5
CONV_W = 3

LANES = 128
SUBLANES = 8
BF16_ROWS = 16
VMEM_LIMIT_BYTES = 62 * 1024 * 1024

KEY_SLOTS = 2 * WINDOW
MAX_ROW_TILE = 640
WEIGHT_SLOTS = 3
MIX_ROW_TILE = 512
SAMPLE_SEQS_PER_STEP = 8


def _params(n_axes):
    return pltpu.CompilerParams(
        dimension_semantics=("arbitrary",) * n_axes,
        vmem_limit_bytes=VMEM_LIMIT_BYTES,
    )


def _layer_norm(y, g, b):
    mu = jnp.mean(y, axis=-1, keepdims=True)
    d = y - mu
    var = jnp.mean(d * d, axis=-1, keepdims=True)
    return d * lax.rsqrt(var + LN_EPS) * g + b


def _round_up(n, k):
    return -(-n // k) * k


def _ffn_ln_kernel(alpha, has_tail, emit_bf16, split, n_jobs, *refs):
    refs = list(refs)
    x_ref = refs.pop(0)
    xt_ref = refs.pop(0) if has_tail else None
    wg_ref, wu_ref, wd_ref, g_ref, b_ref = refs[:5]
    del refs[:5]
    job_src = refs[:n_jobs]
    del refs[:n_jobs]
    o_ref = refs.pop(0)
    ob_ref = refs.pop(0) if emit_bf16 else None
    ot_ref = refs.pop(0) if split else None
    job_dst = refs[:n_jobs]
    del refs[:n_jobs]
    xb_ref, wg_buf, wu_buf, wd_buf, sems = refs
    acc_ref = o_ref
    i = pl.program_id(0)
    j = pl.program_id(1)
    nj = pl.num_programs(1)
    last_tile = i == pl.num_programs(0) - 1
    tm = x_ref.shape[0]
    tf = wg_buf.shape[2]

    step = i * nj + j
    n_steps = pl.num_programs(0) * nj

    def chunk_copies(chunk, slot):
        return (
            pltpu.make_async_copy(wg_ref.at[:, pl.ds(chunk * tf, tf)], wg_buf.at[slot], sems.at[slot, 0]),
            pltpu.make_async_copy(wu_ref.at[:, pl.ds((nj + chunk) * tf, tf)], wu_buf.at[slot],
                                  sems.at[slot, 1]),
            pltpu.make_async_copy(wd_ref.at[pl.ds(chunk * tf, tf), :], wd_buf.at[slot], sems.at[slot, 2]),
        )

    def start_step(t):
        for cp in chunk_copies(lax.rem(t, nj), lax.rem(t, WEIGHT_SLOTS)):
            cp.start(priority=1)

    @pl.when(step == 0)
    def _():
        for t in range(WEIGHT_SLOTS - 1):
            start_step(jnp.int32(t))

    ahead = step + (WEIGHT_SLOTS - 1)

    @pl.when(ahead < n_steps)
    def _():
        start_step(ahead)

    slot = lax.rem(step, WEIGHT_SLOTS)
    for cp in chunk_copies(j, slot):
        cp.wait()

    def load_x():
        x = x_ref[...]
        if xt_ref is None:
            return x
        merged = jnp.concatenate([x[:tm - xt_ref.shape[0]], xt_ref[...]], axis=0)
        return jnp.where(last_tile, merged, x)

    @pl.when(j == 0)
    def _():
        xb_ref[...] = load_x().astype(BF16)
        acc_ref[...] = jnp.zeros_like(acc_ref)

    xb = xb_ref[...]
    g = jnp.dot(xb, wg_buf[slot], preferred_element_type=F32)
    u = jnp.dot(xb, wu_buf[slot], preferred_element_type=F32)
    h = (jax.nn.silu(g) * u).astype(BF16)
    acc_ref[...] += jnp.dot(h, wd_buf[slot], preferred_element_type=F32)
    for src, dst in zip(job_src, job_dst):
        dst[...] = src[...].astype(BF16)

    @pl.when(j == pl.num_programs(1) - 1)
    def _():
        y = alpha * load_x() + 0.5 * acc_ref[...]
        out = _layer_norm(y, g_ref[...], b_ref[...])
        o_ref[...] = out
        if emit_bf16:
            ob_ref[...] = out.astype(BF16)
        if split:
            @pl.when(last_tile)
            def _():
                ot_ref[...] = out[tm - ot_ref.shape[0]:, :]


def _ffn_ln(x, w_gu, w_down, ln_g, ln_b, ln_idx, alpha, *, tm, tf, emit_bf16=False, x_tail=None,
            split_rows=None, cast_jobs=()):
    d = x.shape[1]
    m = x.shape[0] + (0 if x_tail is None else x_tail.shape[0])
    f = w_down.shape[0]
    nj = f // tf
    n_tiles = m // tm
    row = lambda i, j: (i, 0)
    fixed = lambda i, j: (0, 0)
    in_specs = [pl.BlockSpec((tm, d), row)]
    args = [x]
    if x_tail is not None:
        in_specs.append(pl.BlockSpec(x_tail.shape, fixed))
        args.append(x_tail)
    in_specs += [
        pl.BlockSpec(memory_space=pl.ANY),
        pl.BlockSpec(memory_space=pl.ANY),
        pl.BlockSpec(memory_space=pl.ANY),
        pl.BlockSpec((None, 1, d), lambda i, j: (ln_idx, 0, 0)),
        pl.BlockSpec((None, 1, d), lambda i, j: (ln_idx, 0, 0)),
    ]
    args += [w_gu, w_gu, w_down, ln_g, ln_b]
    m_head = m if split_rows is None else split_rows
    out_shape = [jax.ShapeDtypeStruct((m_head, d), F32)]
    out_specs = [pl.BlockSpec((tm, d), row)]
    if emit_bf16:
        out_shape.append(jax.ShapeDtypeStruct((m, d), BF16))
        out_specs.append(pl.BlockSpec((tm, d), row))
    if split_rows is not None:
        out_shape.append(jax.ShapeDtypeStruct((m - split_rows, d), F32))
        out_specs.append(pl.BlockSpec((m - split_rows, d), fixed))
    n_steps = n_tiles * nj
    for w, layer in cast_jobs:
        _, r, c = w.shape
        rb = _round_up(-(-r // n_steps), BF16_ROWS)
        nb = -(-r // rb)
        block = lambda i, j, nb=nb: (jnp.minimum(i * nj + j, nb - 1), 0)
        in_specs.append(pl.BlockSpec((None, rb, c), lambda i, j, nb=nb, layer=layer:
                                     (layer, jnp.minimum(i * nj + j, nb - 1), 0)))
        args.append(w)
        out_shape.append(jax.ShapeDtypeStruct((r, c), BF16))
        out_specs.append(pl.BlockSpec((rb, c), block))
    return pl.pallas_call(
        functools.partial(_ffn_ln_kernel, alpha, x_tail is not None, emit_bf16, split_rows is not None,
                          len(cast_jobs)),
        grid=(n_tiles, nj),
        in_specs=in_specs,
        out_specs=out_specs,
        out_shape=out_shape,
        scratch_shapes=[
            pltpu.VMEM((tm, d), BF16),
            pltpu.VMEM((WEIGHT_SLOTS, d, tf), BF16),
            pltpu.VMEM((WEIGHT_SLOTS, d, tf), BF16),
            pltpu.VMEM((WEIGHT_SLOTS, tf, d), BF16),
            pltpu.SemaphoreType.DMA((WEIGHT_SLOTS, 3)),
        ],
        compiler_params=_params(2),
        name="ffn_ln",
    )(*args)


def _rope_cols(x, cos, sin_signed, first_half, half):
    fwd = pltpu.roll(x, x.shape[1] - half, 1)
    bwd = pltpu.roll(x, half, 1)
    return x * cos + jnp.where(first_half, fwd, bwd) * sin_signed


def _qkv_kernel(d_q, d_kv, head_dim, xb_ref, w_ref, cos_ref, sin_ref, q_ref, k_ref, v_ref):
    xb = xb_ref[...]
    bounds = [0, d_q // 2, d_q, d_q + 2 * d_kv]
    parts = [jnp.dot(xb, w_ref[:, lo:hi], preferred_element_type=F32) for lo, hi in zip(bounds, bounds[1:])]
    q = jnp.concatenate(parts[:2], axis=1)
    kv = parts[2]
    cos = cos_ref[...]
    sin = sin_ref[...]
    half = head_dim // 2
    lane = lax.broadcasted_iota(jnp.int32, cos.shape, 1)
    first_half = (lane % head_dim) < half
    q_scale = head_dim ** -0.5
    for c in range(d_q // LANES):
        sl = slice(c * LANES, (c + 1) * LANES)
        q_ref[:, sl] = (_rope_cols(q[:, sl], cos, sin, first_half, half) * q_scale).astype(BF16)
    for c in range(d_kv // LANES):
        sl = slice(c * LANES, (c + 1) * LANES)
        k_ref[:, sl] = _rope_cols(kv[:, sl], cos, sin, first_half, half)
    v_ref[...] = kv[:, d_kv:]


def _qkv_proj(xb, w_in, cos_tab, sin_tab, *, tm, d_q, d_kv, head_dim):
    m, d = xb.shape
    n_cols = d_q + 2 * d_kv
    return pl.pallas_call(
        functools.partial(_qkv_kernel, d_q, d_kv, head_dim),
        grid=(m // tm,),
        in_specs=[
            pl.BlockSpec((tm, d), lambda i: (i, 0)),
            pl.BlockSpec((d, n_cols), lambda i: (0, 0)),
            pl.BlockSpec((tm, LANES), lambda i: (i, 0)),
            pl.BlockSpec((tm, LANES), lambda i: (i, 0)),
        ],
        out_specs=[
            pl.BlockSpec((tm, d_q), lambda i: (i, 0)),
            pl.BlockSpec((tm, d_kv), lambda i: (i, 0)),
            pl.BlockSpec((tm, d_kv), lambda i: (i, 0)),
        ],
        out_shape=[
            jax.ShapeDtypeStruct((m, d_q), BF16),
            jax.ShapeDtypeStruct((m, d_kv), F32),
            jax.ShapeDtypeStruct((m, d_kv), F32),
        ],
        compiler_params=_params(1),
        name="qkv_proj",
    )(xb, w_in, cos_tab, sin_tab)


def _lane_tile4(x128, want_high):
    lane = lax.broadcasted_iota(jnp.int32, x128.shape, 1)
    swapped = pltpu.roll(x128, LANES // 2, 1)
    low = lane < LANES // 2
    both = jnp.where(low, swapped, x128) if want_high else jnp.where(low, x128, swapped)
    return jnp.concatenate([both, both], axis=1)


def _band_attention(problems, sink, n_kv, group, head_dim, between=None):
    rows = problems[0][0].shape[0]
    gw = group * head_dim
    r_idx = lax.broadcasted_iota(jnp.int32, (rows, KEY_SLOTS), 0)
    s_idx = lax.broadcasted_iota(jnp.int32, (rows, KEY_SLOTS), 1)
    head_of_lane = lax.broadcasted_iota(jnp.int32, (rows, gw), 1) // head_dim
    head_keep = [(head_of_lane == g).astype(F32) for g in range(group)]

    scores, values = [], []
    for q, kband, vband, row0, kpos0 in problems:
        diff = r_idx - row0 + WINDOW - s_idx
        valid = (diff >= 0) & (diff <= WINDOW) & (s_idx + kpos0 >= 0)
        for kh in range(n_kv):
            col = (kh * head_dim) // LANES
            high = ((kh * head_dim) % LANES) != 0
            kk = _lane_tile4(kband[:, col * LANES:(col + 1) * LANES], high).astype(BF16)
            values.append(_lane_tile4(vband[:, col * LANES:(col + 1) * LANES], high).astype(BF16))
            qg = q[:, kh * gw:(kh + 1) * gw]
            qs = jnp.concatenate([qg * head_keep[g] for g in range(group)], axis=0).astype(BF16)
            s = lax.dot_general(qs, kk, (((1,), (1,)), ((), ())), preferred_element_type=F32)
            scores.append(jnp.where(valid[None], s.reshape(group, rows, KEY_SLOTS), -jnp.inf))
    s = jnp.concatenate(scores, axis=0)
    sink_all = jnp.concatenate([sink] * len(problems), axis=0)
    mx = jnp.maximum(jnp.max(s, axis=-1, keepdims=True), sink_all)
    p = jnp.exp(s - mx)
    den = jnp.sum(p, axis=-1, keepdims=True) + jnp.exp(sink_all - mx)
    p = p * (1.0 / den)

    if between is not None:
        between()

    outs = []
    for i in range(len(problems)):
        slabs = []
        for kh in range(n_kv):
            c = i * n_kv + kh
            pg = p[c * group:(c + 1) * group].reshape(group * rows, KEY_SLOTS).astype(BF16)
            o = jnp.dot(pg, values[c], preferred_element_type=F32).reshape(group, rows, gw)
            out = o[0]
            for g in range(1, group):
                out = jnp.where(head_of_lane == g, o[g], out)
            slabs.append(out)
        outs.append(jnp.concatenate(slabs, axis=1))
    return outs


def _sink_column(sink_ref):
    return sink_ref[...][:, :, 0:1]


def _attn_sample_kernel(seq, n_kv, group, head_dim, sink_ref, q_ref, kc_ref, kn_ref, vc_ref, vn_ref,
                        _, o_ref):
    n_seq = kc_ref.shape[0]
    d_kv = kn_ref.shape[1]
    per_block = SUBLANES // seq
    pad = jnp.zeros((KEY_SLOTS - WINDOW - SUBLANES, d_kv), F32)
    qf = q_ref[...].astype(F32)
    problems = []
    for i in range(n_seq):
        blk, row0 = i // per_block, (i % per_block) * seq
        rows = slice(blk * SUBLANES, (blk + 1) * SUBLANES)
        k8, v8 = kn_ref[rows, :], vn_ref[rows, :]
        if row0:
            k8 = pltpu.roll(k8, SUBLANES - row0, 0)
            v8 = pltpu.roll(v8, SUBLANES - row0, 0)
        kband = jnp.concatenate([kc_ref[i], k8, pad], axis=0)
        vband = jnp.concatenate([vc_ref[i], v8, pad], axis=0)
        problems.append((qf[rows, :], kband, vband, row0, PAST_LEN - WINDOW))
    outs = _band_attention(problems, _sink_column(sink_ref), n_kv, group, head_dim)
    r_idx = lax.broadcasted_iota(jnp.int32, outs[0].shape, 0)
    blocks = []
    for blk in range(n_seq // per_block):
        out = outs[blk * per_block]
        for s in range(1, per_block):
            out = jnp.where(r_idx >= s * seq, outs[blk * per_block + s], out)
        blocks.append(out)
    o_ref[...] = jnp.concatenate(blocks, axis=0).astype(BF16)


def _attn_sample(attn, q, k, v, k_cache, v_cache, sink_tab, layer, *, row0, seq, seqs_per_step, n_kv,
                 group, head_dim):
    d_q = q.shape[1]
    d_kv = k.shape[1]
    nbatch = k_cache.shape[1]
    ns = seqs_per_step
    rows = ns * seq
    blk0 = row0 // rows
    new = lambda b: (blk0 + b, 0)
    cache = lambda b: (layer, b, 0, 0)
    return pl.pallas_call(
        functools.partial(_attn_sample_kernel, seq, n_kv, group, head_dim),
        grid=(nbatch // ns,),
        in_specs=[
            pl.BlockSpec((None, n_kv * group, 1, LANES), lambda b: (layer, 0, 0, 0)),
            pl.BlockSpec((rows, d_q), new),
            pl.BlockSpec((None, ns, WINDOW, d_kv), cache),
            pl.BlockSpec((rows, d_kv), new),
            pl.BlockSpec((None, ns, WINDOW, d_kv), cache),
            pl.BlockSpec((rows, d_kv), new),
            pl.BlockSpec(memory_space=pl.ANY),
        ],
        out_specs=pl.BlockSpec((rows, d_q), new),
        out_shape=jax.ShapeDtypeStruct(attn.shape, attn.dtype),
        input_output_aliases={6: 0},
        compiler_params=_params(1),
        name="attn_sample",
    )(sink_tab, q, k_cache, k, v_cache, v, attn)


def _conv_attn_kernel(nc, col0, blocks_per_seq, n_kv, group, head_dim,
                      xb_ref, w_hbm, cw_ref, t_ref, f1_ref, f2_ref,
                      sink_ref, q_ref, kp_ref, kc_ref, vp_ref, vc_ref,
                      y_ref, u_ref, o_ref, ubuf, w_res, w_sems):
    tm = xb_ref.shape[0]
    tc = w_res.shape[2]
    i = pl.program_id(0)
    c = pl.program_id(1)

    @pl.when((i == 0) & (c == 0))
    def _():
        copies = [pltpu.make_async_copy(w_hbm.at[:, pl.ds(col0 + b * tc, tc)], w_res.at[b], w_sems.at[b])
                  for b in range(w_res.shape[0])]
        for cp in copies:
            cp.start()
        for cp in copies:
            cp.wait()

    wb_ref, wc_ref, wh_ref = (w_res.at[k * nc + c] for k in range(3))

    @pl.when(i == 0)
    def _():
        ubuf[c] = jnp.zeros(ubuf.shape[1:], F32)

    carry = ubuf[c]

    n_blocks = tm // WINDOW
    per_step = n_blocks // nc
    problems, starts = [], []
    for s in range(per_step):
        b = c * per_step + s
        r0 = pl.multiple_of(b * WINDOW, WINDOW)
        rp = pl.multiple_of(jnp.maximum(b - 1, 0) * WINDOW, WINDOW)
        n = lax.rem(i * n_blocks + b, blocks_per_seq)
        k_before = jnp.where(b == 0, kp_ref[...], kc_ref[pl.ds(rp, WINDOW), :])
        v_before = jnp.where(b == 0, vp_ref[...], vc_ref[pl.ds(rp, WINDOW), :])
        kband = jnp.concatenate([k_before, kc_ref[pl.ds(r0, WINDOW), :]], axis=0)
        vband = jnp.concatenate([v_before, vc_ref[pl.ds(r0, WINDOW), :]], axis=0)
        problems.append((q_ref[pl.ds(r0, WINDOW), :].astype(F32), kband, vband, 0, (n - 1) * WINDOW))
        starts.append(r0)
    proj = []

    def conv_input_matmuls():
        xb = xb_ref[...]
        for w_ref in (wc_ref, wh_ref):
            proj.append(jnp.dot(xb, w_ref[...], preferred_element_type=F32))

    outs = _band_attention(problems, _sink_column(sink_ref), n_kv, group, head_dim,
                           between=conv_input_matmuls)
    cc, ch = proj
    cb = jnp.dot(xb_ref[...], wb_ref[...], preferred_element_type=F32)
    u = cc * ch
    row8 = lax.broadcasted_iota(jnp.int32, carry.shape, 0)

    def shifted(k):
        r = pltpu.roll(u, k, 0)
        head = jnp.where(row8 < k, pltpu.roll(carry, k, 0), r[0:SUBLANES])
        return jnp.concatenate([head, r[SUBLANES:]], axis=0)

    t = t_ref[...]
    last = i == pl.num_programs(0) - 1
    u_m1 = jnp.where(t >= 1, shifted(1), jnp.where(last, f1_ref[...], 0.0))
    u_m2 = jnp.where(t >= 2, shifted(2), jnp.where(last, f2_ref[...], 0.0))
    cw = cw_ref[...]
    conv = cw[0:1, :] * u_m2 + cw[1:2, :] * u_m1 + cw[2:3, :] * u

    for r0, out in zip(starts, outs):
        o_ref[pl.ds(r0, WINDOW), :] = out.astype(BF16)
    y_ref[...] = (cb * conv).astype(BF16)
    u_ref[...] = u
    ubuf[c] = u[tm - SUBLANES:, :]


def _conv_attn(xb, w_in, conv_w, t_idx, fill1, fill2, q, k, v, sink_tab, layer, *, tm, tc, col0, d_conv,
               seq, n_kv, group, head_dim):
    m, d = xb.shape
    d_q = q.shape[1]
    d_kv = k.shape[1]
    nc = d_conv // tc
    n_blocks = tm // WINDOW
    tile = lambda i, c: (i, 0)
    before = lambda i, c: (jnp.maximum(i * n_blocks - 1, 0), 0)
    return pl.pallas_call(
        functools.partial(_conv_attn_kernel, nc, col0, seq // WINDOW, n_kv, group, head_dim),
        grid=(pl.cdiv(m, tm), nc),
        in_specs=[
            pl.BlockSpec((tm, d), tile),
            pl.BlockSpec(memory_space=pl.ANY),
            pl.BlockSpec((None, CONV_W, tc), lambda i, c: (layer, 0, c)),
            pl.BlockSpec((tm, 1), tile),
            pl.BlockSpec((tm, tc), lambda i, c: (0, c)),
            pl.BlockSpec((tm, tc), lambda i, c: (0, c)),
            pl.BlockSpec((None, n_kv * group, 1, LANES), lambda i, c: (layer, 0, 0, 0)),
            pl.BlockSpec((tm, d_q), tile),
            pl.BlockSpec((WINDOW, d_kv), before),
            pl.BlockSpec((tm, d_kv), tile),
            pl.BlockSpec((WINDOW, d_kv), before),
            pl.BlockSpec((tm, d_kv), tile),
        ],
        out_specs=[
            pl.BlockSpec((tm, tc), lambda i, c: (i, c)),
            pl.BlockSpec((tm, tc), lambda i, c: (i, c)),
            pl.BlockSpec((tm, d_q), tile),
        ],
        out_shape=[
            jax.ShapeDtypeStruct((m, d_conv), BF16),
            jax.ShapeDtypeStruct((m, d_conv), F32),
            jax.ShapeDtypeStruct((m, d_q), BF16),
        ],
        scratch_shapes=[
            pltpu.VMEM((nc, SUBLANES, tc), F32),
            pltpu.VMEM((3 * nc, d, tc), BF16),
            pltpu.SemaphoreType.DMA((3 * nc,)),
        ],
        compiler_params=_params(2),
        name="conv_attn",
    )(xb, w_in, conv_w, t_idx, fill1, fill2, sink_tab, q, k, k, v, v)


def _mix_ln_kernel(alpha, x_ref, xb_ref, a_ref, c_ref, wga_ref, wgc_ref, wa_ref, wc_ref, wo_ref, g_ref,
                   b_ref, o_ref, acc_ref):
    j = pl.program_id(1)

    @pl.when(j == 0)
    def _():
        acc_ref[...] = jnp.zeros_like(acc_ref)

    xb = xb_ref[...]
    ga = jnp.dot(xb, wga_ref[...], preferred_element_type=F32)
    gc = jnp.dot(xb, wgc_ref[...], preferred_element_type=F32)
    pa = jnp.dot(a_ref[...], wa_ref[...], preferred_element_type=F32)
    pc = jnp.dot(c_ref[...], wc_ref[...], preferred_element_type=F32)
    merged = (jax.nn.sigmoid(ga) * pa + jax.nn.sigmoid(gc) * pc).astype(BF16)
    acc_ref[...] += jnp.dot(merged, wo_ref[...], preferred_element_type=F32)

    @pl.when(j == pl.num_programs(1) - 1)
    def _():
        y = alpha * x_ref[...] + acc_ref[...]
        o_ref[...] = _layer_norm(y, g_ref[...], b_ref[...])


def _mix_ln(x, xb, attn, yconv, w_in, w_a, w_c, w_o, ln_g, ln_b, ln_idx, alpha, *, tm, tc, ga_col0,
            gc_col0):
    m, d = x.shape
    d_q = attn.shape[1]
    d_conv = yconv.shape[1]
    return pl.pallas_call(
        functools.partial(_mix_ln_kernel, alpha),
        grid=(m // tm, d // tc),
        in_specs=[
            pl.BlockSpec((tm, d), lambda i, j: (i, 0)),
            pl.BlockSpec((tm, d), lambda i, j: (i, 0)),
            pl.BlockSpec((tm, d_q), lambda i, j: (i, 0)),
            pl.BlockSpec((tm, d_conv), lambda i, j: (i, 0)),
            pl.BlockSpec((d, tc), lambda i, j: (0, ga_col0 // tc + j)),
            pl.BlockSpec((d, tc), lambda i, j: (0, gc_col0 // tc + j)),
            pl.BlockSpec((d_q, tc), lambda i, j: (0, j)),
            pl.BlockSpec((d_conv, tc), lambda i, j: (0, j)),
            pl.BlockSpec((tc, d), lambda i, j: (j, 0)),
            pl.BlockSpec((None, 1, d), lambda i, j: (ln_idx, 0, 0)),
            pl.BlockSpec((None, 1, d), lambda i, j: (ln_idx, 0, 0)),
        ],
        out_specs=pl.BlockSpec((tm, d), lambda i, j: (i, 0)),
        out_shape=jax.ShapeDtypeStruct((m, d), F32),
        scratch_shapes=[pltpu.VMEM((tm, d), F32)],
        compiler_params=_params(2),
        name="mix_ln",
    )(x, xb, attn, yconv, w_in, w_in, w_a, w_c, w_o, ln_g, ln_b)


def _rope_tables(pos, head_dim):
    inv_freq = ROPE_THETA ** (-jnp.arange(0, head_dim, 2, dtype=F32) / head_dim)
    ang = pos.astype(F32)[:, None] * inv_freq[None, :]
    cos = jnp.cos(ang)
    sin = jnp.sin(ang)
    reps = LANES // head_dim
    cos_tab = jnp.tile(jnp.concatenate([cos, cos], axis=1), (1, reps))
    sin_tab = jnp.tile(jnp.concatenate([-sin, sin], axis=1), (1, reps))
    return cos_tab, sin_tab


def _largest_divisor(n, cap, multiple_of=1):
    return max(k for k in range(multiple_of, cap + 1, multiple_of) if n % k == 0)


def _seq_tails(a, n_seq, seq, rows):
    return jnp.stack([a[(b + 1) * seq - rows:(b + 1) * seq] for b in range(n_seq)])


def kernel(x_prompt, x_sample, cache_k_win, cache_v_win, state_conv, ln_g, ln_b, w_in, sinks, conv_w,
           w_branch_attn, w_branch_conv, w_out, ffn1_gu, ffn1_down, ffn2_gu, ffn2_down):
    depth = w_in.shape[0]
    bp, tp, d = x_prompt.shape
    bs, ts, _ = x_sample.shape
    n_kv, head_dim = cache_k_win.shape[-2:]
    d_q = w_branch_attn.shape[1]
    d_conv = conv_w.shape[-1]
    d_kv = n_kv * head_dim
    group = d_q // d_kv
    mp, ms = bp * tp, bs * ts
    m = mp + ms
    alpha = (2.0 * depth) ** 0.25
    tm = _largest_divisor(m, MAX_ROW_TILE, BF16_ROWS)
    ns = _largest_divisor(bs, SAMPLE_SEQS_PER_STEP, SUBLANES // ts)
    assert tm >= ms and SUBLANES % ts == 0 and mp % (ns * ts) == 0 and tp % WINDOW == 0
    assert ts >= CONV_W - 1 and tp >= WINDOW
    tf = 512
    tc = 512
    tm_mix = MIX_ROW_TILE
    assert mp % tm_mix == 0 and ms <= tm_mix and tm_mix % (WINDOW * (d_conv // tc)) == 0
    conv_col0 = d_q + 2 * d_kv
    ga_col0 = conv_col0 + 3 * d_conv
    gc_col0 = ga_col0 + d

    ln_g = ln_g.reshape(depth * 3, 1, d)
    ln_b = ln_b.reshape(depth * 3, 1, d)
    sink_tab = jnp.broadcast_to(sinks[:, :, None, None], sinks.shape + (1, LANES))

    t_prompt = jnp.tile(jnp.arange(tp, dtype=jnp.int32), bp)
    t_sample = jnp.tile(jnp.arange(ts, dtype=jnp.int32), bs)
    cos_tab, sin_tab = _rope_tables(jnp.concatenate([t_prompt, PAST_LEN + t_sample]), head_dim)
    t_idx = jnp.concatenate([t_prompt, t_sample]).reshape(m, 1)
    k_cache = cache_k_win.reshape(depth, bs, WINDOW, d_kv)
    v_cache = cache_v_win.reshape(depth, bs, WINDOW, d_kv)

    f1_gu, f1_down = ffn1_gu[0].astype(BF16), ffn1_down[0].astype(BF16)
    x = x_prompt.reshape(mp, d)
    x_tail = x_sample.reshape(ms, d)
    ks_p, vs_p, cs_p, ks_s, vs_s, cs_s = [], [], [], [], [], []
    for l in range(depth):
        jobs = [(w, l) for w in (w_in, w_branch_attn, w_branch_conv, w_out, ffn2_gu, ffn2_down)]
        x1, x1b, w_in_b, w_a_b, w_c_b, w_o_b, f2_gu, f2_down = _ffn_ln(
            x, f1_gu, f1_down, ln_g, ln_b, 3 * l, alpha, tm=tm, tf=tf, emit_bf16=True, x_tail=x_tail,
            cast_jobs=jobs)
        q, k, v = _qkv_proj(x1b, w_in_b, cos_tab, sin_tab, tm=tm, d_q=d_q, d_kv=d_kv, head_dim=head_dim)
        st = state_conv[l]
        zeros = jnp.zeros((bs, ts, d_conv), F32)
        fill1 = zeros.at[:, 0].set(st[:, 1]).reshape(ms, d_conv)
        fill2 = zeros.at[:, 0].set(st[:, 0]).at[:, 1].set(st[:, 1]).reshape(ms, d_conv)
        fill1 = jnp.pad(fill1, ((0, tm_mix - ms), (0, 0)))
        fill2 = jnp.pad(fill2, ((0, tm_mix - ms), (0, 0)))
        yconv, u, attn = _conv_attn(x1b, w_in_b, conv_w, t_idx, fill1, fill2, q, k, v, sink_tab, l,
                                    tm=tm_mix, tc=tc, col0=conv_col0, d_conv=d_conv, seq=tp, n_kv=n_kv,
                                    group=group, head_dim=head_dim)
        attn = _attn_sample(attn, q, k, v, k_cache, v_cache, sink_tab, l, row0=mp, seq=ts,
                            seqs_per_step=ns, n_kv=n_kv, group=group, head_dim=head_dim)
        x2 = _mix_ln(x1, x1b, attn, yconv, w_in_b, w_a_b, w_c_b, w_o_b, ln_g, ln_b, 3 * l + 1, alpha,
                     tm=tm, tc=tc, ga_col0=ga_col0, gc_col0=gc_col0)
        if l + 1 < depth:
            x, f1_gu, f1_down = _ffn_ln(x2, f2_gu, f2_down, ln_g, ln_b, 3 * l + 2, alpha, tm=tm, tf=tf,
                                        cast_jobs=[(ffn1_gu, l + 1), (ffn1_down, l + 1)])
            x_tail = None
        else:
            y_p, y_s = _ffn_ln(x2, f2_gu, f2_down, ln_g, ln_b, 3 * l + 2, alpha, tm=tm, tf=tf,
                               split_rows=mp)

        ks_p.append(_seq_tails(k, bp, tp, WINDOW).reshape(bp, WINDOW, n_kv, head_dim))
        vs_p.append(_seq_tails(v, bp, tp, WINDOW).reshape(bp, WINDOW, n_kv, head_dim))
        cs_p.append(_seq_tails(u, bp, tp, CONV_W - 1))
        k_new = k[mp:].reshape(bs, ts, n_kv, head_dim)
        v_new = v[mp:].reshape(bs, ts, n_kv, head_dim)
        ks_s.append(jnp.concatenate([cache_k_win[l][:, ts:], k_new], axis=1))
        vs_s.append(jnp.concatenate([cache_v_win[l][:, ts:], v_new], axis=1))
        cs_s.append(u[mp:].reshape(bs, ts, d_conv)[:, -(CONV_W - 1):])

    return (y_p.reshape(bp, tp, d), y_s.reshape(bs, ts, d), jnp.stack(ks_p), jnp.stack(vs_p),
            jnp.stack(cs_p), jnp.stack(ks_s), jnp.stack(vs_s), jnp.stack(cs_s))
```
